```python
import jax, jax.numpy as jnp
from jax import lax
import numpy as np

D_MODEL = 1024
BATCH = 32
SEQ = 256
DEPTH = 1
DEC_BATCH = 8
DEC_SEQ = 1024
PAST_LEN = 256

GRID_W = 64
MIX_WIDTH = D_MODEL
CONV_WIDTH = MIX_WIDTH // 2
SSD_WIDTH = MIX_WIDTH - CONV_WIDTH
SSD_HEAD_DIM = 64
SSD_HEADS = SSD_WIDTH // SSD_HEAD_DIM
SSD_GROUPS = 2
HEADS_PER_GROUP = SSD_HEADS // SSD_GROUPS
D_STATE = 128
CHUNK = 128
KSIZE = 3
D_FF = 2816
N_MOD = 6
EPS = 1e-6
SSD_CONV_CH = SSD_WIDTH + 2 * SSD_GROUPS * D_STATE
D_IN_PROJ = 3 * CONV_WIDTH + 2 * SSD_WIDTH + 2 * SSD_GROUPS * D_STATE + 2 * SSD_HEADS

kernel_name = "hybrid_shortconv_ssd_convffn_diffusion_step"


def rmsnorm(x, g):
    xf = x.astype(jnp.float32)
    r = lax.rsqrt(jnp.mean(xf * xf, axis=-1, keepdims=True) + EPS)
    return (xf * r).astype(x.dtype) * g


def dwconv1d(x, w, b=None):
    L = x.shape[1]
    xp = jnp.pad(x, ((0, 0), (1, 1), (0, 0)))
    y = xp[:, 0:L] * w[0] + xp[:, 1:L + 1] * w[1] + xp[:, 2:L + 2] * w[2]
    return y if b is None else y + b


def dwconv2d_grid(x, w, b):
    bsz, L, ch = x.shape
    rows = L // GRID_W
    xg = jnp.pad(x.reshape(bsz, rows, GRID_W, ch), ((0, 0), (1, 1), (1, 1), (0, 0)))
    y = b
    for di in range(KSIZE):
        for dj in range(KSIZE):
            y = y + xg[:, di:di + rows, dj:dj + GRID_W] * w[di, dj]
    return y.reshape(bsz, L, ch)


def ssd_scan(x, dt, a, bm, cm, init_state):
    f32 = jnp.float32
    bsz, l = x.shape[0], x.shape[1]
    nc = l // CHUNK
    xdt = x.astype(f32) * dt[..., None]
    bh = jnp.repeat(bm.astype(f32), HEADS_PER_GROUP, axis=2)
    chh = jnp.repeat(cm.astype(f32), HEADS_PER_GROUP, axis=2)
    x_c = xdt.reshape(bsz, nc, CHUNK, SSD_HEADS, SSD_HEAD_DIM)
    b_c = bh.reshape(bsz, nc, CHUNK, SSD_HEADS, D_STATE)
    c_c = chh.reshape(bsz, nc, CHUNK, SSD_HEADS, D_STATE)
    da = (dt * a).reshape(bsz, nc, CHUNK, SSD_HEADS).transpose(0, 3, 1, 2)
    cs = jnp.cumsum(da, axis=-1)
    causal = jnp.tril(jnp.ones((CHUNK, CHUNK), dtype=bool))
    decay_in = jnp.exp(jnp.where(causal, cs[..., :, None] - cs[..., None, :], -jnp.inf))
    scores = jnp.einsum('bclhn,bcshn->bhcls', c_c, b_c) * decay_in
    y_diag = jnp.einsum('bhcls,bcshp->bclhp', scores, x_c)
    decay_to_end = jnp.exp(cs[..., -1:] - cs)
    chunk_states = jnp.einsum('bcshn,bhcs,bcshp->bchpn', b_c, decay_to_end, x_c)
    chunk_decay = jnp.exp(cs[..., -1])

    def step(state, inp):
        dec, add = inp
        return dec[..., None, None] * state + add, state

    final, starts = lax.scan(step, init_state.astype(f32),
                             (jnp.moveaxis(chunk_decay, 2, 0), jnp.moveaxis(chunk_states, 1, 0)))
    y_off = jnp.einsum('bclhn,cbhpn,bhcl->bclhp', c_c, starts, jnp.exp(cs))
    y = (y_diag + y_off).reshape(bsz, l, SSD_HEADS, SSD_HEAD_DIM)
    return y, final


def mixer(h, w_in, w_conv_short, w_conv_ssd, b_conv_ssd, dt_bias, a_log, d_skip,
          g_ssd_norm, w_out, init_f, init_b):
    bsz, l = h.shape[0], h.shape[1]
    proj = h @ w_in
    sizes = [CONV_WIDTH, CONV_WIDTH, CONV_WIDTH, SSD_WIDTH, SSD_CONV_CH, SSD_HEADS, SSD_HEADS]
    offs = [int(v) for v in np.cumsum(sizes)[:-1]]
    hc, gb, gc, z, xbc, dtf, dtb = jnp.split(proj, offs, axis=-1)
    out_a = gb * dwconv1d(gc * hc, w_conv_short)
    xbc = jax.nn.silu(dwconv1d(xbc, w_conv_ssd, b_conv_ssd))
    xs, bm, cm = jnp.split(xbc, [SSD_WIDTH, SSD_WIDTH + SSD_GROUPS * D_STATE], axis=-1)
    xh = xs.reshape(bsz, l, SSD_HEADS, SSD_HEAD_DIM)
    bm = bm.reshape(bsz, l, SSD_GROUPS, D_STATE)
    cm = cm.reshape(bsz, l, SSD_GROUPS, D_STATE)
    dt_f = jax.nn.softplus(dtf.astype(jnp.float32) + dt_bias[0].astype(jnp.float32))
    dt_b = jax.nn.softplus(dtb.astype(jnp.float32) + dt_bias[1].astype(jnp.float32))
    a = -jnp.exp(a_log.astype(jnp.float32))
    y_f, s_f = ssd_scan(xh, dt_f, a[0], bm, cm, init_f)
    y_b_rev, s_b = ssd_scan(xh[:, ::-1], dt_b[:, ::-1], a[1], bm[:, ::-1], cm[:, ::-1], init_b)
    y = y_f + y_b_rev[:, ::-1] + xh.astype(jnp.float32) * d_skip.astype(jnp.float32)[:, None]
    y = y.astype(h.dtype).reshape(bsz, l, SSD_WIDTH)
    y = rmsnorm(y * jax.nn.silu(z), g_ssd_norm)
    out = jnp.concatenate([out_a, y], axis=-1) @ w_out
    return out, s_f.astype(h.dtype), s_b.astype(h.dtype)


def layer(x, mod, is_latent, init_f, init_b, g_norm1, g_norm2, w_in, w_conv_short, w_conv_ssd,
          b_conv_ssd, dt_bias, a_log, d_skip, g_ssd_norm, w_out, w_up, w_ffn_conv, b_ffn_conv, w_down):
    shift_m, scale_m, gate_m = mod[:, :, 0], mod[:, :, 1], mod[:, :, 2]
    shift_f, scale_f, gate_f = mod[:, :, 3], mod[:, :, 4], mod[:, :, 5]
    h = rmsnorm(x, g_norm1) * (1.0 + scale_m) + shift_m
    mix, s_f, s_b = mixer(h, w_in, w_conv_short, w_conv_ssd, b_conv_ssd, dt_bias, a_log, d_skip,
                          g_ssd_norm, w_out, init_f, init_b)
    x = x + gate_m * mix
    h2 = rmsnorm(x, g_norm2) * (1.0 + scale_f) + shift_f
    u = h2 @ w_up
    if is_latent:
        u = dwconv2d_grid(u, w_ffn_conv, b_ffn_conv)
    else:
        u = dwconv1d(u, w_ffn_conv[1], b_ffn_conv)
    ug, uv = jnp.split(u, 2, axis=-1)
    x = x + gate_f * ((jax.nn.silu(ug) * uv) @ w_down)
    return x, s_f, s_b


def setup_inputs(seed: int = 0) -> dict:
    key = jax.random.key(seed)
    ks = jax.random.split(key, 32)
    f32 = jnp.float32
    nrm = lambda k, shp, s: jax.random.normal(k, shp, f32) * s
    dt0 = jnp.exp(jax.random.uniform(ks[14], (DEPTH, 2, SSD_HEADS), f32, np.log(1e-3), np.log(1e-1)))
    return {
        "x_prompt": nrm(ks[0], (BATCH, SEQ, D_MODEL), 1.0),
        "x_sample": nrm(ks[1], (DEC_BATCH, DEC_SEQ, D_MODEL), 1.0),
        "state_ssd_fwd": nrm(ks[2], (DEC_BATCH, DEPTH, SSD_HEADS, SSD_HEAD_DIM, D_STATE), 0.5),
        "state_ssd_bwd": nrm(ks[3], (DEC_BATCH, DEPTH, SSD_HEADS, SSD_HEAD_DIM, D_STATE), 0.5),
        "c": nrm(ks[4], (DEC_BATCH, D_MODEL), 1.0),
        "c_ctx": nrm(ks[5], (D_MODEL,), 1.0),
        "g_norm1": 1.0 + nrm(ks[6], (DEPTH, D_MODEL), 0.02),
        "g_norm2": 1.0 + nrm(ks[7], (DEPTH, D_MODEL), 0.02),
        "w_ada": nrm(ks[8], (DEPTH, D_MODEL, N_MOD * D_MODEL), 0.5 * D_MODEL ** -0.5),
        "b_ada": nrm(ks[9], (DEPTH, N_MOD * D_MODEL), 0.02),
        "w_in": nrm(ks[10], (DEPTH, D_MODEL, D_IN_PROJ), D_MODEL ** -0.5),
        "w_conv_short": nrm(ks[11], (DEPTH, KSIZE, CONV_WIDTH), KSIZE ** -0.5),
        "w_conv_ssd": nrm(ks[12], (DEPTH, KSIZE, SSD_CONV_CH), KSIZE ** -0.5),
        "b_conv_ssd": nrm(ks[13], (DEPTH, SSD_CONV_CH), 0.02),
        "dt_bias": jnp.log(jnp.expm1(dt0)),
        "a_log": jnp.log(jax.random.uniform(ks[15], (DEPTH, 2, SSD_HEADS), f32, 1.0, 16.0)),
        "d_skip": 1.0 + nrm(ks[16], (DEPTH, SSD_HEADS), 0.1),
        "g_ssd_norm": 1.0 + nrm(ks[17], (DEPTH, SSD_WIDTH), 0.02),
        "w_out": nrm(ks[18], (DEPTH, MIX_WIDTH, D_MODEL), MIX_WIDTH ** -0.5),
        "w_up": nrm(ks[19], (DEPTH, D_MODEL, 2 * D_FF), D_MODEL ** -0.5),
        "w_ffn_conv": nrm(ks[20], (DEPTH, KSIZE, KSIZE, 2 * D_FF), KSIZE ** -1.0),
        "b_ffn_conv": nrm(ks[21], (DEPTH, 2 * D_FF), 0.02),
        "w_down": nrm(ks[22], (DEPTH, D_FF, D_MODEL), D_FF ** -0.5),
        "g_final": 1.0 + nrm(ks[23], (D_MODEL,), 0.02),
    }


def reference(x_prompt, x_sample, state_ssd_fwd, state_ssd_bwd, c, c_ctx, g_norm1, g_norm2,
              w_ada, b_ada, w_in, w_conv_short, w_conv_ssd, b_conv_ssd, dt_bias, a_log, d_skip,
              g_ssd_norm, w_out, w_up, w_ffn_conv, b_ffn_conv, w_down, g_final):
    xp = x_prompt
    xs = x_sample
    bp = x_prompt.shape[0]
    zero_state = jnp.zeros((bp, SSD_HEADS, SSD_HEAD_DIM, D_STATE), x_prompt.dtype)
    new_f, new_b = [], []
    for i in range(DEPTH):
        mod_ctx = (jax.nn.silu(c_ctx)[None] @ w_ada[i] + b_ada[i]).reshape(1, 1, N_MOD, D_MODEL)
        mod_lat = (jax.nn.silu(c) @ w_ada[i] + b_ada[i]).reshape(c.shape[0], 1, N_MOD, D_MODEL)
        wl = (g_norm1[i], g_norm2[i], w_in[i], w_conv_short[i], w_conv_ssd[i], b_conv_ssd[i],
              dt_bias[i], a_log[i], d_skip[i], g_ssd_norm[i], w_out[i], w_up[i], w_ffn_conv[i],
              b_ffn_conv[i], w_down[i])
        xp, s_f, s_b = layer(xp, mod_ctx, False, zero_state, zero_state, *wl)
        new_f.append(s_f)
        new_b.append(s_b)
        xs, _, _ = layer(xs, mod_lat, True, state_ssd_fwd[:, i], state_ssd_bwd[:, i], *wl)
    y_prompt = rmsnorm(xp, g_final)
    y_sample = rmsnorm(xs, g_final)
    new_state_ssd_fwd = jnp.stack(new_f, axis=1)
    new_state_ssd_bwd = jnp.stack(new_b, axis=1)
    return (y_prompt, y_sample, new_state_ssd_fwd, new_state_ssd_bwd)
```

```python
import functools

import jax
import jax.numpy as jnp
from jax import lax
from jax.experimental import pallas as pl
from jax.experimental.pallas import tpu as pltpu

F32 = jnp.float32
BF16 = jnp.bfloat16

D_MODEL = 1024
GRID_W = 64
CONV_WIDTH = 512
SSD_WIDTH = 512
SSD_HEAD_DIM = 64
SSD_HEADS = 8
SSD_GROUPS = 2
D_STATE = 128
CHUNK = 128
D_FF = 2816
N_MOD = 6
EPS = 1e-6

SLAB = 1024
N_CHUNKS = SLAB // CHUNK
ROW_BLOCK = 256
PAD = 8
FF_BLOCK = 256
N_FF_BLOCKS = D_FF // FF_BLOCK
LANES = 128
DT_COL = 3072
D_IN_PAD = DT_COL + LANES
VMEM_LIMIT = 56 * 1024 * 1024


def _silu(v):
    return v / (1.0 + jnp.exp(-v))


def _softplus(v):
    return jnp.maximum(v, 0.0) + jnp.log1p(jnp.exp(-jnp.abs(v)))


def _split3(v):
    hi = v.astype(BF16)
    r1 = v - hi.astype(F32)
    mid = r1.astype(BF16)
    lo = (r1 - mid.astype(F32)).astype(BF16)
    return hi, mid, lo


def _dot(a, b):
    return jnp.dot(a, b, preferred_element_type=F32)


def _dot3(m, v):
    hi, mid, lo = _split3(v)
    return _dot(m, hi) + _dot(m, mid) + _dot(m, lo)


def _expand3(v, expand):
    hi, mid, lo = _split3(v)
    return _dot(hi, expand) + _dot(mid, expand) + _dot(lo, expand)


def _conv3_rows(win, w_ref, start_row, end_row):
    n = win.shape[0]
    row = lax.broadcasted_iota(jnp.int32, (CHUNK, 1), 0)
    prev = pltpu.roll(win, 1, 0)[PAD:PAD + CHUNK]
    cur = win[PAD:PAD + CHUNK]
    nxt = pltpu.roll(win, n - 1, 0)[PAD:PAD + CHUNK]
    prev = jnp.where(row == start_row, 0.0, prev)
    nxt = jnp.where(row == end_row, 0.0, nxt)
    return prev * w_ref[0:1, :] + cur * w_ref[1:2, :] + nxt * w_ref[2:3, :]


def _mod_kernel(c_ref, w_ref, b_ref, o_ref):
    s = _silu(c_ref[...]).astype(BF16)
    o_ref[...] = _dot(s, w_ref[...].astype(BF16)) + b_ref[...]


def _mod_call(cvec, w_ada, b_ada):
    rows = cvec.shape[0]
    return pl.pallas_call(
        _mod_kernel,
        grid=(N_MOD,),
        in_specs=[
            pl.BlockSpec((rows, D_MODEL), lambda j: (0, 0)),
            pl.BlockSpec((D_MODEL, D_MODEL), lambda j: (0, j)),
            pl.BlockSpec((1, D_MODEL), lambda j: (0, j)),
        ],
        out_specs=pl.BlockSpec((rows, D_MODEL), lambda j: (0, j)),
        out_shape=jax.ShapeDtypeStruct((rows, N_MOD * D_MODEL), F32),
        compiler_params=pltpu.CompilerParams(dimension_semantics=("arbitrary",)),
        name="mod_vectors",
    )(cvec, w_ada, b_ada)


def _mixer_kernel(seg_chunks, has_init, *refs):
    refs = list(refs)
    x_ref, mod_ref = refs[:2]
    refs = refs[2:]
    if has_init:
        initf_ref, initb_ref = refs[:2]
        refs = refs[2:]
    (g1_ref, win_ref, wcs_ref, wcx_ref, bcx_ref, dtb_ref, alog_ref, dskip_ref, gssd_ref, wout_ref,
     ltri_ref, utri_ref, exp_ref) = refs[:13]
    refs = refs[13:]
    out_ref = refs[0]
    refs = refs[1:]
    if not has_init:
        sfo_ref, sbo_ref = refs[:2]
        refs = refs[2:]
    (gchc_ref, xbc_ref, gb_ref, z_ref, dt_ref, e_ref, xs_ref, bc_ref, ymix_ref, sfin_ref, csb_ref,
     sf_ref, sb_ref) = refs

    shift_m = mod_ref[0, 0:1, :]
    scale_m = mod_ref[0, 1:2, :]
    gate_m = mod_ref[0, 2:3, :]

    zpad_a = jnp.zeros((PAD, CONV_WIDTH), F32)
    zpad_x = jnp.zeros((PAD, 2 * SSD_WIDTH), F32)
    gchc_ref[0:PAD, :] = zpad_a
    gchc_ref[SLAB + PAD:SLAB + 2 * PAD, :] = zpad_a
    xbc_ref[0:PAD, :] = zpad_x
    xbc_ref[SLAB + PAD:SLAB + 2 * PAD, :] = zpad_x

    def proj_body(i, carry):
        r0 = pl.multiple_of(i * ROW_BLOCK, ROW_BLOCK)
        rp = pl.multiple_of(i * ROW_BLOCK + PAD, PAD)
        xb = x_ref[pl.ds(r0, ROW_BLOCK), :]
        ms = jnp.mean(xb * xb, axis=-1, keepdims=True)
        h = (xb * lax.rsqrt(ms + EPS)) * g1_ref[...] * (1.0 + scale_m) + shift_m
        hb = h.astype(BF16)

        def proj(lo, hi):
            return _dot(hb, win_ref[:, lo:hi])

        gchc_ref[pl.ds(rp, ROW_BLOCK), :] = proj(0, 512) * proj(1024, 1536)
        gb_ref[pl.ds(r0, ROW_BLOCK), :] = proj(512, 1024)
        z_ref[pl.ds(r0, ROW_BLOCK), :] = proj(1536, 2048)
        xbc_ref[pl.ds(rp, ROW_BLOCK), :] = proj(2048, 3072)
        dt_ref[pl.ds(r0, ROW_BLOCK), :] = _softplus(proj(DT_COL, D_IN_PAD) + dtb_ref[...])
        return carry

    lax.fori_loop(0, SLAB // ROW_BLOCK, proj_body, 0)

    a_row = -jnp.exp(alog_ref[...])
    lane_t = lax.broadcasted_iota(jnp.int32, (CHUNK, LANES), 1)
    lane_1 = lax.broadcasted_iota(jnp.int32, (1, LANES), 1)
    is_fwd_t = lane_t < SSD_HEADS
    is_fwd_1 = lane_1 < SSD_HEADS

    def chunk_flags(c):
        pos = lax.rem(c, seg_chunks)
        is_start = pos == 0
        is_end = pos == seg_chunks - 1
        start_row = jnp.where(is_start, 0, -1)
        end_row = jnp.where(is_end, CHUNK - 1, -1)
        return is_start, is_end, start_row, end_row

    def chunk_decay_row(e):
        tot = jnp.where(is_fwd_1, e[CHUNK - 1:CHUNK, :], e[0:1, :])
        dec = jnp.broadcast_to(jnp.exp(tot), (8, LANES))
        return tot, _expand3(dec, exp_ref[...])[0:1, :]

    def fwd_body(c, carry):
        r0 = pl.multiple_of(c * CHUNK, CHUNK)
        rows = pl.ds(r0, CHUNK)
        is_start, is_end, start_row, end_row = chunk_flags(c)

        xc = _conv3_rows(xbc_ref[pl.ds(r0, CHUNK + 2 * PAD), :], wcx_ref, start_row, end_row)
        xc = _silu(xc + bcx_ref[...])
        xs = xc[:, 0:SSD_WIDTH]
        xs_ref[rows, :] = xs
        bc_ref[rows, :] = xc[:, SSD_WIDTH:].astype(BF16)

        ca = _conv3_rows(gchc_ref[pl.ds(r0, CHUNK + 2 * PAD), :], wcs_ref, start_row, end_row)
        ymix_ref[rows, 0:CONV_WIDTH] = (gb_ref[rows, :] * ca).astype(BF16)

        dtt = dt_ref[rows, :]
        da = dtt * a_row
        cs = _dot3(ltri_ref[...], da)
        rcs = _dot3(utri_ref[...], da)
        e = jnp.where(is_fwd_t, cs, rcs)
        e_ref[rows, :] = e
        tot, dec_row = chunk_decay_row(e)
        wst = (dtt * jnp.exp(tot - e)).astype(BF16)
        wexp = _dot(wst, exp_ref[...])
        xst_f = (xs * wexp[:, 0:SSD_WIDTH]).astype(BF16)
        xst_b = (xs * wexp[:, SSD_WIDTH:]).astype(BF16)

        cst_f, cst_b = [], []
        for g in range(SSD_GROUPS):
            bt = jnp.transpose(xc[:, SSD_WIDTH + g * D_STATE:SSD_WIDTH + (g + 1) * D_STATE]).astype(BF16)
            cst_f.append(_dot(bt, xst_f[:, g * 256:(g + 1) * 256]))
            cst_b.append(_dot(bt, xst_b[:, g * 256:(g + 1) * 256]))
        csb_ref[c] = jnp.concatenate(cst_b, axis=1)

        @pl.when(is_start)
        def _():
            if has_init:
                sf_ref[...] = jnp.transpose(initf_ref[0])
            else:
                sf_ref[...] = jnp.zeros_like(sf_ref)

        s_in = sf_ref[...]
        sfin_ref[c] = s_in
        s_new = s_in * dec_row[:, 0:SSD_WIDTH] + jnp.concatenate(cst_f, axis=1)
        sf_ref[...] = s_new

        if not has_init:
            @pl.when(is_end)
            def _():
                sfo_ref[lax.div(c, seg_chunks)] = jnp.transpose(s_new)

        return carry

    lax.fori_loop(0, N_CHUNKS, fwd_body, 0)

    row_t = lax.broadcasted_iota(jnp.int32, (CHUNK, CHUNK), 0)
    col_t = lax.broadcasted_iota(jnp.int32, (CHUNK, CHUNK), 1)
    causal = row_t >= col_t
    anti = col_t >= row_t
    low_half = lane_t < SSD_HEAD_DIM

    def bwd_body(k, carry):
        c = N_CHUNKS - 1 - k
        r0 = pl.multiple_of(c * CHUNK, CHUNK)
        rows = pl.ds(r0, CHUNK)
        is_start, is_end, _, _ = chunk_flags(c)

        @pl.when(is_end)
        def _():
            if has_init:
                sb_ref[...] = jnp.transpose(initb_ref[0])
            else:
                sb_ref[...] = jnp.zeros_like(sb_ref)

        e = e_ref[rows, :]
        et = jnp.transpose(e)
        dtt = dt_ref[rows, :]
        xs = xs_ref[rows, :]
        bc = bc_ref[rows, :]
        s_f = sfin_ref[c].astype(BF16)
        s_b_f32 = sb_ref[...]
        s_b = s_b_f32.astype(BF16)

        dtexp = _dot(dtt.astype(BF16), exp_ref[...])
        xdt_f = xs * dtexp[:, 0:SSD_WIDTH]
        xdt_b = xs * dtexp[:, SSD_WIDTH:]
        ecs = _dot(jnp.exp(e).astype(BF16), exp_ref[...])

        y_parts = []
        for g in range(SSD_GROUPS):
            b_g = bc[:, g * D_STATE:(g + 1) * D_STATE]
            c_g = bc[:, 256 + g * D_STATE:256 + (g + 1) * D_STATE]
            gmat = lax.dot_general(c_g, b_g, (((1,), (1,)), ((), ())), preferred_element_type=F32)
            cs_f = _dot(c_g, s_f[:, g * 256:(g + 1) * 256])
            cs_b = _dot(c_g, s_b[:, g * 256:(g + 1) * 256])
            y_off = (cs_f * ecs[:, g * 256:(g + 1) * 256]
                     + cs_b * ecs[:, SSD_WIDTH + g * 256:SSD_WIDTH + (g + 1) * 256])
            for pair in range(2):
                p0 = g * 256 + pair * LANES
                xf_pair = xdt_f[:, p0:p0 + LANES]
                xb_pair = xdt_b[:, p0:p0 + LANES]
                y_pair = y_off[:, pair * LANES:(pair + 1) * LANES]
                for sub in range(2):
                    head = g * 4 + pair * 2 + sub
                    keep = low_half if sub == 0 else jnp.logical_not(low_half)
                    rhs_f = jnp.where(keep, xf_pair, 0.0).astype(BF16)
                    rhs_b = jnp.where(keep, xb_pair, 0.0).astype(BF16)
                    jf, jb = head, SSD_HEADS + head
                    d_f = e[:, jf:jf + 1] - et[jf:jf + 1, :]
                    d_b = e[:, jb:jb + 1] - et[jb:jb + 1, :]
                    sc_f = (gmat * jnp.where(causal, jnp.exp(d_f), 0.0)).astype(BF16)
                    sc_b = (gmat * jnp.where(anti, jnp.exp(d_b), 0.0)).astype(BF16)
                    y_pair = y_pair + _dot(sc_f, rhs_f) + _dot(sc_b, rhs_b)
                y_parts.append(y_pair)
        y = jnp.concatenate(y_parts, axis=1) + xs * dskip_ref[...]
        yz = y * _silu(z_ref[rows, :])
        ms = jnp.mean(yz * yz, axis=-1, keepdims=True)
        yn = (yz * lax.rsqrt(ms + EPS)) * gssd_ref[...]
        ymix_ref[rows, CONV_WIDTH:] = yn.astype(BF16)

        _, dec_row = chunk_decay_row(e)
        s_new = s_b_f32 * dec_row[:, SSD_WIDTH:] + csb_ref[c]
        sb_ref[...] = s_new

        if not has_init:
            @pl.when(is_start)
            def _():
                sbo_ref[lax.div(c, seg_chunks)] = jnp.transpose(s_new)

        return carry

    lax.fori_loop(0, N_CHUNKS, bwd_body, 0)

    def out_body(i, carry):
        r0 = pl.multiple_of(i * ROW_BLOCK, ROW_BLOCK)
        rows = pl.ds(r0, ROW_BLOCK)
        o = _dot(ymix_ref[rows, :], wout_ref[...])
        out_ref[rows, :] = x_ref[rows, :] + gate_m * o
        return carry

    lax.fori_loop(0, SLAB // ROW_BLOCK, out_body, 0)


def _const_spec(shape):
    nd = len(shape)
    return pl.BlockSpec(shape, lambda i: (0,) * nd)


def _mixer_call(x2d, mod, mod_per_slab, seg_len, init_f, init_b, consts):
    n_slabs = x2d.shape[0] // SLAB
    seg_chunks = seg_len // CHUNK
    segs_per_slab = SLAB // seg_len
    has_init = init_f is not None
    state_rows = SSD_HEADS * SSD_HEAD_DIM

    mod_map = (lambda i: (i, 0, 0)) if mod_per_slab else (lambda i: (0, 0, 0))
    in_specs = [
        pl.BlockSpec((SLAB, D_MODEL), lambda i: (i, 0)),
        pl.BlockSpec((1, N_MOD, D_MODEL), mod_map),
    ]
    args = [x2d, mod]
    if has_init:
        in_specs += [pl.BlockSpec((1, state_rows, D_STATE), lambda i: (i, 0, 0))] * 2
        args += [init_f, init_b]
    in_specs += [_const_spec(a.shape) for a in consts]
    args += list(consts)

    out_specs = [pl.BlockSpec((SLAB, D_MODEL), lambda i: (i, 0))]
    out_shape = [jax.ShapeDtypeStruct(x2d.shape, F32)]
    if not has_init:
        n_seq = n_slabs * segs_per_slab
        out_specs += [pl.BlockSpec((segs_per_slab, state_rows, D_STATE), lambda i: (i, 0, 0))] * 2
        out_shape += [jax.ShapeDtypeStruct((n_seq, state_rows, D_STATE), F32)] * 2

    scratch = [
        pltpu.VMEM((SLAB + 2 * PAD, CONV_WIDTH), F32),
        pltpu.VMEM((SLAB + 2 * PAD, 2 * SSD_WIDTH), F32),
        pltpu.VMEM((SLAB, CONV_WIDTH), F32),
        pltpu.VMEM((SLAB, SSD_WIDTH), F32),
        pltpu.VMEM((SLAB, LANES), F32),
        pltpu.VMEM((SLAB, LANES), F32),
        pltpu.VMEM((SLAB, SSD_WIDTH), F32),
        pltpu.VMEM((SLAB, 2 * SSD_GROUPS * D_STATE), BF16),
        pltpu.VMEM((SLAB, D_MODEL), BF16),
        pltpu.VMEM((N_CHUNKS, D_STATE, SSD_WIDTH), F32),
        pltpu.VMEM((N_CHUNKS, D_STATE, SSD_WIDTH), F32),
        pltpu.VMEM((D_STATE, SSD_WIDTH), F32),
        pltpu.VMEM((D_STATE, SSD_WIDTH), F32),
    ]
    return pl.pallas_call(
        functools.partial(_mixer_kernel, seg_chunks, has_init),
        grid=(n_slabs,),
        in_specs=in_specs,
        out_specs=out_specs,
        out_shape=out_shape,
        scratch_shapes=scratch,
        compiler_params=pltpu.CompilerParams(
            dimension_semantics=("arbitrary",), vmem_limit_bytes=VMEM_LIMIT),
        name="mixer_latent" if has_init else "mixer_prompt",
    )(*args)


def _ffn_conv(u, wc_ref, bc_ref, is_grid):
    n = u.shape[0]
    row = lax.broadcasted_iota(jnp.int32, (n, 1), 0)
    period = GRID_W if is_grid else 256
    pos = jnp.bitwise_and(row, period - 1)
    um = jnp.where(pos == 0, 0.0, pltpu.roll(u, 1, 0))
    up = jnp.where(pos == period - 1, 0.0, pltpu.roll(u, n - 1, 0))

    def taps(di):
        return um * wc_ref[0, 3 * di:3 * di + 1, :] + u * wc_ref[0, 3 * di + 1:3 * di + 2, :] \
            + up * wc_ref[0, 3 * di + 2:3 * di + 3, :]

    out = taps(1) + bc_ref[0]
    if is_grid:
        zeros = jnp.zeros((GRID_W, u.shape[1]), F32)
        out = out + jnp.concatenate([zeros, taps(0)[:n - GRID_W]], axis=0)
        out = out + jnp.concatenate([taps(2)[GRID_W:], zeros], axis=0)
    return out


def _ffn_kernel(is_grid, x_ref, mod_ref, g2_ref, wup_ref, wc_ref, bc_ref, wdn_ref, gfin_ref, out_ref,
                h2_ref, acc_ref):
    j = pl.program_id(1)

    @pl.when(j == 0)
    def _():
        shift_f = mod_ref[0, 3:4, :]
        scale_f = mod_ref[0, 4:5, :]
        x = x_ref[...]
        ms = jnp.mean(x * x, axis=-1, keepdims=True)
        h2 = (x * lax.rsqrt(ms + EPS)) * g2_ref[...] * (1.0 + scale_f) + shift_f
        h2_ref[...] = h2.astype(BF16)
        acc_ref[...] = jnp.zeros_like(acc_ref)

    u = _dot(h2_ref[...], wup_ref[0])
    u = _ffn_conv(u, wc_ref, bc_ref, is_grid)
    a = (_silu(u[:, 0:FF_BLOCK]) * u[:, FF_BLOCK:]).astype(BF16)
    acc_ref[...] += _dot(a, wdn_ref[0])

    @pl.when(j == N_FF_BLOCKS - 1)
    def _():
        gate_f = mod_ref[0, 5:6, :]
        x2 = x_ref[...] + gate_f * acc_ref[...]
        ms = jnp.mean(x2 * x2, axis=-1, keepdims=True)
        out_ref[...] = (x2 * lax.rsqrt(ms + EPS)) * gfin_ref[...]


def _ffn_call(x2d, mod, mod_per_slab, is_grid, consts):
    n_slabs = x2d.shape[0] // SLAB
    g2, wup3, wc3, bc3, wdn3, gfin = consts
    mod_map = (lambda i, j: (i, 0, 0)) if mod_per_slab else (lambda i, j: (0, 0, 0))
    return pl.pallas_call(
        functools.partial(_ffn_kernel, is_grid),
        grid=(n_slabs, N_FF_BLOCKS),
        in_specs=[
            pl.BlockSpec((SLAB, D_MODEL), lambda i, j: (i, 0)),
            pl.BlockSpec((1, N_MOD, D_MODEL), mod_map),
            pl.BlockSpec((1, D_MODEL), lambda i, j: (0, 0)),
            pl.BlockSpec((1, D_MODEL, 2 * FF_BLOCK), lambda i, j: (j, 0, 0)),
            pl.BlockSpec((1, 9, 2 * FF_BLOCK), lambda i, j: (j, 0, 0)),
            pl.BlockSpec((1, 1, 2 * FF_BLOCK), lambda i, j: (j, 0, 0)),
            pl.BlockSpec((1, FF_BLOCK, D_MODEL), lambda i, j: (j, 0, 0)),
            pl.BlockSpec((1, D_MODEL), lambda i, j: (0, 0)),
        ],
        out_specs=pl.BlockSpec((SLAB, D_MODEL), lambda i, j: (i, 0)),
        out_shape=jax.ShapeDtypeStruct(x2d.shape, F32),
        scratch_shapes=[
            pltpu.VMEM((SLAB, D_MODEL), BF16),
            pltpu.VMEM((SLAB, D_MODEL), F32),
        ],
        compiler_params=pltpu.CompilerParams(
            dimension_semantics=("arbitrary", "arbitrary"), vmem_limit_bytes=VMEM_LIMIT),
        name="ffn_latent" if is_grid else "ffn_prompt",
    )(x2d, mod, g2, wup3, wc3, bc3, wdn3, gfin)


def _pad_lanes(v):
    return jnp.pad(v.reshape(1, -1), ((0, 0), (0, LANES - v.size)))


def kernel(x_prompt, x_sample, state_ssd_fwd, state_ssd_bwd, c, c_ctx, g_norm1, g_norm2, w_ada, b_ada, w_in, w_conv_short, w_conv_ssd, b_conv_ssd, dt_bias, a_log, d_skip, g_ssd_norm, w_out, w_up, w_ffn_conv, b_ffn_conv, w_down, g_final):
    depth = w_in.shape[0]
    assert depth == 1, "kernel is written for the single-layer problem"
    bp, seq = x_prompt.shape[0], x_prompt.shape[1]
    bd, dec_seq = x_sample.shape[0], x_sample.shape[1]
    assert seq == 256 and dec_seq == SLAB and (bp * seq) % SLAB == 0

    n_c = bd + 1
    rows = -(-n_c // 8) * 8
    cvec = jnp.concatenate([c, c_ctx[None], jnp.zeros((rows - n_c, D_MODEL), F32)], axis=0)
    mod = _mod_call(cvec, w_ada[0], b_ada[0].reshape(1, -1)).reshape(rows, N_MOD, D_MODEL)
    mod_lat = mod[:bd]
    mod_ctx = mod[bd:bd + 1]

    w_in_p = jnp.pad(w_in[0], ((0, 0), (0, D_IN_PAD - w_in.shape[2]))).astype(BF16)
    idx = jnp.arange(CHUNK)
    ltri = (idx[:, None] >= idx[None, :]).astype(BF16)
    utri = (idx[:, None] <= idx[None, :]).astype(BF16)
    expand = (jnp.arange(LANES)[:, None] == (jnp.arange(2 * SSD_WIDTH)[None, :] // SSD_HEAD_DIM)).astype(BF16)
    mixer_consts = (
        g_norm1[0].reshape(1, -1), w_in_p, w_conv_short[0], w_conv_ssd[0], b_conv_ssd[0].reshape(1, -1),
        _pad_lanes(dt_bias[0]), _pad_lanes(a_log[0]),
        jnp.repeat(d_skip[0], SSD_HEAD_DIM).reshape(1, -1), g_ssd_norm[0].reshape(1, -1),
        w_out[0].astype(BF16), ltri, utri, expand,
    )
    wup3 = w_up[0].reshape(D_MODEL, 2, N_FF_BLOCKS, FF_BLOCK).transpose(2, 0, 1, 3) \
        .reshape(N_FF_BLOCKS, D_MODEL, 2 * FF_BLOCK).astype(BF16)
    wc3 = w_ffn_conv[0].reshape(9, 2, N_FF_BLOCKS, FF_BLOCK).transpose(2, 0, 1, 3) \
        .reshape(N_FF_BLOCKS, 9, 2 * FF_BLOCK)
    bc3 = b_ffn_conv[0].reshape(2, N_FF_BLOCKS, FF_BLOCK).transpose(1, 0, 2).reshape(N_FF_BLOCKS, 1, 2 * FF_BLOCK)
    wdn3 = w_down[0].reshape(N_FF_BLOCKS, FF_BLOCK, D_MODEL).astype(BF16)
    ffn_consts = (g_norm2[0].reshape(1, -1), wup3, wc3, bc3, wdn3, g_final.reshape(1, -1))

    state_rows = SSD_HEADS * SSD_HEAD_DIM
    init_f = state_ssd_fwd[:, 0].reshape(bd, state_rows, D_STATE)
    init_b = state_ssd_bwd[:, 0].reshape(bd, state_rows, D_STATE)

    xp2d = x_prompt.reshape(bp * seq, D_MODEL)
    xs2d = x_sample.reshape(bd * dec_seq, D_MODEL)

    xp1, s_f, s_b = _mixer_call(xp2d, mod_ctx, False, seq, None, None, mixer_consts)
    (xs1,) = _mixer_call(xs2d, mod_lat, True, dec_seq, init_f, init_b, mixer_consts)

    y_prompt = _ffn_call(xp1, mod_ctx, False, False, ffn_consts).reshape(x_prompt.shape)
    y_sample = _ffn_call(xs1, mod_lat, True, True, ffn_consts).reshape(x_sample.shape)

    state_shape = (bp, depth, SSD_HEADS, SSD_HEAD_DIM, D_STATE)
    return (y_prompt, y_sample, s_f.reshape(state_shape), s_b.reshape(state_shape))
```

```python
import functools

import jax
import jax.numpy as jnp
from jax import lax
from jax.experimental import pallas as pl
from jax.experimental.pallas import tpu as pltpu

F32 = jnp.float32
BF16 = jnp.bfloat16

D_MODEL = 1024
GRID_W = 64
CONV_WIDTH = 512
SSD_WIDTH = 512
SSD_HEAD_DIM = 64
SSD_HEADS = 8
SSD_GROUPS = 2
D_STATE = 128
CHUNK = 128
D_FF = 2816
N_MOD = 6
EPS = 1e-6

SLAB = 1024
N_CHUNKS = SLAB // CHUNK
ROW_BLOCK = 256
PAD = 8
FF_BLOCK = 256
N_FF_BLOCKS = D_FF // FF_BLOCK
FFN_CHAINS = 2
LANES = 128
DT_COL = 3072
VMEM_LIMIT = 56 * 1024 * 1024


def _silu(v):
    return v / (1.0 + jnp.exp(-v))


def _softplus(v):
    return jnp.maximum(v, 0.0) + jnp.log1p(jnp.exp(-jnp.abs(v)))


def _split3(v):
    hi = v.astype(BF16)
    r1 = v - hi.astype(F32)
    mid = r1.astype(BF16)
    lo = (r1 - mid.astype(F32)).astype(BF16)
    return hi, mid, lo


def _dot(a, b):
    return jnp.dot(a, b, preferred_element_type=F32)


def _dot3(m, v):
    hi, mid, lo = _split3(v)
    return _dot(m, hi) + _dot(m, mid) + _dot(m, lo)


def _expand3(v, expand):
    hi, mid, lo = _split3(v)
    return _dot(hi, expand) + _dot(mid, expand) + _dot(lo, expand)


def _conv3_rows(win, w_ref, start_row, end_row):
    n = win.shape[0]
    row = lax.broadcasted_iota(jnp.int32, (CHUNK, 1), 0)
    prev = pltpu.roll(win, 1, 0)[PAD:PAD + CHUNK]
    cur = win[PAD:PAD + CHUNK]
    nxt = pltpu.roll(win, n - 1, 0)[PAD:PAD + CHUNK]
    prev = jnp.where(row == start_row, 0.0, prev)
    nxt = jnp.where(row == end_row, 0.0, nxt)
    return prev * w_ref[0:1, :] + cur * w_ref[1:2, :] + nxt * w_ref[2:3, :]


def _mod_kernel(c_ref, w_ref, b_ref, o_ref):
    s = _silu(c_ref[...]).astype(BF16)
    o_ref[...] = _dot(s, w_ref[...].astype(BF16)) + b_ref[...]


def _mod_call(cvec, w_ada, b_ada):
    rows = cvec.shape[0]
    return pl.pallas_call(
        _mod_kernel,
        grid=(N_MOD,),
        in_specs=[
            pl.BlockSpec((rows, D_MODEL), lambda j: (0, 0)),
            pl.BlockSpec((D_MODEL, D_MODEL), lambda j: (0, j)),
            pl.BlockSpec((1, D_MODEL), lambda j: (0, j)),
        ],
        out_specs=pl.BlockSpec((rows, D_MODEL), lambda j: (0, j)),
        out_shape=jax.ShapeDtypeStruct((rows, N_MOD * D_MODEL), F32),
        compiler_params=pltpu.CompilerParams(dimension_semantics=("arbitrary",)),
        name="mod_vectors",
    )(cvec, w_ada, b_ada)


def _cast_kernel(src_ref, dst_ref):
    dst_ref[...] = src_ref[...].astype(BF16)


def _cast_bf16(w, cols):
    rows = w.shape[0]
    return pl.pallas_call(
        _cast_kernel,
        grid=(rows // ROW_BLOCK,),
        in_specs=[pl.BlockSpec((ROW_BLOCK, cols), lambda i: (i, 0))],
        out_specs=pl.BlockSpec((ROW_BLOCK, cols), lambda i: (i, 0)),
        out_shape=jax.ShapeDtypeStruct((rows, cols), BF16),
        compiler_params=pltpu.CompilerParams(dimension_semantics=("arbitrary",)),
        name="cast_bf16",
    )(w)


def _up_blocks_kernel(gate_ref, value_ref, dst_ref):
    dst_ref[0, :, 0:FF_BLOCK] = gate_ref[...].astype(BF16)
    dst_ref[0, :, FF_BLOCK:] = value_ref[...].astype(BF16)


def _up_blocks(w_up):
    return pl.pallas_call(
        _up_blocks_kernel,
        grid=(N_FF_BLOCKS,),
        in_specs=[
            pl.BlockSpec((D_MODEL, FF_BLOCK), lambda j: (0, j)),
            pl.BlockSpec((D_MODEL, FF_BLOCK), lambda j: (0, N_FF_BLOCKS + j)),
        ],
        out_specs=pl.BlockSpec((1, D_MODEL, 2 * FF_BLOCK), lambda j: (j, 0, 0)),
        out_shape=jax.ShapeDtypeStruct((N_FF_BLOCKS, D_MODEL, 2 * FF_BLOCK), BF16),
        compiler_params=pltpu.CompilerParams(dimension_semantics=("arbitrary",)),
        name="up_blocks",
    )(w_up, w_up)


def _mixer_kernel(seg_chunks, has_init, *refs):
    refs = list(refs)
    x_ref, mod_ref = refs[:2]
    refs = refs[2:]
    if has_init:
        initf_ref, initb_ref = refs[:2]
        refs = refs[2:]
    (g1_ref, win_ref, wdt_ref, wcs_ref, wcx_ref, bcx_ref, dtb_ref, alog_ref, dskip_ref, gssd_ref, wout_ref,
     ltri_ref, utri_ref, exp_ref) = refs[:14]
    refs = refs[14:]
    out_ref = refs[0]
    refs = refs[1:]
    if not has_init:
        sfo_ref, sbo_ref = refs[:2]
        refs = refs[2:]
    (gchc_ref, xbc_ref, gb_ref, z_ref, dt_ref, e_ref, xs_ref, bc_ref, ymix_ref, sfin_ref, csb_ref,
     sf_ref, sb_ref) = refs

    shift_m = mod_ref[0, 0:1, :]
    scale_m = mod_ref[0, 1:2, :]
    gate_m = mod_ref[0, 2:3, :]

    zpad_a = jnp.zeros((PAD, CONV_WIDTH), F32)
    zpad_x = jnp.zeros((PAD, 2 * SSD_WIDTH), F32)
    gchc_ref[0:PAD, :] = zpad_a
    gchc_ref[SLAB + PAD:SLAB + 2 * PAD, :] = zpad_a
    xbc_ref[0:PAD, :] = zpad_x
    xbc_ref[SLAB + PAD:SLAB + 2 * PAD, :] = zpad_x

    def proj_body(i, carry):
        r0 = pl.multiple_of(i * ROW_BLOCK, ROW_BLOCK)
        rp = pl.multiple_of(i * ROW_BLOCK + PAD, PAD)
        xb = x_ref[pl.ds(r0, ROW_BLOCK), :]
        ms = jnp.mean(xb * xb, axis=-1, keepdims=True)
        h = (xb * lax.rsqrt(ms + EPS)) * g1_ref[...] * (1.0 + scale_m) + shift_m
        hb = h.astype(BF16)

        def proj(lo, hi):
            return _dot(hb, win_ref[:, lo:hi])

        gchc_ref[pl.ds(rp, ROW_BLOCK), :] = proj(0, 512) * proj(1024, 1536)
        gb_ref[pl.ds(r0, ROW_BLOCK), :] = proj(512, 1024)
        z_ref[pl.ds(r0, ROW_BLOCK), :] = proj(1536, 2048)
        xbc_ref[pl.ds(rp, ROW_BLOCK), :] = proj(2048, 3072)
        dt_ref[pl.ds(r0, ROW_BLOCK), :] = _softplus(_dot(hb, wdt_ref[...]) + dtb_ref[...])
        return carry

    lax.fori_loop(0, SLAB // ROW_BLOCK, proj_body, 0)

    a_row = -jnp.exp(alog_ref[...])
    lane_t = lax.broadcasted_iota(jnp.int32, (CHUNK, LANES), 1)
    lane_1 = lax.broadcasted_iota(jnp.int32, (1, LANES), 1)
    is_fwd_t = lane_t < SSD_HEADS
    is_fwd_1 = lane_1 < SSD_HEADS

    def chunk_flags(c):
        pos = lax.rem(c, seg_chunks)
        is_start = pos == 0
        is_end = pos == seg_chunks - 1
        start_row = jnp.where(is_start, 0, -1)
        end_row = jnp.where(is_end, CHUNK - 1, -1)
        return is_start, is_end, start_row, end_row

    def chunk_decay_row(e):
        tot = jnp.where(is_fwd_1, e[CHUNK - 1:CHUNK, :], e[0:1, :])
        dec = jnp.broadcast_to(jnp.exp(tot), (8, LANES))
        return tot, _expand3(dec, exp_ref[...])[0:1, :]

    def fwd_body(c, carry):
        r0 = pl.multiple_of(c * CHUNK, CHUNK)
        rows = pl.ds(r0, CHUNK)
        is_start, is_end, start_row, end_row = chunk_flags(c)

        xc = _conv3_rows(xbc_ref[pl.ds(r0, CHUNK + 2 * PAD), :], wcx_ref, start_row, end_row)
        xc = _silu(xc + bcx_ref[...])
        xs = xc[:, 0:SSD_WIDTH]
        xs_ref[rows, :] = xs
        bc_ref[rows, :] = xc[:, SSD_WIDTH:].astype(BF16)

        ca = _conv3_rows(gchc_ref[pl.ds(r0, CHUNK + 2 * PAD), :], wcs_ref, start_row, end_row)
        ymix_ref[rows, 0:CONV_WIDTH] = (gb_ref[rows, :] * ca).astype(BF16)

        dtt = dt_ref[rows, :]
        da = dtt * a_row
        cs = _dot3(ltri_ref[...], da)
        rcs = _dot3(utri_ref[...], da)
        e = jnp.where(is_fwd_t, cs, rcs)
        e_ref[rows, :] = e
        tot, dec_row = chunk_decay_row(e)
        wst = (dtt * jnp.exp(tot - e)).astype(BF16)
        wexp = _dot(wst, exp_ref[...])
        xst_f = (xs * wexp[:, 0:SSD_WIDTH]).astype(BF16)
        xst_b = (xs * wexp[:, SSD_WIDTH:]).astype(BF16)

        cst_f, cst_b = [], []
        for g in range(SSD_GROUPS):
            bt = jnp.transpose(xc[:, SSD_WIDTH + g * D_STATE:SSD_WIDTH + (g + 1) * D_STATE]).astype(BF16)
            cst_f.append(_dot(bt, xst_f[:, g * 256:(g + 1) * 256]))
            cst_b.append(_dot(bt, xst_b[:, g * 256:(g + 1) * 256]))
        csb_ref[c] = jnp.concatenate(cst_b, axis=1)

        @pl.when(is_start)
        def _():
            if has_init:
                sf_ref[...] = jnp.transpose(initf_ref[0])
            else:
                sf_ref[...] = jnp.zeros_like(sf_ref)

        s_in = sf_ref[...]
        sfin_ref[c] = s_in
        s_new = s_in * dec_row[:, 0:SSD_WIDTH] + jnp.concatenate(cst_f, axis=1)
        sf_ref[...] = s_new

        if not has_init:
            @pl.when(is_end)
            def _():
                sfo_ref[lax.div(c, seg_chunks)] = jnp.transpose(s_new)

        return carry

    lax.fori_loop(0, N_CHUNKS, fwd_body, 0)

    row_t = lax.broadcasted_iota(jnp.int32, (CHUNK, CHUNK), 0)
    col_t = lax.broadcasted_iota(jnp.int32, (CHUNK, CHUNK), 1)
    causal = row_t >= col_t
    anti = col_t >= row_t
    low_half = lane_t < SSD_HEAD_DIM

    def bwd_body(k, carry):
        c = N_CHUNKS - 1 - k
        r0 = pl.multiple_of(c * CHUNK, CHUNK)
        rows = pl.ds(r0, CHUNK)
        is_start, is_end, _, _ = chunk_flags(c)

        @pl.when(is_end)
        def _():
            if has_init:
                sb_ref[...] = jnp.transpose(initb_ref[0])
            else:
                sb_ref[...] = jnp.zeros_like(sb_ref)

        e = e_ref[rows, :]
        et = jnp.transpose(e)
        dtt = dt_ref[rows, :]
        xs = xs_ref[rows, :]
        bc = bc_ref[rows, :]
        s_f = sfin_ref[c].astype(BF16)
        s_b_f32 = sb_ref[...]
        s_b = s_b_f32.astype(BF16)

        dtexp = _dot(dtt.astype(BF16), exp_ref[...])
        xdt_f = xs * dtexp[:, 0:SSD_WIDTH]
        xdt_b = xs * dtexp[:, SSD_WIDTH:]
        ecs = _dot(jnp.exp(e).astype(BF16), exp_ref[...])

        y_parts = []
        for g in range(SSD_GROUPS):
            b_g = bc[:, g * D_STATE:(g + 1) * D_STATE]
            c_g = bc[:, 256 + g * D_STATE:256 + (g + 1) * D_STATE]
            gmat = lax.dot_general(c_g, b_g, (((1,), (1,)), ((), ())), preferred_element_type=F32)
            cs_f = _dot(c_g, s_f[:, g * 256:(g + 1) * 256])
            cs_b = _dot(c_g, s_b[:, g * 256:(g + 1) * 256])
            y_off = (cs_f * ecs[:, g * 256:(g + 1) * 256]
                     + cs_b * ecs[:, SSD_WIDTH + g * 256:SSD_WIDTH + (g + 1) * 256])
            for pair in range(2):
                p0 = g * 256 + pair * LANES
                xf_pair = xdt_f[:, p0:p0 + LANES]
                xb_pair = xdt_b[:, p0:p0 + LANES]
                y_pair = y_off[:, pair * LANES:(pair + 1) * LANES]
                for sub in range(2):
                    head = g * 4 + pair * 2 + sub
                    keep = low_half if sub == 0 else jnp.logical_not(low_half)
                    rhs_f = jnp.where(keep, xf_pair, 0.0).astype(BF16)
                    rhs_b = jnp.where(keep, xb_pair, 0.0).astype(BF16)
                    jf, jb = head, SSD_HEADS + head
                    d_f = e[:, jf:jf + 1] - et[jf:jf + 1, :]
                    d_b = e[:, jb:jb + 1] - et[jb:jb + 1, :]
                    sc_f = (gmat * jnp.where(causal, jnp.exp(d_f), 0.0)).astype(BF16)
                    sc_b = (gmat * jnp.where(anti, jnp.exp(d_b), 0.0)).astype(BF16)
                    y_pair = y_pair + _dot(sc_f, rhs_f) + _dot(sc_b, rhs_b)
                y_parts.append(y_pair)
        y = jnp.concatenate(y_parts, axis=1) + xs * dskip_ref[...]
        yz = y * _silu(z_ref[rows, :])
        ms = jnp.mean(yz * yz, axis=-1, keepdims=True)
        yn = (yz * lax.rsqrt(ms + EPS)) * gssd_ref[...]
        ymix_ref[rows, CONV_WIDTH:] = yn.astype(BF16)

        _, dec_row = chunk_decay_row(e)
        s_new = s_b_f32 * dec_row[:, SSD_WIDTH:] + csb_ref[c]
        sb_ref[...] = s_new

        if not has_init:
            @pl.when(is_start)
            def _():
                sbo_ref[lax.div(c, seg_chunks)] = jnp.transpose(s_new)

        return carry

    lax.fori_loop(0, N_CHUNKS, bwd_body, 0)

    def out_body(i, carry):
        r0 = pl.multiple_of(i * ROW_BLOCK, ROW_BLOCK)
        rows = pl.ds(r0, ROW_BLOCK)
        o = _dot(ymix_ref[rows, :], wout_ref[...])
        out_ref[rows, :] = x_ref[rows, :] + gate_m * o
        return carry

    lax.fori_loop(0, SLAB // ROW_BLOCK, out_body, 0)


def _const_spec(shape):
    nd = len(shape)
    return pl.BlockSpec(shape, lambda i: (0,) * nd)


def _mixer_call(x2d, mod, mod_per_slab, seg_len, init_f, init_b, consts):
    n_slabs = x2d.shape[0] // SLAB
    seg_chunks = seg_len // CHUNK
    segs_per_slab = SLAB // seg_len
    has_init = init_f is not None
    state_rows = SSD_HEADS * SSD_HEAD_DIM

    mod_map = (lambda i: (i, 0, 0)) if mod_per_slab else (lambda i: (0, 0, 0))
    in_specs = [
        pl.BlockSpec((SLAB, D_MODEL), lambda i: (i, 0)),
        pl.BlockSpec((1, N_MOD, D_MODEL), mod_map),
    ]
    args = [x2d, mod]
    if has_init:
        in_specs += [pl.BlockSpec((1, state_rows, D_STATE), lambda i: (i, 0, 0))] * 2
        args += [init_f, init_b]
    in_specs += [_const_spec(a.shape) for a in consts]
    args += list(consts)

    out_specs = [pl.BlockSpec((SLAB, D_MODEL), lambda i: (i, 0))]
    out_shape = [jax.ShapeDtypeStruct(x2d.shape, F32)]
    if not has_init:
        n_seq = n_slabs * segs_per_slab
        out_specs += [pl.BlockSpec((segs_per_slab, state_rows, D_STATE), lambda i: (i, 0, 0))] * 2
        out_shape += [jax.ShapeDtypeStruct((n_seq, state_rows, D_STATE), F32)] * 2

    scratch = [
        pltpu.VMEM((SLAB + 2 * PAD, CONV_WIDTH), F32),
        pltpu.VMEM((SLAB + 2 * PAD, 2 * SSD_WIDTH), F32),
        pltpu.VMEM((SLAB, CONV_WIDTH), F32),
        pltpu.VMEM((SLAB, SSD_WIDTH), F32),
        pltpu.VMEM((SLAB, LANES), F32),
        pltpu.VMEM((SLAB, LANES), F32),
        pltpu.VMEM((SLAB, SSD_WIDTH), F32),
        pltpu.VMEM((SLAB, 2 * SSD_GROUPS * D_STATE), BF16),
        pltpu.VMEM((SLAB, D_MODEL), BF16),
        pltpu.VMEM((N_CHUNKS, D_STATE, SSD_WIDTH), F32),
        pltpu.VMEM((N_CHUNKS, D_STATE, SSD_WIDTH), F32),
        pltpu.VMEM((D_STATE, SSD_WIDTH), F32),
        pltpu.VMEM((D_STATE, SSD_WIDTH), F32),
    ]
    return pl.pallas_call(
        functools.partial(_mixer_kernel, seg_chunks, has_init),
        grid=(n_slabs,),
        in_specs=in_specs,
        out_specs=out_specs,
        out_shape=out_shape,
        scratch_shapes=scratch,
        compiler_params=pltpu.CompilerParams(
            dimension_semantics=("arbitrary",), vmem_limit_bytes=VMEM_LIMIT),
        name="mixer_latent" if has_init else "mixer_prompt",
    )(*args)


def _ffn_conv(u, wc_ref, bc_ref, is_grid):
    n = u.shape[0]
    row = lax.broadcasted_iota(jnp.int32, (n, 1), 0)
    period = GRID_W if is_grid else 256
    pos = jnp.bitwise_and(row, period - 1)
    um = jnp.where(pos == 0, 0.0, pltpu.roll(u, 1, 0))
    up = jnp.where(pos == period - 1, 0.0, pltpu.roll(u, n - 1, 0))

    def taps(di):
        return um * wc_ref[0, 3 * di:3 * di + 1, :] + u * wc_ref[0, 3 * di + 1:3 * di + 2, :] \
            + up * wc_ref[0, 3 * di + 2:3 * di + 3, :]

    out = taps(1) + bc_ref[0]
    if is_grid:
        zeros = jnp.zeros((GRID_W, u.shape[1]), F32)
        out = out + jnp.concatenate([zeros, taps(0)[:n - GRID_W]], axis=0)
        out = out + jnp.concatenate([taps(2)[GRID_W:], zeros], axis=0)
    return out


def _ffn_kernel(is_grid, x_ref, mod_ref, g2_ref, wup_ref, wc_ref, bc_ref, wdn_ref, gfin_ref, out_ref,
                h2_ref, acc_ref):
    j = pl.program_id(1)

    @pl.when(j == 0)
    def _():
        shift_f = mod_ref[0, 3:4, :]
        scale_f = mod_ref[0, 4:5, :]
        x = x_ref[...]
        ms = jnp.mean(x * x, axis=-1, keepdims=True)
        h2 = (x * lax.rsqrt(ms + EPS)) * g2_ref[...] * (1.0 + scale_f) + shift_f
        h2_ref[...] = h2.astype(BF16)
        acc_ref[...] = jnp.zeros_like(acc_ref)

    n_chains = 1 if is_grid else FFN_CHAINS
    chain = SLAB // n_chains
    for k in range(n_chains):
        rows = slice(k * chain, (k + 1) * chain)
        u = _dot(h2_ref[rows, :], wup_ref[0])
        u = _ffn_conv(u, wc_ref, bc_ref, is_grid)
        a = (_silu(u[:, 0:FF_BLOCK]) * u[:, FF_BLOCK:]).astype(BF16)
        acc_ref[rows, :] += _dot(a, wdn_ref[0])

    @pl.when(j == N_FF_BLOCKS - 1)
    def _():
        gate_f = mod_ref[0, 5:6, :]
        x2 = x_ref[...] + gate_f * acc_ref[...]
        ms = jnp.mean(x2 * x2, axis=-1, keepdims=True)
        out_ref[...] = (x2 * lax.rsqrt(ms + EPS)) * gfin_ref[...]


def _ffn_call(x2d, mod, mod_per_slab, is_grid, consts):
    n_slabs = x2d.shape[0] // SLAB
    g2, wup3, wc3, bc3, wdn3, gfin = consts
    mod_map = (lambda i, j: (i, 0, 0)) if mod_per_slab else (lambda i, j: (0, 0, 0))
    return pl.pallas_call(
        functools.partial(_ffn_kernel, is_grid),
        grid=(n_slabs, N_FF_BLOCKS),
        in_specs=[
            pl.BlockSpec((SLAB, D_MODEL), lambda i, j: (i, 0)),
            pl.BlockSpec((1, N_MOD, D_MODEL), mod_map),
            pl.BlockSpec((1, D_MODEL), lambda i, j: (0, 0)),
            pl.BlockSpec((1, D_MODEL, 2 * FF_BLOCK), lambda i, j: (j, 0, 0)),
            pl.BlockSpec((1, 9, 2 * FF_BLOCK), lambda i, j: (j, 0, 0)),
            pl.BlockSpec((1, 1, 2 * FF_BLOCK), lambda i, j: (j, 0, 0)),
            pl.BlockSpec((1, FF_BLOCK, D_MODEL), lambda i, j: (j, 0, 0)),
            pl.BlockSpec((1, D_MODEL), lambda i, j: (0, 0)),
        ],
        out_specs=pl.BlockSpec((SLAB, D_MODEL), lambda i, j: (i, 0)),
        out_shape=jax.ShapeDtypeStruct(x2d.shape, F32),
        scratch_shapes=[
            pltpu.VMEM((SLAB, D_MODEL), BF16),
            pltpu.VMEM((SLAB, D_MODEL), F32),
        ],
        compiler_params=pltpu.CompilerParams(
            dimension_semantics=("arbitrary", "arbitrary"), vmem_limit_bytes=VMEM_LIMIT),
        name="ffn_latent" if is_grid else "ffn_prompt",
    )(x2d, mod, g2, wup3, wc3, bc3, wdn3, gfin)


def _pad_lanes(v):
    return jnp.pad(v.reshape(1, -1), ((0, 0), (0, LANES - v.size)))


def kernel(x_prompt, x_sample, state_ssd_fwd, state_ssd_bwd, c, c_ctx, g_norm1, g_norm2, w_ada, b_ada, w_in, w_conv_short, w_conv_ssd, b_conv_ssd, dt_bias, a_log, d_skip, g_ssd_norm, w_out, w_up, w_ffn_conv, b_ffn_conv, w_down, g_final):
    depth = w_in.shape[0]
    assert depth == 1, "kernel is written for the single-layer problem"
    bp, seq = x_prompt.shape[0], x_prompt.shape[1]
    bd, dec_seq = x_sample.shape[0], x_sample.shape[1]
    assert seq == 256 and dec_seq == SLAB and (bp * seq) % SLAB == 0

    n_c = bd + 1
    rows = -(-n_c // 8) * 8
    cvec = jnp.concatenate([c, c_ctx[None], jnp.zeros((rows - n_c, D_MODEL), F32)], axis=0)
    mod = _mod_call(cvec, w_ada[0], b_ada[0].reshape(1, -1)).reshape(rows, N_MOD, D_MODEL)
    mod_lat = mod[:bd]
    mod_ctx = mod[bd:bd + 1]

    w_in_main = _cast_bf16(w_in[0], DT_COL)
    w_dt = jnp.pad(w_in[0][:, DT_COL:], ((0, 0), (0, LANES - 2 * SSD_HEADS))).astype(BF16)
    idx = jnp.arange(CHUNK)
    ltri = (idx[:, None] >= idx[None, :]).astype(BF16)
    utri = (idx[:, None] <= idx[None, :]).astype(BF16)
    expand = (jnp.arange(LANES)[:, None] == (jnp.arange(2 * SSD_WIDTH)[None, :] // SSD_HEAD_DIM)).astype(BF16)
    mixer_consts = (
        g_norm1[0].reshape(1, -1), w_in_main, w_dt, w_conv_short[0], w_conv_ssd[0], b_conv_ssd[0].reshape(1, -1),
        _pad_lanes(dt_bias[0]), _pad_lanes(a_log[0]),
        jnp.repeat(d_skip[0], SSD_HEAD_DIM).reshape(1, -1), g_ssd_norm[0].reshape(1, -1),
        _cast_bf16(w_out[0], D_MODEL), ltri, utri, expand,
    )
    wup3 = _up_blocks(w_up[0])
    wc3 = w_ffn_conv[0].reshape(9, 2, N_FF_BLOCKS, FF_BLOCK).transpose(2, 0, 1, 3) \
        .reshape(N_FF_BLOCKS, 9, 2 * FF_BLOCK)
    bc3 = b_ffn_conv[0].reshape(2, N_FF_BLOCKS, FF_BLOCK).transpose(1, 0, 2).reshape(N_FF_BLOCKS, 1, 2 * FF_BLOCK)
    wdn3 = _cast_bf16(w_down[0], D_MODEL).reshape(N_FF_BLOCKS, FF_BLOCK, D_MODEL)
    ffn_consts = (g_norm2[0].reshape(1, -1), wup3, wc3, bc3, wdn3, g_final.reshape(1, -1))

    state_rows = SSD_HEADS * SSD_HEAD_DIM
    init_f = state_ssd_fwd[:, 0].reshape(bd, state_rows, D_STATE)
    init_b = state_ssd_bwd[:, 0].reshape(bd, state_rows, D_STATE)

    xp2d = x_prompt.reshape(bp * seq, D_MODEL)
    xs2d = x_sample.reshape(bd * dec_seq, D_MODEL)

    xp1, s_f, s_b = _mixer_call(xp2d, mod_ctx, False, seq, None, None, mixer_consts)
    (xs1,) = _mixer_call(xs2d, mod_lat, True, dec_seq, init_f, init_b, mixer_consts)

    y_prompt = _ffn_call(xp1, mod_ctx, False, False, ffn_consts).reshape(x_prompt.shape)
    y_sample = _ffn_call(xs1, mod_lat, True, True, ffn_consts).reshape(x_sample.shape)

    state_shape = (bp, depth, SSD_HEADS, SSD_HEAD_DIM, D_STATE)
    return (y_prompt, y_sample, s_f.reshape(state_shape), s_b.reshape(state_shape))
```

```python
import functools

import jax
import jax.numpy as jnp
from jax import lax
from jax.experimental import pallas as pl
from jax.experimental.pallas import tpu as pltpu

F32 = jnp.float32
BF16 = jnp.bfloat16

D_MODEL = 1024
GRID_W = 64
CONV_WIDTH = 512
SSD_WIDTH = 512
SSD_HEAD_DIM = 64
SSD_HEADS = 8
SSD_GROUPS = 2
D_STATE = 128
CHUNK = 128
D_FF = 2816
N_MOD = 6
EPS = 1e-6

SLAB = 1024
N_CHUNKS = SLAB // CHUNK
ROW_BLOCK = 256
PAD = 8
FF_BLOCK = 256
N_FF_BLOCKS = D_FF // FF_BLOCK
FF_BLOCKS_PER_STEP = 4
N_FF_STEPS = -(-N_FF_BLOCKS // FF_BLOCKS_PER_STEP)
N_FF_PADDED = N_FF_STEPS * FF_BLOCKS_PER_STEP
LANES = 128
DT_COL = 3072
VMEM_LIMIT = 56 * 1024 * 1024


def _silu(v):
    return v / (1.0 + jnp.exp(-v))


def _softplus(v):
    return jnp.maximum(v, 0.0) + jnp.log1p(jnp.exp(-jnp.abs(v)))


def _split3(v):
    hi = v.astype(BF16)
    r1 = v - hi.astype(F32)
    mid = r1.astype(BF16)
    lo = (r1 - mid.astype(F32)).astype(BF16)
    return hi, mid, lo


def _dot(a, b):
    return jnp.dot(a, b, preferred_element_type=F32)


def _dot3(m, v):
    hi, mid, lo = _split3(v)
    return _dot(m, hi) + _dot(m, mid) + _dot(m, lo)


def _expand3(v, expand):
    hi, mid, lo = _split3(v)
    return _dot(hi, expand) + _dot(mid, expand) + _dot(lo, expand)


def _conv3_rows(win, w_ref, start_row, end_row):
    n = win.shape[0]
    row = lax.broadcasted_iota(jnp.int32, (CHUNK, 1), 0)
    prev = pltpu.roll(win, 1, 0)[PAD:PAD + CHUNK]
    cur = win[PAD:PAD + CHUNK]
    nxt = pltpu.roll(win, n - 1, 0)[PAD:PAD + CHUNK]
    prev = jnp.where(row == start_row, 0.0, prev)
    nxt = jnp.where(row == end_row, 0.0, nxt)
    return prev * w_ref[0:1, :] + cur * w_ref[1:2, :] + nxt * w_ref[2:3, :]


def _mod_kernel(c_ref, w_ref, b_ref, o_ref):
    s = _silu(c_ref[...]).astype(BF16)
    o_ref[...] = _dot(s, w_ref[...].astype(BF16)) + b_ref[...]


def _mod_call(cvec, w_ada, b_ada):
    rows = cvec.shape[0]
    return pl.pallas_call(
        _mod_kernel,
        grid=(N_MOD,),
        in_specs=[
            pl.BlockSpec((rows, D_MODEL), lambda j: (0, 0)),
            pl.BlockSpec((D_MODEL, D_MODEL), lambda j: (0, j)),
            pl.BlockSpec((1, D_MODEL), lambda j: (0, j)),
        ],
        out_specs=pl.BlockSpec((rows, D_MODEL), lambda j: (0, j)),
        out_shape=jax.ShapeDtypeStruct((rows, N_MOD * D_MODEL), F32),
        compiler_params=pltpu.CompilerParams(dimension_semantics=("arbitrary",)),
        name="mod_vectors",
    )(cvec, w_ada, b_ada)


def _cast_kernel(n_src_blocks, src_ref, dst_ref):
    i = pl.program_id(0)

    @pl.when(i < n_src_blocks)
    def _():
        dst_ref[...] = src_ref[...].astype(BF16)

    @pl.when(i >= n_src_blocks)
    def _():
        dst_ref[...] = jnp.zeros_like(dst_ref)


def _cast_bf16(w, cols, pad_row_blocks=0):
    n_src = w.shape[0] // ROW_BLOCK
    n_dst = n_src + pad_row_blocks
    return pl.pallas_call(
        functools.partial(_cast_kernel, n_src),
        grid=(n_dst,),
        in_specs=[pl.BlockSpec((ROW_BLOCK, cols), lambda i: (jnp.minimum(i, n_src - 1), 0))],
        out_specs=pl.BlockSpec((ROW_BLOCK, cols), lambda i: (i, 0)),
        out_shape=jax.ShapeDtypeStruct((n_dst * ROW_BLOCK, cols), BF16),
        compiler_params=pltpu.CompilerParams(dimension_semantics=("arbitrary",)),
        name="cast_bf16",
    )(w)


def _up_blocks_kernel(src_ref, dst_ref):
    j = pl.program_id(0)

    @pl.when(j < N_FF_BLOCKS)
    def _():
        dst_ref[0] = src_ref[...].astype(BF16)

    @pl.when(j >= N_FF_BLOCKS)
    def _():
        dst_ref[...] = jnp.zeros_like(dst_ref)


def _up_blocks(w_up):
    last = N_FF_BLOCKS - 1
    return pl.pallas_call(
        _up_blocks_kernel,
        grid=(N_FF_PADDED, 2),
        in_specs=[pl.BlockSpec((D_MODEL, FF_BLOCK), lambda j, h: (0, h * N_FF_BLOCKS + jnp.minimum(j, last)))],
        out_specs=pl.BlockSpec((1, D_MODEL, FF_BLOCK), lambda j, h: (j, 0, h)),
        out_shape=jax.ShapeDtypeStruct((N_FF_PADDED, D_MODEL, 2 * FF_BLOCK), BF16),
        compiler_params=pltpu.CompilerParams(dimension_semantics=("arbitrary", "arbitrary")),
        name="up_blocks",
    )(w_up)


def _mixer_kernel(seg_chunks, has_init, *refs):
    refs = list(refs)
    x_ref, mod_ref = refs[:2]
    refs = refs[2:]
    if has_init:
        initf_ref, initb_ref = refs[:2]
        refs = refs[2:]
    (g1_ref, win_ref, wdt_ref, wcs_ref, wcx_ref, bcx_ref, dtb_ref, alog_ref, dskip_ref, gssd_ref, wout_ref,
     ltri_ref, utri_ref, exp_ref) = refs[:14]
    refs = refs[14:]
    out_ref = refs[0]
    refs = refs[1:]
    if not has_init:
        sfo_ref, sbo_ref = refs[:2]
        refs = refs[2:]
    (gchc_ref, xbc_ref, gb_ref, z_ref, dt_ref, e_ref, xs_ref, bc_ref, ymix_ref, sfin_ref, csb_ref,
     sf_ref, sb_ref) = refs

    shift_m = mod_ref[0, 0:1, :]
    scale_m = mod_ref[0, 1:2, :]
    gate_m = mod_ref[0, 2:3, :]

    zpad_a = jnp.zeros((PAD, CONV_WIDTH), F32)
    zpad_x = jnp.zeros((PAD, 2 * SSD_WIDTH), F32)
    gchc_ref[0:PAD, :] = zpad_a
    gchc_ref[SLAB + PAD:SLAB + 2 * PAD, :] = zpad_a
    xbc_ref[0:PAD, :] = zpad_x
    xbc_ref[SLAB + PAD:SLAB + 2 * PAD, :] = zpad_x

    def proj_body(i, carry):
        r0 = pl.multiple_of(i * ROW_BLOCK, ROW_BLOCK)
        rp = pl.multiple_of(i * ROW_BLOCK + PAD, PAD)
        xb = x_ref[pl.ds(r0, ROW_BLOCK), :]
        ms = jnp.mean(xb * xb, axis=-1, keepdims=True)
        h = (xb * lax.rsqrt(ms + EPS)) * g1_ref[...] * (1.0 + scale_m) + shift_m
        hb = h.astype(BF16)

        def proj(lo, hi):
            return _dot(hb, win_ref[:, lo:hi])

        gchc_ref[pl.ds(rp, ROW_BLOCK), :] = proj(0, 512) * proj(1024, 1536)
        gb_ref[pl.ds(r0, ROW_BLOCK), :] = proj(512, 1024)
        z_ref[pl.ds(r0, ROW_BLOCK), :] = proj(1536, 2048)
        xbc_ref[pl.ds(rp, ROW_BLOCK), :] = proj(2048, 3072)
        dt_ref[pl.ds(r0, ROW_BLOCK), :] = _softplus(_dot(hb, wdt_ref[...]) + dtb_ref[...])
        return carry

    lax.fori_loop(0, SLAB // ROW_BLOCK, proj_body, 0)

    a_row = -jnp.exp(alog_ref[...])
    lane_t = lax.broadcasted_iota(jnp.int32, (CHUNK, LANES), 1)
    lane_1 = lax.broadcasted_iota(jnp.int32, (1, LANES), 1)
    is_fwd_t = lane_t < SSD_HEADS
    is_fwd_1 = lane_1 < SSD_HEADS

    def chunk_flags(c):
        pos = lax.rem(c, seg_chunks)
        is_start = pos == 0
        is_end = pos == seg_chunks - 1
        start_row = jnp.where(is_start, 0, -1)
        end_row = jnp.where(is_end, CHUNK - 1, -1)
        return is_start, is_end, start_row, end_row

    def chunk_decay_row(e):
        tot = jnp.where(is_fwd_1, e[CHUNK - 1:CHUNK, :], e[0:1, :])
        dec = jnp.broadcast_to(jnp.exp(tot), (8, LANES))
        return tot, _expand3(dec, exp_ref[...])[0:1, :]

    def fwd_body(c, carry):
        r0 = pl.multiple_of(c * CHUNK, CHUNK)
        rows = pl.ds(r0, CHUNK)
        is_start, is_end, start_row, end_row = chunk_flags(c)

        xc = _conv3_rows(xbc_ref[pl.ds(r0, CHUNK + 2 * PAD), :], wcx_ref, start_row, end_row)
        xc = _silu(xc + bcx_ref[...])
        xs = xc[:, 0:SSD_WIDTH]
        xs_ref[rows, :] = xs
        bc_ref[rows, :] = xc[:, SSD_WIDTH:].astype(BF16)

        ca = _conv3_rows(gchc_ref[pl.ds(r0, CHUNK + 2 * PAD), :], wcs_ref, start_row, end_row)
        ymix_ref[rows, 0:CONV_WIDTH] = (gb_ref[rows, :] * ca).astype(BF16)

        dtt = dt_ref[rows, :]
        da = dtt * a_row
        cs = _dot3(ltri_ref[...], da)
        rcs = _dot3(utri_ref[...], da)
        e = jnp.where(is_fwd_t, cs, rcs)
        e_ref[rows, :] = e
        tot, dec_row = chunk_decay_row(e)
        wst = (dtt * jnp.exp(tot - e)).astype(BF16)
        wexp = _dot(wst, exp_ref[...])
        xst_f = (xs * wexp[:, 0:SSD_WIDTH]).astype(BF16)
        xst_b = (xs * wexp[:, SSD_WIDTH:]).astype(BF16)

        cst_f, cst_b = [], []
        for g in range(SSD_GROUPS):
            bt = jnp.transpose(xc[:, SSD_WIDTH + g * D_STATE:SSD_WIDTH + (g + 1) * D_STATE]).astype(BF16)
            cst_f.append(_dot(bt, xst_f[:, g * 256:(g + 1) * 256]))
            cst_b.append(_dot(bt, xst_b[:, g * 256:(g + 1) * 256]))
        csb_ref[c] = jnp.concatenate(cst_b, axis=1)

        @pl.when(is_start)
        def _():
            if has_init:
                sf_ref[...] = jnp.transpose(initf_ref[0])
            else:
                sf_ref[...] = jnp.zeros_like(sf_ref)

        s_in = sf_ref[...]
        sfin_ref[c] = s_in
        s_new = s_in * dec_row[:, 0:SSD_WIDTH] + jnp.concatenate(cst_f, axis=1)
        sf_ref[...] = s_new

        if not has_init:
            @pl.when(is_end)
            def _():
                sfo_ref[lax.div(c, seg_chunks)] = jnp.transpose(s_new)

        return carry

    lax.fori_loop(0, N_CHUNKS, fwd_body, 0)

    row_t = lax.broadcasted_iota(jnp.int32, (CHUNK, CHUNK), 0)
    col_t = lax.broadcasted_iota(jnp.int32, (CHUNK, CHUNK), 1)
    causal = row_t >= col_t
    anti = col_t >= row_t
    low_half = lane_t < SSD_HEAD_DIM

    def bwd_body(k, carry):
        c = N_CHUNKS - 1 - k
        r0 = pl.multiple_of(c * CHUNK, CHUNK)
        rows = pl.ds(r0, CHUNK)
        is_start, is_end, _, _ = chunk_flags(c)

        @pl.when(is_end)
        def _():
            if has_init:
                sb_ref[...] = jnp.transpose(initb_ref[0])
            else:
                sb_ref[...] = jnp.zeros_like(sb_ref)

        e = e_ref[rows, :]
        et = jnp.transpose(e)
        dtt = dt_ref[rows, :]
        xs = xs_ref[rows, :]
        bc = bc_ref[rows, :]
        s_f = sfin_ref[c].astype(BF16)
        s_b_f32 = sb_ref[...]
        s_b = s_b_f32.astype(BF16)

        dtexp = _dot(dtt.astype(BF16), exp_ref[...])
        xdt_f = xs * dtexp[:, 0:SSD_WIDTH]
        xdt_b = xs * dtexp[:, SSD_WIDTH:]
        ecs = _dot(jnp.exp(e).astype(BF16), exp_ref[...])

        y_parts = []
        for g in range(SSD_GROUPS):
            b_g = bc[:, g * D_STATE:(g + 1) * D_STATE]
            c_g = bc[:, 256 + g * D_STATE:256 + (g + 1) * D_STATE]
            gmat = lax.dot_general(c_g, b_g, (((1,), (1,)), ((), ())), preferred_element_type=F32)
            cs_f = _dot(c_g, s_f[:, g * 256:(g + 1) * 256])
            cs_b = _dot(c_g, s_b[:, g * 256:(g + 1) * 256])
            y_off = (cs_f * ecs[:, g * 256:(g + 1) * 256]
                     + cs_b * ecs[:, SSD_WIDTH + g * 256:SSD_WIDTH + (g + 1) * 256])
            for pair in range(2):
                p0 = g * 256 + pair * LANES
                xf_pair = xdt_f[:, p0:p0 + LANES]
                xb_pair = xdt_b[:, p0:p0 + LANES]
                y_pair = y_off[:, pair * LANES:(pair + 1) * LANES]
                for sub in range(2):
                    head = g * 4 + pair * 2 + sub
                    keep = low_half if sub == 0 else jnp.logical_not(low_half)
                    rhs_f = jnp.where(keep, xf_pair, 0.0).astype(BF16)
                    rhs_b = jnp.where(keep, xb_pair, 0.0).astype(BF16)
                    jf, jb = head, SSD_HEADS + head
                    d_f = e[:, jf:jf + 1] - et[jf:jf + 1, :]
                    d_b = e[:, jb:jb + 1] - et[jb:jb + 1, :]
                    sc_f = (gmat * jnp.where(causal, jnp.exp(d_f), 0.0)).astype(BF16)
                    sc_b = (gmat * jnp.where(anti, jnp.exp(d_b), 0.0)).astype(BF16)
                    y_pair = y_pair + _dot(sc_f, rhs_f) + _dot(sc_b, rhs_b)
                y_parts.append(y_pair)
        y = jnp.concatenate(y_parts, axis=1) + xs * dskip_ref[...]
        yz = y * _silu(z_ref[rows, :])
        ms = jnp.mean(yz * yz, axis=-1, keepdims=True)
        yn = (yz * lax.rsqrt(ms + EPS)) * gssd_ref[...]
        ymix_ref[rows, CONV_WIDTH:] = yn.astype(BF16)

        _, dec_row = chunk_decay_row(e)
        s_new = s_b_f32 * dec_row[:, SSD_WIDTH:] + csb_ref[c]
        sb_ref[...] = s_new

        if not has_init:
            @pl.when(is_start)
            def _():
                sbo_ref[lax.div(c, seg_chunks)] = jnp.transpose(s_new)

        return carry

    lax.fori_loop(0, N_CHUNKS, bwd_body, 0)

    def out_body(i, carry):
        r0 = pl.multiple_of(i * ROW_BLOCK, ROW_BLOCK)
        rows = pl.ds(r0, ROW_BLOCK)
        o = _dot(ymix_ref[rows, :], wout_ref[...])
        out_ref[rows, :] = x_ref[rows, :] + gate_m * o
        return carry

    lax.fori_loop(0, SLAB // ROW_BLOCK, out_body, 0)


def _const_spec(shape):
    nd = len(shape)
    return pl.BlockSpec(shape, lambda i: (0,) * nd)


def _mixer_call(x2d, mod, mod_per_slab, seg_len, init_f, init_b, consts):
    n_slabs = x2d.shape[0] // SLAB
    seg_chunks = seg_len // CHUNK
    segs_per_slab = SLAB // seg_len
    has_init = init_f is not None
    state_rows = SSD_HEADS * SSD_HEAD_DIM

    mod_map = (lambda i: (i, 0, 0)) if mod_per_slab else (lambda i: (0, 0, 0))
    in_specs = [
        pl.BlockSpec((SLAB, D_MODEL), lambda i: (i, 0)),
        pl.BlockSpec((1, N_MOD, D_MODEL), mod_map),
    ]
    args = [x2d, mod]
    if has_init:
        in_specs += [pl.BlockSpec((1, state_rows, D_STATE), lambda i: (i, 0, 0))] * 2
        args += [init_f, init_b]
    in_specs += [_const_spec(a.shape) for a in consts]
    args += list(consts)

    out_specs = [pl.BlockSpec((SLAB, D_MODEL), lambda i: (i, 0))]
    out_shape = [jax.ShapeDtypeStruct(x2d.shape, F32)]
    if not has_init:
        n_seq = n_slabs * segs_per_slab
        out_specs += [pl.BlockSpec((segs_per_slab, state_rows, D_STATE), lambda i: (i, 0, 0))] * 2
        out_shape += [jax.ShapeDtypeStruct((n_seq, state_rows, D_STATE), F32)] * 2

    scratch = [
        pltpu.VMEM((SLAB + 2 * PAD, CONV_WIDTH), F32),
        pltpu.VMEM((SLAB + 2 * PAD, 2 * SSD_WIDTH), F32),
        pltpu.VMEM((SLAB, CONV_WIDTH), F32),
        pltpu.VMEM((SLAB, SSD_WIDTH), F32),
        pltpu.VMEM((SLAB, LANES), F32),
        pltpu.VMEM((SLAB, LANES), F32),
        pltpu.VMEM((SLAB, SSD_WIDTH), F32),
        pltpu.VMEM((SLAB, 2 * SSD_GROUPS * D_STATE), BF16),
        pltpu.VMEM((SLAB, D_MODEL), BF16),
        pltpu.VMEM((N_CHUNKS, D_STATE, SSD_WIDTH), F32),
        pltpu.VMEM((N_CHUNKS, D_STATE, SSD_WIDTH), F32),
        pltpu.VMEM((D_STATE, SSD_WIDTH), F32),
        pltpu.VMEM((D_STATE, SSD_WIDTH), F32),
    ]
    return pl.pallas_call(
        functools.partial(_mixer_kernel, seg_chunks, has_init),
        grid=(n_slabs,),
        in_specs=in_specs,
        out_specs=out_specs,
        out_shape=out_shape,
        scratch_shapes=scratch,
        compiler_params=pltpu.CompilerParams(
            dimension_semantics=("arbitrary",), vmem_limit_bytes=VMEM_LIMIT),
        name="mixer_latent" if has_init else "mixer_prompt",
    )(*args)


def _ffn_conv(u, wc_ref, bc_ref, is_grid):
    n = u.shape[0]
    row = lax.broadcasted_iota(jnp.int32, (n, 1), 0)
    period = GRID_W if is_grid else 256
    pos = jnp.bitwise_and(row, period - 1)
    um = jnp.where(pos == 0, 0.0, pltpu.roll(u, 1, 0))
    up = jnp.where(pos == period - 1, 0.0, pltpu.roll(u, n - 1, 0))

    def taps(di):
        return um * wc_ref[3 * di:3 * di + 1, :] + u * wc_ref[3 * di + 1:3 * di + 2, :] \
            + up * wc_ref[3 * di + 2:3 * di + 3, :]

    out = taps(1) + bc_ref[...]
    if is_grid:
        zeros = jnp.zeros((GRID_W, u.shape[1]), F32)
        out = out + jnp.concatenate([zeros, taps(0)[:n - GRID_W]], axis=0)
        out = out + jnp.concatenate([taps(2)[GRID_W:], zeros], axis=0)
    return out


def _ffn_kernel(is_grid, x_ref, mod_ref, g2_ref, wup_ref, wc_ref, bc_ref, wdn_ref, gfin_ref, out_ref,
                h2_ref, acc_ref):
    j = pl.program_id(1)

    @pl.when(j == 0)
    def _():
        shift_f = mod_ref[0, 3:4, :]
        scale_f = mod_ref[0, 4:5, :]
        x = x_ref[...]
        ms = jnp.mean(x * x, axis=-1, keepdims=True)
        h2 = (x * lax.rsqrt(ms + EPS)) * g2_ref[...] * (1.0 + scale_f) + shift_f
        h2_ref[...] = h2.astype(BF16)
        acc_ref[...] = jnp.zeros_like(acc_ref)

    def up_proj(b):
        return _dot(h2_ref[...], wup_ref[b])

    u_next = up_proj(0)
    for b in range(FF_BLOCKS_PER_STEP):
        u = u_next
        if b + 1 < FF_BLOCKS_PER_STEP:
            u_next = up_proj(b + 1)
        u = _ffn_conv(u, wc_ref.at[b], bc_ref.at[b], is_grid)
        a = (_silu(u[:, 0:FF_BLOCK]) * u[:, FF_BLOCK:]).astype(BF16)
        acc_ref[...] += _dot(a, wdn_ref[b])

    @pl.when(j == N_FF_STEPS - 1)
    def _():
        gate_f = mod_ref[0, 5:6, :]
        x2 = x_ref[...] + gate_f * acc_ref[...]
        ms = jnp.mean(x2 * x2, axis=-1, keepdims=True)
        out_ref[...] = (x2 * lax.rsqrt(ms + EPS)) * gfin_ref[...]


def _ffn_call(x2d, mod, mod_per_slab, is_grid, consts):
    n_slabs = x2d.shape[0] // SLAB
    g2, wup3, wc3, bc3, wdn3, gfin = consts
    mod_map = (lambda i, j: (i, 0, 0)) if mod_per_slab else (lambda i, j: (0, 0, 0))
    return pl.pallas_call(
        functools.partial(_ffn_kernel, is_grid),
        grid=(n_slabs, N_FF_STEPS),
        in_specs=[
            pl.BlockSpec((SLAB, D_MODEL), lambda i, j: (i, 0)),
            pl.BlockSpec((1, N_MOD, D_MODEL), mod_map),
            pl.BlockSpec((1, D_MODEL), lambda i, j: (0, 0)),
            pl.BlockSpec((FF_BLOCKS_PER_STEP, D_MODEL, 2 * FF_BLOCK), lambda i, j: (j, 0, 0)),
            pl.BlockSpec((FF_BLOCKS_PER_STEP, 9, 2 * FF_BLOCK), lambda i, j: (j, 0, 0)),
            pl.BlockSpec((FF_BLOCKS_PER_STEP, 1, 2 * FF_BLOCK), lambda i, j: (j, 0, 0)),
            pl.BlockSpec((FF_BLOCKS_PER_STEP, FF_BLOCK, D_MODEL), lambda i, j: (j, 0, 0)),
            pl.BlockSpec((1, D_MODEL), lambda i, j: (0, 0)),
        ],
        out_specs=pl.BlockSpec((SLAB, D_MODEL), lambda i, j: (i, 0)),
        out_shape=jax.ShapeDtypeStruct(x2d.shape, F32),
        scratch_shapes=[
            pltpu.VMEM((SLAB, D_MODEL), BF16),
            pltpu.VMEM((SLAB, D_MODEL), F32),
        ],
        compiler_params=pltpu.CompilerParams(
            dimension_semantics=("arbitrary", "arbitrary"), vmem_limit_bytes=VMEM_LIMIT),
        name="ffn_latent" if is_grid else "ffn_prompt",
    )(x2d, mod, g2, wup3, wc3, bc3, wdn3, gfin)


def _pad_lanes(v):
    return jnp.pad(v.reshape(1, -1), ((0, 0), (0, LANES - v.size)))


def kernel(x_prompt, x_sample, state_ssd_fwd, state_ssd_bwd, c, c_ctx, g_norm1, g_norm2, w_ada, b_ada, w_in, w_conv_short, w_conv_ssd, b_conv_ssd, dt_bias, a_log, d_skip, g_ssd_norm, w_out, w_up, w_ffn_conv, b_ffn_conv, w_down, g_final):
    depth = w_in.shape[0]
    assert depth == 1, "kernel is written for the single-layer problem"
    bp, seq = x_prompt.shape[0], x_prompt.shape[1]
    bd, dec_seq = x_sample.shape[0], x_sample.shape[1]
    assert seq == 256 and dec_seq == SLAB and (bp * seq) % SLAB == 0

    n_c = bd + 1
    rows = -(-n_c // 8) * 8
    cvec = jnp.concatenate([c, c_ctx[None], jnp.zeros((rows - n_c, D_MODEL), F32)], axis=0)
    mod = _mod_call(cvec, w_ada[0], b_ada[0].reshape(1, -1)).reshape(rows, N_MOD, D_MODEL)
    mod_lat = mod[:bd]
    mod_ctx = mod[bd:bd + 1]

    w_in_main = _cast_bf16(w_in[0], DT_COL)
    w_dt = jnp.pad(w_in[0][:, DT_COL:], ((0, 0), (0, LANES - 2 * SSD_HEADS))).astype(BF16)
    idx = jnp.arange(CHUNK)
    ltri = (idx[:, None] >= idx[None, :]).astype(BF16)
    utri = (idx[:, None] <= idx[None, :]).astype(BF16)
    expand = (jnp.arange(LANES)[:, None] == (jnp.arange(2 * SSD_WIDTH)[None, :] // SSD_HEAD_DIM)).astype(BF16)
    mixer_consts = (
        g_norm1[0].reshape(1, -1), w_in_main, w_dt, w_conv_short[0], w_conv_ssd[0], b_conv_ssd[0].reshape(1, -1),
        _pad_lanes(dt_bias[0]), _pad_lanes(a_log[0]),
        jnp.repeat(d_skip[0], SSD_HEAD_DIM).reshape(1, -1), g_ssd_norm[0].reshape(1, -1),
        _cast_bf16(w_out[0], D_MODEL), ltri, utri, expand,
    )
    wup3 = _up_blocks(w_up[0])
    pad_blocks = ((0, N_FF_PADDED - N_FF_BLOCKS), (0, 0), (0, 0))
    wc3 = jnp.pad(w_ffn_conv[0].reshape(9, 2, N_FF_BLOCKS, FF_BLOCK).transpose(2, 0, 1, 3)
                  .reshape(N_FF_BLOCKS, 9, 2 * FF_BLOCK), pad_blocks)
    bc3 = jnp.pad(b_ffn_conv[0].reshape(2, N_FF_BLOCKS, FF_BLOCK).transpose(1, 0, 2)
                  .reshape(N_FF_BLOCKS, 1, 2 * FF_BLOCK), pad_blocks)
    wdn3 = _cast_bf16(w_down[0], D_MODEL, N_FF_PADDED - N_FF_BLOCKS).reshape(N_FF_PADDED, FF_BLOCK, D_MODEL)
    ffn_consts = (g_norm2[0].reshape(1, -1), wup3, wc3, bc3, wdn3, g_final.reshape(1, -1))

    state_rows = SSD_HEADS * SSD_HEAD_DIM
    init_f = state_ssd_fwd[:, 0].reshape(bd, state_rows, D_STATE)
    init_b = state_ssd_bwd[:, 0].reshape(bd, state_rows, D_STATE)

    xp2d = x_prompt.reshape(bp * seq, D_MODEL)
    xs2d = x_sample.reshape(bd * dec_seq, D_MODEL)

    xp1, s_f, s_b = _mixer_call(xp2d, mod_ctx, False, seq, None, None, mixer_consts)
    (xs1,) = _mixer_call(xs2d, mod_lat, True, dec_seq, init_f, init_b, mixer_consts)

    y_prompt = _ffn_call(xp1, mod_ctx, False, False, ffn_consts).reshape(x_prompt.shape)
    y_sample = _ffn_call(xs1, mod_lat, True, True, ffn_consts).reshape(x_sample.shape)

    state_shape = (bp, depth, SSD_HEADS, SSD_HEAD_DIM, D_STATE)
    return (y_prompt, y_sample, s_f.reshape(state_shape), s_b.reshape(state_shape))
```

```python
import functools

import jax
import jax.numpy as jnp
from jax import lax
from jax.experimental import pallas as pl
from jax.experimental.pallas import tpu as pltpu

F32 = jnp.float32
BF16 = jnp.bfloat16

D_MODEL = 1024
GRID_W = 64
CONV_WIDTH = 512
SSD_WIDTH = 512
SSD_HEAD_DIM = 64
SSD_HEADS = 8
SSD_GROUPS = 2
D_STATE = 128
CHUNK = 128
D_FF = 2816
N_MOD = 6
EPS = 1e-6

SLAB = 1024
N_CHUNKS = SLAB // CHUNK
ROW_BLOCK = 256
PAD = 8
FF_BLOCK = 256
N_FF_BLOCKS = D_FF // FF_BLOCK
FF_BLOCKS_PER_STEP = 4
N_FF_STEPS = -(-N_FF_BLOCKS // FF_BLOCKS_PER_STEP)
N_FF_PADDED = N_FF_STEPS * FF_BLOCKS_PER_STEP
LANES = 128
DT_COL = 3072
VMEM_LIMIT = 56 * 1024 * 1024


def _silu(v):
    return v / (1.0 + jnp.exp(-v))


def _softplus(v):
    return jnp.maximum(v, 0.0) + jnp.log1p(jnp.exp(-jnp.abs(v)))


def _split3(v):
    hi = v.astype(BF16)
    r1 = v - hi.astype(F32)
    mid = r1.astype(BF16)
    lo = (r1 - mid.astype(F32)).astype(BF16)
    return hi, mid, lo


def _dot(a, b):
    return jnp.dot(a, b, preferred_element_type=F32)


def _dot3(m, v):
    hi, mid, lo = _split3(v)
    return _dot(m, hi) + _dot(m, mid) + _dot(m, lo)


def _expand3(v, expand):
    hi, mid, lo = _split3(v)
    return _dot(hi, expand) + _dot(mid, expand) + _dot(lo, expand)


def _conv3_rows(win, w_ref):
    n = win.shape[0]
    prev = pltpu.roll(win, 1, 0)[PAD:PAD + CHUNK]
    cur = win[PAD:PAD + CHUNK]
    nxt = pltpu.roll(win, n - 1, 0)[PAD:PAD + CHUNK]
    return prev * w_ref[0:1, :] + cur * w_ref[1:2, :] + nxt * w_ref[2:3, :]


def _mod_kernel(c_ref, w_ref, b_ref, o_ref):
    s = _silu(c_ref[...]).astype(BF16)
    o_ref[...] = _dot(s, w_ref[...].astype(BF16)) + b_ref[...]


def _mod_call(cvec, w_ada, b_ada):
    rows = cvec.shape[0]
    return pl.pallas_call(
        _mod_kernel,
        grid=(N_MOD,),
        in_specs=[
            pl.BlockSpec((rows, D_MODEL), lambda j: (0, 0)),
            pl.BlockSpec((D_MODEL, D_MODEL), lambda j: (0, j)),
            pl.BlockSpec((1, D_MODEL), lambda j: (0, j)),
        ],
        out_specs=pl.BlockSpec((rows, D_MODEL), lambda j: (0, j)),
        out_shape=jax.ShapeDtypeStruct((rows, N_MOD * D_MODEL), F32),
        compiler_params=pltpu.CompilerParams(dimension_semantics=("arbitrary",)),
        name="mod_vectors",
    )(cvec, w_ada, b_ada)


def _cast_kernel(n_src_blocks, src_ref, dst_ref):
    i = pl.program_id(0)

    @pl.when(i < n_src_blocks)
    def _():
        dst_ref[...] = src_ref[...].astype(BF16)

    @pl.when(i >= n_src_blocks)
    def _():
        dst_ref[...] = jnp.zeros_like(dst_ref)


def _cast_bf16(w, cols, pad_row_blocks=0):
    n_src = w.shape[0] // ROW_BLOCK
    n_dst = n_src + pad_row_blocks
    return pl.pallas_call(
        functools.partial(_cast_kernel, n_src),
        grid=(n_dst,),
        in_specs=[pl.BlockSpec((ROW_BLOCK, cols), lambda i: (jnp.minimum(i, n_src - 1), 0))],
        out_specs=pl.BlockSpec((ROW_BLOCK, cols), lambda i: (i, 0)),
        out_shape=jax.ShapeDtypeStruct((n_dst * ROW_BLOCK, cols), BF16),
        compiler_params=pltpu.CompilerParams(dimension_semantics=("arbitrary",)),
        name="cast_bf16",
    )(w)


def _up_blocks_kernel(src_ref, dst_ref):
    j = pl.program_id(0)

    @pl.when(j < N_FF_BLOCKS)
    def _():
        dst_ref[0] = src_ref[...].astype(BF16)

    @pl.when(j >= N_FF_BLOCKS)
    def _():
        dst_ref[...] = jnp.zeros_like(dst_ref)


def _up_blocks(w_up):
    last = N_FF_BLOCKS - 1
    return pl.pallas_call(
        _up_blocks_kernel,
        grid=(N_FF_PADDED, 2),
        in_specs=[pl.BlockSpec((D_MODEL, FF_BLOCK), lambda j, h: (0, h * N_FF_BLOCKS + jnp.minimum(j, last)))],
        out_specs=pl.BlockSpec((1, D_MODEL, FF_BLOCK), lambda j, h: (j, 0, h)),
        out_shape=jax.ShapeDtypeStruct((N_FF_PADDED, D_MODEL, 2 * FF_BLOCK), BF16),
        compiler_params=pltpu.CompilerParams(dimension_semantics=("arbitrary", "arbitrary")),
        name="up_blocks",
    )(w_up)


def _mixer_kernel(seg_chunks, has_init, *refs):
    refs = list(refs)
    x_ref, mod_ref = refs[:2]
    refs = refs[2:]
    if has_init:
        initf_ref, initb_ref = refs[:2]
        refs = refs[2:]
    (g1_ref, win_ref, wdt_ref, wcs_ref, wcx_ref, bcx_ref, dtb_ref, alog_ref, dskip_ref, gssd_ref, wout_ref,
     ltri_ref, utri_ref, exp_ref) = refs[:14]
    refs = refs[14:]
    out_ref = refs[0]
    refs = refs[1:]
    if not has_init:
        sfo_ref, sbo_ref = refs[:2]
        refs = refs[2:]
    (gchc_ref, xbc_ref, gb_ref, z_ref, dt_ref, e_ref, xs_ref, bc_ref, ymix_ref, sfin_ref, csb_ref,
     sf_ref, sb_ref) = refs

    shift_m = mod_ref[0, 0:1, :]
    scale_m = mod_ref[0, 1:2, :]
    gate_m = mod_ref[0, 2:3, :]

    seg_len = seg_chunks * CHUNK
    zpad_a = jnp.zeros((PAD, CONV_WIDTH), F32)
    zpad_x = jnp.zeros((PAD, 2 * SSD_WIDTH), F32)
    for s in range(SLAB // seg_len + 1):
        gap = slice(s * (seg_len + PAD), s * (seg_len + PAD) + PAD)
        gchc_ref[gap, :] = zpad_a
        xbc_ref[gap, :] = zpad_x

    def staged_row(r):
        return pl.multiple_of(r + lax.div(r, seg_len) * PAD, PAD)

    def proj_body(i, carry):
        r0 = pl.multiple_of(i * ROW_BLOCK, ROW_BLOCK)
        rp = pl.multiple_of(staged_row(i * ROW_BLOCK) + PAD, PAD)
        xb = x_ref[pl.ds(r0, ROW_BLOCK), :]
        ms = jnp.mean(xb * xb, axis=-1, keepdims=True)
        h = (xb * lax.rsqrt(ms + EPS)) * g1_ref[...] * (1.0 + scale_m) + shift_m
        hb = h.astype(BF16)

        def proj(lo, hi):
            return _dot(hb, win_ref[:, lo:hi])

        gchc_ref[pl.ds(rp, ROW_BLOCK), :] = proj(0, 512) * proj(1024, 1536)
        gb_ref[pl.ds(r0, ROW_BLOCK), :] = proj(512, 1024)
        z_ref[pl.ds(r0, ROW_BLOCK), :] = proj(1536, 2048)
        xbc_ref[pl.ds(rp, ROW_BLOCK), :] = proj(2048, 3072)
        dt_ref[pl.ds(r0, ROW_BLOCK), :] = _softplus(_dot(hb, wdt_ref[...]) + dtb_ref[...])
        return carry

    lax.fori_loop(0, SLAB // ROW_BLOCK, proj_body, 0)

    a_row = -jnp.exp(alog_ref[...])
    lane_t = lax.broadcasted_iota(jnp.int32, (CHUNK, LANES), 1)
    lane_1 = lax.broadcasted_iota(jnp.int32, (1, LANES), 1)
    is_fwd_t = lane_t < SSD_HEADS
    is_fwd_1 = lane_1 < SSD_HEADS

    def chunk_decay_row(e):
        tot = jnp.where(is_fwd_1, e[CHUNK - 1:CHUNK, :], e[0:1, :])
        dec = jnp.broadcast_to(jnp.exp(tot), (8, LANES))
        return tot, _expand3(dec, exp_ref[...])[0:1, :]

    pair_is_sequence = seg_chunks == 2
    assert pair_is_sequence != has_init and seg_chunks in (2, N_CHUNKS)
    n_pairs = N_CHUNKS // 2

    def fwd_convs(c):
        r0 = pl.multiple_of(c * CHUNK, CHUNK)
        rows = pl.ds(r0, CHUNK)
        w0 = staged_row(c * CHUNK)
        xc = _conv3_rows(xbc_ref[pl.ds(w0, CHUNK + 2 * PAD), :], wcx_ref)
        xc = _silu(xc + bcx_ref[...])
        xs_ref[rows, :] = xc[:, 0:SSD_WIDTH]
        bc_ref[rows, :] = xc[:, SSD_WIDTH:].astype(BF16)
        ca = _conv3_rows(gchc_ref[pl.ds(w0, CHUNK + 2 * PAD), :], wcs_ref)
        ymix_ref[rows, 0:CONV_WIDTH] = (gb_ref[rows, :] * ca).astype(BF16)
        return xc

    def fwd_decays(c):
        rows = pl.ds(pl.multiple_of(c * CHUNK, CHUNK), CHUNK)
        dtt = dt_ref[rows, :]
        da = dtt * a_row
        cs = _dot3(ltri_ref[...], da)
        rcs = _dot3(utri_ref[...], da)
        e = jnp.where(is_fwd_t, cs, rcs)
        e_ref[rows, :] = e
        tot, dec_row = chunk_decay_row(e)
        wst = (dtt * jnp.exp(tot - e)).astype(BF16)
        return _dot(wst, exp_ref[...]), dec_row

    def fwd_chunk_states(xc, wexp):
        xs = xc[:, 0:SSD_WIDTH]
        xst_f = (xs * wexp[:, 0:SSD_WIDTH]).astype(BF16)
        xst_b = (xs * wexp[:, SSD_WIDTH:]).astype(BF16)
        cst_f, cst_b = [], []
        for g in range(SSD_GROUPS):
            bt = jnp.transpose(xc[:, SSD_WIDTH + g * D_STATE:SSD_WIDTH + (g + 1) * D_STATE]).astype(BF16)
            cst_f.append(_dot(bt, xst_f[:, g * 256:(g + 1) * 256]))
            cst_b.append(_dot(bt, xst_b[:, g * 256:(g + 1) * 256]))
        return jnp.concatenate(cst_f, axis=1), jnp.concatenate(cst_b, axis=1)

    if has_init:
        sf_ref[...] = jnp.transpose(initf_ref[0])
        sb_ref[...] = jnp.transpose(initb_ref[0])

    def fwd_pair(p, carry):
        cs_pair = (2 * p, 2 * p + 1)
        xcs = [fwd_convs(c) for c in cs_pair]
        decays = [fwd_decays(c) for c in cs_pair]
        states = [fwd_chunk_states(xc, wexp) for xc, (wexp, _) in zip(xcs, decays)]
        for c, (_, cst_b) in zip(cs_pair, states):
            csb_ref[c] = cst_b
        (cst0, _), (cst1, _) = states
        dec0, dec1 = decays[0][1][:, 0:SSD_WIDTH], decays[1][1][:, 0:SSD_WIDTH]
        if pair_is_sequence:
            s1 = cst0
        else:
            s0 = sf_ref[...]
            sfin_ref[cs_pair[0]] = s0
            s1 = s0 * dec0 + cst0
        sfin_ref[cs_pair[1]] = s1
        s2 = s1 * dec1 + cst1
        if pair_is_sequence:
            sfo_ref[p] = jnp.transpose(s2)
        else:
            sf_ref[...] = s2
        return carry

    lax.fori_loop(0, n_pairs, fwd_pair, 0)

    row_t = lax.broadcasted_iota(jnp.int32, (CHUNK, CHUNK), 0)
    col_t = lax.broadcasted_iota(jnp.int32, (CHUNK, CHUNK), 1)
    causal = row_t >= col_t
    anti = col_t >= row_t
    low_half = lane_t < SSD_HEAD_DIM

    def chunk_tables(c):
        rows = pl.ds(pl.multiple_of(c * CHUNK, CHUNK), CHUNK)
        e = e_ref[rows, :]
        xs = xs_ref[rows, :]
        dtexp = _dot(dt_ref[rows, :].astype(BF16), exp_ref[...])
        ecs = _dot(jnp.exp(e).astype(BF16), exp_ref[...])
        return dict(rows=rows, e=e, et=jnp.transpose(e), xs=xs, bc=bc_ref[rows, :], ecs=ecs,
                    xdt_f=xs * dtexp[:, 0:SSD_WIDTH], xdt_b=xs * dtexp[:, SSD_WIDTH:])

    def chunk_output(t, s_f, s_b):
        e, et, xs, bc, ecs, xdt_f, xdt_b = t["e"], t["et"], t["xs"], t["bc"], t["ecs"], t["xdt_f"], t["xdt_b"]
        y_parts = []
        for g in range(SSD_GROUPS):
            b_g = bc[:, g * D_STATE:(g + 1) * D_STATE]
            c_g = bc[:, 256 + g * D_STATE:256 + (g + 1) * D_STATE]
            gmat = lax.dot_general(c_g, b_g, (((1,), (1,)), ((), ())), preferred_element_type=F32)
            y_off = jnp.zeros((CHUNK, 256), F32)
            if s_f is not None:
                y_off = y_off + _dot(c_g, s_f[:, g * 256:(g + 1) * 256].astype(BF16)) \
                    * ecs[:, g * 256:(g + 1) * 256]
            if s_b is not None:
                y_off = y_off + _dot(c_g, s_b[:, g * 256:(g + 1) * 256].astype(BF16)) \
                    * ecs[:, SSD_WIDTH + g * 256:SSD_WIDTH + (g + 1) * 256]
            for pair in range(2):
                p0 = g * 256 + pair * LANES
                xf_pair = xdt_f[:, p0:p0 + LANES]
                xb_pair = xdt_b[:, p0:p0 + LANES]
                y_pair = y_off[:, pair * LANES:(pair + 1) * LANES]
                for sub in range(2):
                    head = g * 4 + pair * 2 + sub
                    keep = low_half if sub == 0 else jnp.logical_not(low_half)
                    rhs_f = jnp.where(keep, xf_pair, 0.0).astype(BF16)
                    rhs_b = jnp.where(keep, xb_pair, 0.0).astype(BF16)
                    jf, jb = head, SSD_HEADS + head
                    d_f = e[:, jf:jf + 1] - et[jf:jf + 1, :]
                    d_b = e[:, jb:jb + 1] - et[jb:jb + 1, :]
                    sc_f = (gmat * jnp.where(causal, jnp.exp(d_f), 0.0)).astype(BF16)
                    sc_b = (gmat * jnp.where(anti, jnp.exp(d_b), 0.0)).astype(BF16)
                    y_pair = y_pair + _dot(sc_f, rhs_f) + _dot(sc_b, rhs_b)
                y_parts.append(y_pair)
        y = jnp.concatenate(y_parts, axis=1) + xs * dskip_ref[...]
        yz = y * _silu(z_ref[t["rows"], :])
        ms = jnp.mean(yz * yz, axis=-1, keepdims=True)
        yn = (yz * lax.rsqrt(ms + EPS)) * gssd_ref[...]
        ymix_ref[t["rows"], CONV_WIDTH:] = yn.astype(BF16)

    def bwd_pair(k, carry):
        q = n_pairs - 1 - k
        c0, c1 = 2 * q, 2 * q + 1
        t1, t0 = chunk_tables(c1), chunk_tables(c0)
        dec1 = chunk_decay_row(t1["e"])[1][:, SSD_WIDTH:]
        dec0 = chunk_decay_row(t0["e"])[1][:, SSD_WIDTH:]
        if pair_is_sequence:
            sb1 = None
            sb0 = csb_ref[c1]
            sf0 = None
        else:
            sb1 = sb_ref[...]
            sb0 = sb1 * dec1 + csb_ref[c1]
            sf0 = sfin_ref[c0]
        chunk_output(t1, sfin_ref[c1], sb1)
        chunk_output(t0, sf0, sb0)
        s_new = sb0 * dec0 + csb_ref[c0]
        if pair_is_sequence:
            sbo_ref[q] = jnp.transpose(s_new)
        else:
            sb_ref[...] = s_new
        return carry

    lax.fori_loop(0, n_pairs, bwd_pair, 0)

    def out_body(i, carry):
        r0 = pl.multiple_of(i * ROW_BLOCK, ROW_BLOCK)
        rows = pl.ds(r0, ROW_BLOCK)
        o = _dot(ymix_ref[rows, :], wout_ref[...])
        out_ref[rows, :] = x_ref[rows, :] + gate_m * o
        return carry

    lax.fori_loop(0, SLAB // ROW_BLOCK, out_body, 0)


def _const_spec(shape):
    nd = len(shape)
    return pl.BlockSpec(shape, lambda i: (0,) * nd)


def _mixer_call(x2d, mod, mod_per_slab, seg_len, init_f, init_b, consts):
    n_slabs = x2d.shape[0] // SLAB
    seg_chunks = seg_len // CHUNK
    segs_per_slab = SLAB // seg_len
    has_init = init_f is not None
    state_rows = SSD_HEADS * SSD_HEAD_DIM
    staged_rows = SLAB + PAD * (1 + segs_per_slab)

    mod_map = (lambda i: (i, 0, 0)) if mod_per_slab else (lambda i: (0, 0, 0))
    in_specs = [
        pl.BlockSpec((SLAB, D_MODEL), lambda i: (i, 0)),
        pl.BlockSpec((1, N_MOD, D_MODEL), mod_map),
    ]
    args = [x2d, mod]
    if has_init:
        in_specs += [pl.BlockSpec((1, state_rows, D_STATE), lambda i: (i, 0, 0))] * 2
        args += [init_f, init_b]
    in_specs += [_const_spec(a.shape) for a in consts]
    args += list(consts)

    out_specs = [pl.BlockSpec((SLAB, D_MODEL), lambda i: (i, 0))]
    out_shape = [jax.ShapeDtypeStruct(x2d.shape, F32)]
    if not has_init:
        n_seq = n_slabs * segs_per_slab
        out_specs += [pl.BlockSpec((segs_per_slab, state_rows, D_STATE), lambda i: (i, 0, 0))] * 2
        out_shape += [jax.ShapeDtypeStruct((n_seq, state_rows, D_STATE), F32)] * 2

    scratch = [
        pltpu.VMEM((staged_rows, CONV_WIDTH), F32),
        pltpu.VMEM((staged_rows, 2 * SSD_WIDTH), F32),
        pltpu.VMEM((SLAB, CONV_WIDTH), F32),
        pltpu.VMEM((SLAB, SSD_WIDTH), F32),
        pltpu.VMEM((SLAB, LANES), F32),
        pltpu.VMEM((SLAB, LANES), F32),
        pltpu.VMEM((SLAB, SSD_WIDTH), F32),
        pltpu.VMEM((SLAB, 2 * SSD_GROUPS * D_STATE), BF16),
        pltpu.VMEM((SLAB, D_MODEL), BF16),
        pltpu.VMEM((N_CHUNKS, D_STATE, SSD_WIDTH), F32),
        pltpu.VMEM((N_CHUNKS, D_STATE, SSD_WIDTH), F32),
        pltpu.VMEM((D_STATE, SSD_WIDTH), F32),
        pltpu.VMEM((D_STATE, SSD_WIDTH), F32),
    ]
    return pl.pallas_call(
        functools.partial(_mixer_kernel, seg_chunks, has_init),
        grid=(n_slabs,),
        in_specs=in_specs,
        out_specs=out_specs,
        out_shape=out_shape,
        scratch_shapes=scratch,
        compiler_params=pltpu.CompilerParams(
            dimension_semantics=("arbitrary",), vmem_limit_bytes=VMEM_LIMIT),
        name="mixer_latent" if has_init else "mixer_prompt",
    )(*args)


def _ffn_conv(u, wc_ref, bc_ref, is_grid):
    n = u.shape[0]
    row = lax.broadcasted_iota(jnp.int32, (n, 1), 0)
    period = GRID_W if is_grid else 256
    pos = jnp.bitwise_and(row, period - 1)
    um = jnp.where(pos == 0, 0.0, pltpu.roll(u, 1, 0))
    up = jnp.where(pos == period - 1, 0.0, pltpu.roll(u, n - 1, 0))

    def taps(di):
        return um * wc_ref[3 * di:3 * di + 1, :] + u * wc_ref[3 * di + 1:3 * di + 2, :] \
            + up * wc_ref[3 * di + 2:3 * di + 3, :]

    out = taps(1) + bc_ref[...]
    if is_grid:
        zeros = jnp.zeros((GRID_W, u.shape[1]), F32)
        out = out + jnp.concatenate([zeros, taps(0)[:n - GRID_W]], axis=0)
        out = out + jnp.concatenate([taps(2)[GRID_W:], zeros], axis=0)
    return out


def _ffn_kernel(is_grid, x_ref, mod_ref, g2_ref, wup_ref, wc_ref, bc_ref, wdn_ref, gfin_ref, out_ref,
                h2_ref, acc_ref):
    j = pl.program_id(1)

    @pl.when(j == 0)
    def _():
        shift_f = mod_ref[0, 3:4, :]
        scale_f = mod_ref[0, 4:5, :]
        x = x_ref[...]
        ms = jnp.mean(x * x, axis=-1, keepdims=True)
        h2 = (x * lax.rsqrt(ms + EPS)) * g2_ref[...] * (1.0 + scale_f) + shift_f
        h2_ref[...] = h2.astype(BF16)
        acc_ref[...] = jnp.zeros_like(acc_ref)

    def up_proj(b):
        return _dot(h2_ref[...], wup_ref[b])

    u_next = up_proj(0)
    for b in range(FF_BLOCKS_PER_STEP):
        u = u_next
        if b + 1 < FF_BLOCKS_PER_STEP:
            u_next = up_proj(b + 1)
        u = _ffn_conv(u, wc_ref.at[b], bc_ref.at[b], is_grid)
        a = (_silu(u[:, 0:FF_BLOCK]) * u[:, FF_BLOCK:]).astype(BF16)
        acc_ref[...] += _dot(a, wdn_ref[b])

    @pl.when(j == N_FF_STEPS - 1)
    def _():
        gate_f = mod_ref[0, 5:6, :]
        x2 = x_ref[...] + gate_f * acc_ref[...]
        ms = jnp.mean(x2 * x2, axis=-1, keepdims=True)
        out_ref[...] = (x2 * lax.rsqrt(ms + EPS)) * gfin_ref[...]


def _ffn_call(x2d, mod, mod_per_slab, is_grid, consts):
    n_slabs = x2d.shape[0] // SLAB
    g2, wup3, wc3, bc3, wdn3, gfin = consts
    mod_map = (lambda i, j: (i, 0, 0)) if mod_per_slab else (lambda i, j: (0, 0, 0))
    return pl.pallas_call(
        functools.partial(_ffn_kernel, is_grid),
        grid=(n_slabs, N_FF_STEPS),
        in_specs=[
            pl.BlockSpec((SLAB, D_MODEL), lambda i, j: (i, 0)),
            pl.BlockSpec((1, N_MOD, D_MODEL), mod_map),
            pl.BlockSpec((1, D_MODEL), lambda i, j: (0, 0)),
            pl.BlockSpec((FF_BLOCKS_PER_STEP, D_MODEL, 2 * FF_BLOCK), lambda i, j: (j, 0, 0)),
            pl.BlockSpec((FF_BLOCKS_PER_STEP, 9, 2 * FF_BLOCK), lambda i, j: (j, 0, 0)),
            pl.BlockSpec((FF_BLOCKS_PER_STEP, 1, 2 * FF_BLOCK), lambda i, j: (j, 0, 0)),
            pl.BlockSpec((FF_BLOCKS_PER_STEP, FF_BLOCK, D_MODEL), lambda i, j: (j, 0, 0)),
            pl.BlockSpec((1, D_MODEL), lambda i, j: (0, 0)),
        ],
        out_specs=pl.BlockSpec((SLAB, D_MODEL), lambda i, j: (i, 0)),
        out_shape=jax.ShapeDtypeStruct(x2d.shape, F32),
        scratch_shapes=[
            pltpu.VMEM((SLAB, D_MODEL), BF16),
            pltpu.VMEM((SLAB, D_MODEL), F32),
        ],
        compiler_params=pltpu.CompilerParams(
            dimension_semantics=("arbitrary", "arbitrary"), vmem_limit_bytes=VMEM_LIMIT),
        name="ffn_latent" if is_grid else "ffn_prompt",
    )(x2d, mod, g2, wup3, wc3, bc3, wdn3, gfin)


def _pad_lanes(v):
    return jnp.pad(v.reshape(1, -1), ((0, 0), (0, LANES - v.size)))


def kernel(x_prompt, x_sample, state_ssd_fwd, state_ssd_bwd, c, c_ctx, g_norm1, g_norm2, w_ada, b_ada, w_in, w_conv_short, w_conv_ssd, b_conv_ssd, dt_bias, a_log, d_skip, g_ssd_norm, w_out, w_up, w_ffn_conv, b_ffn_conv, w_down, g_final):
    depth = w_in.shape[0]
    assert depth == 1, "kernel is written for the single-layer problem"
    bp, seq = x_prompt.shape[0], x_prompt.shape[1]
    bd, dec_seq = x_sample.shape[0], x_sample.shape[1]
    assert seq == 256 and dec_seq == SLAB and (bp * seq) % SLAB == 0

    n_c = bd + 1
    rows = -(-n_c // 8) * 8
    cvec = jnp.concatenate([c, c_ctx[None], jnp.zeros((rows - n_c, D_MODEL), F32)], axis=0)
    mod = _mod_call(cvec, w_ada[0], b_ada[0].reshape(1, -1)).reshape(rows, N_MOD, D_MODEL)
    mod_lat = mod[:bd]
    mod_ctx = mod[bd:bd + 1]

    w_in_main = _cast_bf16(w_in[0], DT_COL)
    w_dt = jnp.pad(w_in[0][:, DT_COL:], ((0, 0), (0, LANES - 2 * SSD_HEADS))).astype(BF16)
    idx = jnp.arange(CHUNK)
    ltri = (idx[:, None] >= idx[None, :]).astype(BF16)
    utri = (idx[:, None] <= idx[None, :]).astype(BF16)
    expand = (jnp.arange(LANES)[:, None] == (jnp.arange(2 * SSD_WIDTH)[None, :] // SSD_HEAD_DIM)).astype(BF16)
    mixer_consts = (
        g_norm1[0].reshape(1, -1), w_in_main, w_dt, w_conv_short[0], w_conv_ssd[0], b_conv_ssd[0].reshape(1, -1),
        _pad_lanes(dt_bias[0]), _pad_lanes(a_log[0]),
        jnp.repeat(d_skip[0], SSD_HEAD_DIM).reshape(1, -1), g_ssd_norm[0].reshape(1, -1),
        _cast_bf16(w_out[0], D_MODEL), ltri, utri, expand,
    )
    wup3 = _up_blocks(w_up[0])
    pad_blocks = ((0, N_FF_PADDED - N_FF_BLOCKS), (0, 0), (0, 0))
    wc3 = jnp.pad(w_ffn_conv[0].reshape(9, 2, N_FF_BLOCKS, FF_BLOCK).transpose(2, 0, 1, 3)
                  .reshape(N_FF_BLOCKS, 9, 2 * FF_BLOCK), pad_blocks)
    bc3 = jnp.pad(b_ffn_conv[0].reshape(2, N_FF_BLOCKS, FF_BLOCK).transpose(1, 0, 2)
                  .reshape(N_FF_BLOCKS, 1, 2 * FF_BLOCK), pad_blocks)
    wdn3 = _cast_bf16(w_down[0], D_MODEL, N_FF_PADDED - N_FF_BLOCKS).reshape(N_FF_PADDED, FF_BLOCK, D_MODEL)
    ffn_consts = (g_norm2[0].reshape(1, -1), wup3, wc3, bc3, wdn3, g_final.reshape(1, -1))

    state_rows = SSD_HEADS * SSD_HEAD_DIM
    init_f = state_ssd_fwd[:, 0].reshape(bd, state_rows, D_STATE)
    init_b = state_ssd_bwd[:, 0].reshape(bd, state_rows, D_STATE)

    xp2d = x_prompt.reshape(bp * seq, D_MODEL)
    xs2d = x_sample.reshape(bd * dec_seq, D_MODEL)

    xp1, s_f, s_b = _mixer_call(xp2d, mod_ctx, False, seq, None, None, mixer_consts)
    (xs1,) = _mixer_call(xs2d, mod_lat, True, dec_seq, init_f, init_b, mixer_consts)

    y_prompt = _ffn_call(xp1, mod_ctx, False, False, ffn_consts).reshape(x_prompt.shape)
    y_sample = _ffn_call(xs1, mod_lat, True, True, ffn_consts).reshape(x_sample.shape)

    state_shape = (bp, depth, SSD_HEADS, SSD_HEAD_DIM, D_STATE)
    return (y_prompt, y_sample, s_f.reshape(state_shape), s_b.reshape(state_shape))
```

```python
import functools

import jax
import jax.numpy as jnp
from jax import lax
from jax.experimental import pallas as pl
from jax.experimental.pallas import tpu as pltpu

F32 = jnp.float32
BF16 = jnp.bfloat16

D_MODEL = 1024
GRID_W = 64
CONV_WIDTH = 512
SSD_WIDTH = 512
SSD_HEAD_DIM = 64
SSD_HEADS = 8
SSD_GROUPS = 2
D_STATE = 128
CHUNK = 128
D_FF = 2816
N_MOD = 6
EPS = 1e-6

SLAB = 1024
N_CHUNKS = SLAB // CHUNK
ROW_BLOCK = 256
PAD = 8
FF_BLOCK = 256
N_FF_BLOCKS = D_FF // FF_BLOCK
FF_BLOCKS_PER_STEP = 3
N_FF_STEPS = -(-N_FF_BLOCKS // FF_BLOCKS_PER_STEP)
N_FF_PADDED = N_FF_STEPS * FF_BLOCKS_PER_STEP
LANES = 128
DT_COL = 3072
VMEM_LIMIT = 56 * 1024 * 1024


def _silu(v):
    return v / (1.0 + jnp.exp(-v))


def _softplus(v):
    return jnp.maximum(v, 0.0) + jnp.log1p(jnp.exp(-jnp.abs(v)))


def _split3(v):
    hi = v.astype(BF16)
    r1 = v - hi.astype(F32)
    mid = r1.astype(BF16)
    lo = (r1 - mid.astype(F32)).astype(BF16)
    return hi, mid, lo


def _dot(a, b):
    return jnp.dot(a, b, preferred_element_type=F32)


def _dot3(m3, v):
    return _dot(m3, jnp.concatenate(_split3(v), axis=0))


def _expand3(v, expand3):
    return _dot(jnp.concatenate(_split3(v), axis=1), expand3)


def _conv3_rows(win, w_ref):
    n = win.shape[0]
    prev = pltpu.roll(win, 1, 0)[PAD:PAD + CHUNK]
    cur = win[PAD:PAD + CHUNK]
    nxt = pltpu.roll(win, n - 1, 0)[PAD:PAD + CHUNK]
    return prev * w_ref[0:1, :] + cur * w_ref[1:2, :] + nxt * w_ref[2:3, :]


def _mod_kernel(c_ref, w_ref, b_ref, o_ref):
    s = _silu(c_ref[...]).astype(BF16)
    o_ref[...] = _dot(s, w_ref[...].astype(BF16)) + b_ref[...]


def _mod_call(cvec, w_ada, b_ada):
    rows = cvec.shape[0]
    return pl.pallas_call(
        _mod_kernel,
        grid=(N_MOD,),
        in_specs=[
            pl.BlockSpec((rows, D_MODEL), lambda j: (0, 0)),
            pl.BlockSpec((D_MODEL, D_MODEL), lambda j: (0, j)),
            pl.BlockSpec((1, D_MODEL), lambda j: (0, j)),
        ],
        out_specs=pl.BlockSpec((rows, D_MODEL), lambda j: (0, j)),
        out_shape=jax.ShapeDtypeStruct((rows, N_MOD * D_MODEL), F32),
        compiler_params=pltpu.CompilerParams(dimension_semantics=("arbitrary",)),
        name="mod_vectors",
    )(cvec, w_ada, b_ada)


def _cast_kernel(n_src_blocks, src_ref, dst_ref):
    i = pl.program_id(0)

    @pl.when(i < n_src_blocks)
    def _():
        dst_ref[...] = src_ref[...].astype(BF16)

    @pl.when(i >= n_src_blocks)
    def _():
        dst_ref[...] = jnp.zeros_like(dst_ref)


def _cast_bf16(w, cols, pad_row_blocks=0):
    n_src = w.shape[0] // ROW_BLOCK
    n_dst = n_src + pad_row_blocks
    return pl.pallas_call(
        functools.partial(_cast_kernel, n_src),
        grid=(n_dst,),
        in_specs=[pl.BlockSpec((ROW_BLOCK, cols), lambda i: (jnp.minimum(i, n_src - 1), 0))],
        out_specs=pl.BlockSpec((ROW_BLOCK, cols), lambda i: (i, 0)),
        out_shape=jax.ShapeDtypeStruct((n_dst * ROW_BLOCK, cols), BF16),
        compiler_params=pltpu.CompilerParams(dimension_semantics=("arbitrary",)),
        name="cast_bf16",
    )(w)


def _up_blocks_kernel(gate_ref, value_ref, dst_ref):
    j = pl.program_id(0)

    @pl.when(j < N_FF_BLOCKS)
    def _():
        dst_ref[0, :, 0:FF_BLOCK] = gate_ref[...].astype(BF16)
        dst_ref[0, :, FF_BLOCK:] = value_ref[...].astype(BF16)

    @pl.when(j >= N_FF_BLOCKS)
    def _():
        dst_ref[...] = jnp.zeros_like(dst_ref)


def _up_blocks(w_up):
    last = N_FF_BLOCKS - 1
    return pl.pallas_call(
        _up_blocks_kernel,
        grid=(N_FF_PADDED,),
        in_specs=[
            pl.BlockSpec((D_MODEL, FF_BLOCK), lambda j: (0, jnp.minimum(j, last))),
            pl.BlockSpec((D_MODEL, FF_BLOCK), lambda j: (0, N_FF_BLOCKS + jnp.minimum(j, last))),
        ],
        out_specs=pl.BlockSpec((1, D_MODEL, 2 * FF_BLOCK), lambda j: (j, 0, 0)),
        out_shape=jax.ShapeDtypeStruct((N_FF_PADDED, D_MODEL, 2 * FF_BLOCK), BF16),
        compiler_params=pltpu.CompilerParams(dimension_semantics=("arbitrary",)),
        name="up_blocks",
    )(w_up, w_up)


def _mixer_kernel(seg_chunks, has_init, *refs):
    refs = list(refs)
    x_ref, mod_ref = refs[:2]
    refs = refs[2:]
    if has_init:
        initf_ref, initb_ref = refs[:2]
        refs = refs[2:]
    (g1_ref, win_ref, wdt_ref, wcs_ref, wcx_ref, bcx_ref, dtb_ref, alog_ref, dskip_ref, gssd_ref, wout_ref,
     tri3_ref, exp_ref, exp3_ref) = refs[:14]
    refs = refs[14:]
    out_ref = refs[0]
    refs = refs[1:]
    if not has_init:
        sfo_ref, sbo_ref = refs[:2]
        refs = refs[2:]
    (gchc_ref, xbc_ref, gb_ref, z_ref, dt_ref, e_ref, xs_ref, bc_ref, ymix_ref, sfin_ref, csb_ref,
     sf_ref, sb_ref) = refs

    shift_m = mod_ref[0, 0:1, :]
    scale_m = mod_ref[0, 1:2, :]
    gate_m = mod_ref[0, 2:3, :]

    seg_len = seg_chunks * CHUNK
    zpad_a = jnp.zeros((PAD, CONV_WIDTH), F32)
    zpad_x = jnp.zeros((PAD, 2 * SSD_WIDTH), F32)
    for s in range(SLAB // seg_len + 1):
        gap = slice(s * (seg_len + PAD), s * (seg_len + PAD) + PAD)
        gchc_ref[gap, :] = zpad_a
        xbc_ref[gap, :] = zpad_x

    def staged_row(r):
        return pl.multiple_of(r + lax.div(r, seg_len) * PAD, PAD)

    def proj_body(i, carry):
        r0 = pl.multiple_of(i * ROW_BLOCK, ROW_BLOCK)
        rp = pl.multiple_of(staged_row(i * ROW_BLOCK) + PAD, PAD)
        xb = x_ref[pl.ds(r0, ROW_BLOCK), :]
        ms = jnp.mean(xb * xb, axis=-1, keepdims=True)
        h = (xb * lax.rsqrt(ms + EPS)) * g1_ref[...] * (1.0 + scale_m) + shift_m
        hb = h.astype(BF16)

        def proj(lo, hi):
            return _dot(hb, win_ref[:, lo:hi])

        gchc_ref[pl.ds(rp, ROW_BLOCK), :] = proj(0, 512) * proj(1024, 1536)
        gb_ref[pl.ds(r0, ROW_BLOCK), :] = proj(512, 1024)
        z_ref[pl.ds(r0, ROW_BLOCK), :] = proj(1536, 2048)
        xbc_ref[pl.ds(rp, ROW_BLOCK), :] = proj(2048, 3072)
        dt_ref[pl.ds(r0, ROW_BLOCK), :] = _softplus(_dot(hb, wdt_ref[...]) + dtb_ref[...])
        return carry

    lax.fori_loop(0, SLAB // ROW_BLOCK, proj_body, 0)

    a_row = -jnp.exp(alog_ref[...])
    lane_t = lax.broadcasted_iota(jnp.int32, (CHUNK, LANES), 1)
    lane_1 = lax.broadcasted_iota(jnp.int32, (1, LANES), 1)
    is_fwd_t = lane_t < SSD_HEADS
    is_fwd_1 = lane_1 < SSD_HEADS

    def chunk_decay_row(e):
        tot = jnp.where(is_fwd_1, e[CHUNK - 1:CHUNK, :], e[0:1, :])
        dec = jnp.broadcast_to(jnp.exp(tot), (8, LANES))
        return tot, _expand3(dec, exp3_ref[...])[0:1, :]

    pair_is_sequence = seg_chunks == 2
    assert pair_is_sequence != has_init and seg_chunks in (2, N_CHUNKS)
    n_pairs = N_CHUNKS // 2

    def fwd_convs(c):
        r0 = pl.multiple_of(c * CHUNK, CHUNK)
        rows = pl.ds(r0, CHUNK)
        w0 = staged_row(c * CHUNK)
        xc = _conv3_rows(xbc_ref[pl.ds(w0, CHUNK + 2 * PAD), :], wcx_ref)
        xc = _silu(xc + bcx_ref[...])
        xs_ref[rows, :] = xc[:, 0:SSD_WIDTH]
        bc_ref[rows, :] = xc[:, SSD_WIDTH:].astype(BF16)
        ca = _conv3_rows(gchc_ref[pl.ds(w0, CHUNK + 2 * PAD), :], wcs_ref)
        ymix_ref[rows, 0:CONV_WIDTH] = (gb_ref[rows, :] * ca).astype(BF16)
        return xc

    def fwd_decays(c):
        rows = pl.ds(pl.multiple_of(c * CHUNK, CHUNK), CHUNK)
        dtt = dt_ref[rows, :]
        da = dtt * a_row
        sums = _dot3(tri3_ref[...], da)
        e = jnp.where(is_fwd_t, sums[0:CHUNK], sums[CHUNK:])
        e_ref[rows, :] = e
        tot, dec_row = chunk_decay_row(e)
        wst = (dtt * jnp.exp(tot - e)).astype(BF16)
        return _dot(wst, exp_ref[...]), dec_row

    def fwd_chunk_states(xc, wexp):
        xs = xc[:, 0:SSD_WIDTH]
        xst_f = (xs * wexp[:, 0:SSD_WIDTH]).astype(BF16)
        xst_b = (xs * wexp[:, SSD_WIDTH:]).astype(BF16)
        cst_f, cst_b = [], []
        for g in range(SSD_GROUPS):
            bt = jnp.transpose(xc[:, SSD_WIDTH + g * D_STATE:SSD_WIDTH + (g + 1) * D_STATE]).astype(BF16)
            cst_f.append(_dot(bt, xst_f[:, g * 256:(g + 1) * 256]))
            cst_b.append(_dot(bt, xst_b[:, g * 256:(g + 1) * 256]))
        return jnp.concatenate(cst_f, axis=1), jnp.concatenate(cst_b, axis=1)

    if has_init:
        sf_ref[...] = jnp.transpose(initf_ref[0])
        sb_ref[...] = jnp.transpose(initb_ref[0])

    def fwd_pair(p, carry):
        cs_pair = (2 * p, 2 * p + 1)
        xcs = [fwd_convs(c) for c in cs_pair]
        decays = [fwd_decays(c) for c in cs_pair]
        states = [fwd_chunk_states(xc, wexp) for xc, (wexp, _) in zip(xcs, decays)]
        for c, (_, cst_b) in zip(cs_pair, states):
            csb_ref[c] = cst_b
        (cst0, _), (cst1, _) = states
        dec0, dec1 = decays[0][1][:, 0:SSD_WIDTH], decays[1][1][:, 0:SSD_WIDTH]
        if pair_is_sequence:
            s1 = cst0
        else:
            s0 = sf_ref[...]
            sfin_ref[cs_pair[0]] = s0
            s1 = s0 * dec0 + cst0
        sfin_ref[cs_pair[1]] = s1
        s2 = s1 * dec1 + cst1
        if pair_is_sequence:
            sfo_ref[p] = jnp.transpose(s2)
        else:
            sf_ref[...] = s2
        return carry

    lax.fori_loop(0, n_pairs, fwd_pair, 0)

    row_t = lax.broadcasted_iota(jnp.int32, (CHUNK, CHUNK), 0)
    col_t = lax.broadcasted_iota(jnp.int32, (CHUNK, CHUNK), 1)
    causal = row_t >= col_t
    anti = col_t >= row_t
    low_half = lane_t < SSD_HEAD_DIM

    def chunk_tables(c):
        rows = pl.ds(pl.multiple_of(c * CHUNK, CHUNK), CHUNK)
        e = e_ref[rows, :]
        xs = xs_ref[rows, :]
        dtexp = _dot(dt_ref[rows, :].astype(BF16), exp_ref[...])
        ecs = _dot(jnp.exp(e).astype(BF16), exp_ref[...])
        return dict(rows=rows, e=e, et=jnp.transpose(e), xs=xs, bc=bc_ref[rows, :], ecs=ecs,
                    xdt_f=xs * dtexp[:, 0:SSD_WIDTH], xdt_b=xs * dtexp[:, SSD_WIDTH:])

    def chunk_output(t, s_f, s_b):
        e, et, xs, bc, ecs, xdt_f, xdt_b = t["e"], t["et"], t["xs"], t["bc"], t["ecs"], t["xdt_f"], t["xdt_b"]
        y_parts = []
        for g in range(SSD_GROUPS):
            b_g = bc[:, g * D_STATE:(g + 1) * D_STATE]
            c_g = bc[:, 256 + g * D_STATE:256 + (g + 1) * D_STATE]
            gmat = lax.dot_general(c_g, b_g, (((1,), (1,)), ((), ())), preferred_element_type=F32)
            y_off = jnp.zeros((CHUNK, 256), F32)
            if s_f is not None:
                y_off = y_off + _dot(c_g, s_f[:, g * 256:(g + 1) * 256].astype(BF16)) \
                    * ecs[:, g * 256:(g + 1) * 256]
            if s_b is not None:
                y_off = y_off + _dot(c_g, s_b[:, g * 256:(g + 1) * 256].astype(BF16)) \
                    * ecs[:, SSD_WIDTH + g * 256:SSD_WIDTH + (g + 1) * 256]
            for pair in range(2):
                p0 = g * 256 + pair * LANES
                xf_pair = xdt_f[:, p0:p0 + LANES]
                xb_pair = xdt_b[:, p0:p0 + LANES]
                scores, rhs = [], []
                for sub in range(2):
                    head = g * 4 + pair * 2 + sub
                    keep = low_half if sub == 0 else jnp.logical_not(low_half)
                    rhs.append(jnp.where(keep, xf_pair, 0.0).astype(BF16))
                    rhs.append(jnp.where(keep, xb_pair, 0.0).astype(BF16))
                    jf, jb = head, SSD_HEADS + head
                    d_f = e[:, jf:jf + 1] - et[jf:jf + 1, :]
                    d_b = e[:, jb:jb + 1] - et[jb:jb + 1, :]
                    scores.append((gmat * jnp.where(causal, jnp.exp(d_f), 0.0)).astype(BF16))
                    scores.append((gmat * jnp.where(anti, jnp.exp(d_b), 0.0)).astype(BF16))
                y_diag = _dot(jnp.concatenate(scores, axis=1), jnp.concatenate(rhs, axis=0))
                y_parts.append(y_off[:, pair * LANES:(pair + 1) * LANES] + y_diag)
        y = jnp.concatenate(y_parts, axis=1) + xs * dskip_ref[...]
        yz = y * _silu(z_ref[t["rows"], :])
        ms = jnp.mean(yz * yz, axis=-1, keepdims=True)
        yn = (yz * lax.rsqrt(ms + EPS)) * gssd_ref[...]
        ymix_ref[t["rows"], CONV_WIDTH:] = yn.astype(BF16)

    def bwd_pair(k, carry):
        q = n_pairs - 1 - k
        c0, c1 = 2 * q, 2 * q + 1
        t1 = chunk_tables(c1)
        dec1 = chunk_decay_row(t1["e"])[1][:, SSD_WIDTH:]
        if pair_is_sequence:
            sb1 = None
            sb0 = csb_ref[c1]
            sf0 = None
        else:
            sb1 = sb_ref[...]
            sb0 = sb1 * dec1 + csb_ref[c1]
            sf0 = sfin_ref[c0]
        chunk_output(t1, sfin_ref[c1], sb1)
        t0 = chunk_tables(c0)
        dec0 = chunk_decay_row(t0["e"])[1][:, SSD_WIDTH:]
        chunk_output(t0, sf0, sb0)
        s_new = sb0 * dec0 + csb_ref[c0]
        if pair_is_sequence:
            sbo_ref[q] = jnp.transpose(s_new)
        else:
            sb_ref[...] = s_new
        return carry

    lax.fori_loop(0, n_pairs, bwd_pair, 0)

    def out_body(i, carry):
        r0 = pl.multiple_of(i * ROW_BLOCK, ROW_BLOCK)
        rows = pl.ds(r0, ROW_BLOCK)
        o = _dot(ymix_ref[rows, :], wout_ref[...])
        out_ref[rows, :] = x_ref[rows, :] + gate_m * o
        return carry

    lax.fori_loop(0, SLAB // ROW_BLOCK, out_body, 0)


def _const_spec(shape):
    nd = len(shape)
    return pl.BlockSpec(shape, lambda i: (0,) * nd)


def _mixer_call(x2d, mod, mod_per_slab, seg_len, init_f, init_b, consts):
    n_slabs = x2d.shape[0] // SLAB
    seg_chunks = seg_len // CHUNK
    segs_per_slab = SLAB // seg_len
    has_init = init_f is not None
    state_rows = SSD_HEADS * SSD_HEAD_DIM
    staged_rows = SLAB + PAD * (1 + segs_per_slab)

    mod_map = (lambda i: (i, 0, 0)) if mod_per_slab else (lambda i: (0, 0, 0))
    in_specs = [
        pl.BlockSpec((SLAB, D_MODEL), lambda i: (i, 0)),
        pl.BlockSpec((1, N_MOD, D_MODEL), mod_map),
    ]
    args = [x2d, mod]
    if has_init:
        in_specs += [pl.BlockSpec((1, state_rows, D_STATE), lambda i: (i, 0, 0))] * 2
        args += [init_f, init_b]
    in_specs += [_const_spec(a.shape) for a in consts]
    args += list(consts)

    out_specs = [pl.BlockSpec((SLAB, D_MODEL), lambda i: (i, 0))]
    out_shape = [jax.ShapeDtypeStruct(x2d.shape, F32)]
    if not has_init:
        n_seq = n_slabs * segs_per_slab
        out_specs += [pl.BlockSpec((segs_per_slab, state_rows, D_STATE), lambda i: (i, 0, 0))] * 2
        out_shape += [jax.ShapeDtypeStruct((n_seq, state_rows, D_STATE), F32)] * 2

    scratch = [
        pltpu.VMEM((staged_rows, CONV_WIDTH), F32),
        pltpu.VMEM((staged_rows, 2 * SSD_WIDTH), F32),
        pltpu.VMEM((SLAB, CONV_WIDTH), F32),
        pltpu.VMEM((SLAB, SSD_WIDTH), F32),
        pltpu.VMEM((SLAB, LANES), F32),
        pltpu.VMEM((SLAB, LANES), F32),
        pltpu.VMEM((SLAB, SSD_WIDTH), F32),
        pltpu.VMEM((SLAB, 2 * SSD_GROUPS * D_STATE), BF16),
        pltpu.VMEM((SLAB, D_MODEL), BF16),
        pltpu.VMEM((N_CHUNKS, D_STATE, SSD_WIDTH), F32),
        pltpu.VMEM((N_CHUNKS, D_STATE, SSD_WIDTH), F32),
        pltpu.VMEM((D_STATE, SSD_WIDTH), F32),
        pltpu.VMEM((D_STATE, SSD_WIDTH), F32),
    ]
    return pl.pallas_call(
        functools.partial(_mixer_kernel, seg_chunks, has_init),
        grid=(n_slabs,),
        in_specs=in_specs,
        out_specs=out_specs,
        out_shape=out_shape,
        scratch_shapes=scratch,
        compiler_params=pltpu.CompilerParams(
            dimension_semantics=("arbitrary",), vmem_limit_bytes=VMEM_LIMIT),
        name="mixer_latent" if has_init else "mixer_prompt",
    )(*args)


def _ffn_conv(u, wc_ref, bc_ref, is_grid):
    n = u.shape[0]
    row = lax.broadcasted_iota(jnp.int32, (n, 1), 0)
    period = GRID_W if is_grid else 256
    pos = jnp.bitwise_and(row, period - 1)
    um = jnp.where(pos == 0, 0.0, pltpu.roll(u, 1, 0))
    up = jnp.where(pos == period - 1, 0.0, pltpu.roll(u, n - 1, 0))

    def taps(di):
        return um * wc_ref[3 * di:3 * di + 1, :] + u * wc_ref[3 * di + 1:3 * di + 2, :] \
            + up * wc_ref[3 * di + 2:3 * di + 3, :]

    out = taps(1) + bc_ref[...]
    if is_grid:
        zeros = jnp.zeros((GRID_W, u.shape[1]), F32)
        out = out + jnp.concatenate([zeros, taps(0)[:n - GRID_W]], axis=0)
        out = out + jnp.concatenate([taps(2)[GRID_W:], zeros], axis=0)
    return out


def _ffn_kernel(is_grid, x_ref, mod_ref, g2_ref, wup_ref, wc_ref, bc_ref, wdn_ref, gfin_ref, out_ref,
                h2_ref, acc_ref):
    j = pl.program_id(1)

    @pl.when(j == 0)
    def _():
        shift_f = mod_ref[0, 3:4, :]
        scale_f = mod_ref[0, 4:5, :]
        x = x_ref[...]
        ms = jnp.mean(x * x, axis=-1, keepdims=True)
        h2 = (x * lax.rsqrt(ms + EPS)) * g2_ref[...] * (1.0 + scale_f) + shift_f
        h2_ref[...] = h2.astype(BF16)
        acc_ref[...] = jnp.zeros_like(acc_ref)

    def up_proj(b):
        return _dot(h2_ref[...], wup_ref[b])

    u_next = up_proj(0)
    for b in range(FF_BLOCKS_PER_STEP):
        u = u_next
        if b + 1 < FF_BLOCKS_PER_STEP:
            u_next = up_proj(b + 1)
        u = _ffn_conv(u, wc_ref.at[b], bc_ref.at[b], is_grid)
        a = (_silu(u[:, 0:FF_BLOCK]) * u[:, FF_BLOCK:]).astype(BF16)
        acc_ref[...] += _dot(a, wdn_ref[b])

    @pl.when(j == N_FF_STEPS - 1)
    def _():
        gate_f = mod_ref[0, 5:6, :]
        x2 = x_ref[...] + gate_f * acc_ref[...]
        ms = jnp.mean(x2 * x2, axis=-1, keepdims=True)
        out_ref[...] = (x2 * lax.rsqrt(ms + EPS)) * gfin_ref[...]


def _ffn_call(x2d, mod, mod_per_slab, is_grid, consts):
    n_slabs = x2d.shape[0] // SLAB
    g2, wup3, wc3, bc3, wdn3, gfin = consts
    mod_map = (lambda i, j: (i, 0, 0)) if mod_per_slab else (lambda i, j: (0, 0, 0))
    return pl.pallas_call(
        functools.partial(_ffn_kernel, is_grid),
        grid=(n_slabs, N_FF_STEPS),
        in_specs=[
            pl.BlockSpec((SLAB, D_MODEL), lambda i, j: (i, 0)),
            pl.BlockSpec((1, N_MOD, D_MODEL), mod_map),
            pl.BlockSpec((1, D_MODEL), lambda i, j: (0, 0)),
            pl.BlockSpec((FF_BLOCKS_PER_STEP, D_MODEL, 2 * FF_BLOCK), lambda i, j: (j, 0, 0)),
            pl.BlockSpec((FF_BLOCKS_PER_STEP, 9, 2 * FF_BLOCK), lambda i, j: (j, 0, 0)),
            pl.BlockSpec((FF_BLOCKS_PER_STEP, 1, 2 * FF_BLOCK), lambda i, j: (j, 0, 0)),
            pl.BlockSpec((FF_BLOCKS_PER_STEP, FF_BLOCK, D_MODEL), lambda i, j: (j, 0, 0)),
            pl.BlockSpec((1, D_MODEL), lambda i, j: (0, 0)),
        ],
        out_specs=pl.BlockSpec((SLAB, D_MODEL), lambda i, j: (i, 0)),
        out_shape=jax.ShapeDtypeStruct(x2d.shape, F32),
        scratch_shapes=[
            pltpu.VMEM((SLAB, D_MODEL), BF16),
            pltpu.VMEM((SLAB, D_MODEL), F32),
        ],
        compiler_params=pltpu.CompilerParams(
            dimension_semantics=("arbitrary", "arbitrary"), vmem_limit_bytes=VMEM_LIMIT),
        name="ffn_latent" if is_grid else "ffn_prompt",
    )(x2d, mod, g2, wup3, wc3, bc3, wdn3, gfin)


def _pad_lanes(v):
    return jnp.pad(v.reshape(1, -1), ((0, 0), (0, LANES - v.size)))


def kernel(x_prompt, x_sample, state_ssd_fwd, state_ssd_bwd, c, c_ctx, g_norm1, g_norm2, w_ada, b_ada, w_in, w_conv_short, w_conv_ssd, b_conv_ssd, dt_bias, a_log, d_skip, g_ssd_norm, w_out, w_up, w_ffn_conv, b_ffn_conv, w_down, g_final):
    depth = w_in.shape[0]
    assert depth == 1, "kernel is written for the single-layer problem"
    bp, seq = x_prompt.shape[0], x_prompt.shape[1]
    bd, dec_seq = x_sample.shape[0], x_sample.shape[1]
    assert seq == 256 and dec_seq == SLAB and (bp * seq) % SLAB == 0

    n_c = bd + 1
    rows = -(-n_c // 8) * 8
    cvec = jnp.concatenate([c, c_ctx[None], jnp.zeros((rows - n_c, D_MODEL), F32)], axis=0)
    mod = _mod_call(cvec, w_ada[0], b_ada[0].reshape(1, -1)).reshape(rows, N_MOD, D_MODEL)
    mod_lat = mod[:bd]
    mod_ctx = mod[bd:bd + 1]

    w_in_main = _cast_bf16(w_in[0], DT_COL)
    w_dt = jnp.pad(w_in[0][:, DT_COL:], ((0, 0), (0, LANES - 2 * SSD_HEADS))).astype(BF16)
    idx = jnp.arange(CHUNK)
    ltri = (idx[:, None] >= idx[None, :]).astype(BF16)
    utri = (idx[:, None] <= idx[None, :]).astype(BF16)
    tri3 = jnp.concatenate([jnp.concatenate([ltri] * 3, axis=1), jnp.concatenate([utri] * 3, axis=1)], axis=0)
    expand = (jnp.arange(LANES)[:, None] == (jnp.arange(2 * SSD_WIDTH)[None, :] // SSD_HEAD_DIM)).astype(BF16)
    mixer_consts = (
        g_norm1[0].reshape(1, -1), w_in_main, w_dt, w_conv_short[0], w_conv_ssd[0], b_conv_ssd[0].reshape(1, -1),
        _pad_lanes(dt_bias[0]), _pad_lanes(a_log[0]),
        jnp.repeat(d_skip[0], SSD_HEAD_DIM).reshape(1, -1), g_ssd_norm[0].reshape(1, -1),
        _cast_bf16(w_out[0], D_MODEL), tri3, expand, jnp.concatenate([expand] * 3, axis=0),
    )
    wup3 = _up_blocks(w_up[0])
    pad_blocks = ((0, N_FF_PADDED - N_FF_BLOCKS), (0, 0), (0, 0))
    wc3 = jnp.pad(w_ffn_conv[0].reshape(9, 2, N_FF_BLOCKS, FF_BLOCK).transpose(2, 0, 1, 3)
                  .reshape(N_FF_BLOCKS, 9, 2 * FF_BLOCK), pad_blocks)
    bc3 = jnp.pad(b_ffn_conv[0].reshape(2, N_FF_BLOCKS, FF_BLOCK).transpose(1, 0, 2)
                  .reshape(N_FF_BLOCKS, 1, 2 * FF_BLOCK), pad_blocks)
    wdn3 = _cast_bf16(w_down[0], D_MODEL, N_FF_PADDED - N_FF_BLOCKS).reshape(N_FF_PADDED, FF_BLOCK, D_MODEL)
    ffn_consts = (g_norm2[0].reshape(1, -1), wup3, wc3, bc3, wdn3, g_final.reshape(1, -1))

    state_rows = SSD_HEADS * SSD_HEAD_DIM
    init_f = state_ssd_fwd[:, 0].reshape(bd, state_rows, D_STATE)
    init_b = state_ssd_bwd[:, 0].reshape(bd, state_rows, D_STATE)

    xp2d = x_prompt.reshape(bp * seq, D_MODEL)
    xs2d = x_sample.reshape(bd * dec_seq, D_MODEL)

    xp1, s_f, s_b = _mixer_call(xp2d, mod_ctx, False, seq, None, None, mixer_consts)
    (xs1,) = _mixer_call(xs2d, mod_lat, True, dec_seq, init_f, init_b, mixer_consts)

    y_prompt = _ffn_call(xp1, mod_ctx, False, False, ffn_consts).reshape(x_prompt.shape)
    y_sample = _ffn_call(xs1, mod_lat, True, True, ffn_consts).reshape(x_sample.shape)

    state_shape = (bp, depth, SSD_HEADS, SSD_HEAD_DIM, D_STATE)
    return (y_prompt, y_sample, s_f.reshape(state_shape), s_b.reshape(state_shape))
```

```python
import functools

import jax
import jax.numpy as jnp
from jax import lax
from jax.experimental import pallas as pl
from jax.experimental.pallas import tpu as pltpu

F32 = jnp.float32
BF16 = jnp.bfloat16

D_MODEL = 1024
GRID_W = 64
CONV_WIDTH = 512
SSD_WIDTH = 512
SSD_HEAD_DIM = 64
SSD_HEADS = 8
SSD_GROUPS = 2
D_STATE = 128
CHUNK = 128
D_FF = 2816
N_MOD = 6
EPS = 1e-6

SLAB = 1024
N_CHUNKS = SLAB // CHUNK
ROW_BLOCK = 256
PAD = 8
FF_BLOCK = 256
N_FF_BLOCKS = D_FF // FF_BLOCK
FF_BLOCKS_PER_STEP = 3
N_FF_STEPS = -(-N_FF_BLOCKS // FF_BLOCKS_PER_STEP)
N_FF_PADDED = N_FF_STEPS * FF_BLOCKS_PER_STEP
LANES = 128
DT_COL = 3072
VMEM_LIMIT = 56 * 1024 * 1024


def _silu(v):
    return v / (1.0 + jnp.exp(-v))


def _softplus(v):
    return jnp.maximum(v, 0.0) + jnp.log1p(jnp.exp(-jnp.abs(v)))


def _split3(v):
    hi = v.astype(BF16)
    r1 = v - hi.astype(F32)
    mid = r1.astype(BF16)
    lo = (r1 - mid.astype(F32)).astype(BF16)
    return hi, mid, lo


def _dot(a, b):
    return jnp.dot(a, b, preferred_element_type=F32)


def _dot3(m3, v):
    return _dot(m3, jnp.concatenate(_split3(v), axis=0))


def _expand3(v, expand3):
    return _dot(jnp.concatenate(_split3(v), axis=1), expand3)


def _conv3_rows(win, w_ref):
    n = win.shape[0]
    prev = pltpu.roll(win, 1, 0)[PAD:PAD + CHUNK]
    cur = win[PAD:PAD + CHUNK]
    nxt = pltpu.roll(win, n - 1, 0)[PAD:PAD + CHUNK]
    return prev * w_ref[0:1, :] + cur * w_ref[1:2, :] + nxt * w_ref[2:3, :]


def _mod_kernel(c_ref, w_ref, b_ref, o_ref):
    s = _silu(c_ref[...]).astype(BF16)
    o_ref[...] = _dot(s, w_ref[...].astype(BF16)) + b_ref[...]


def _mod_call(cvec, w_ada, b_ada):
    rows = cvec.shape[0]
    return pl.pallas_call(
        _mod_kernel,
        grid=(N_MOD,),
        in_specs=[
            pl.BlockSpec((rows, D_MODEL), lambda j: (0, 0)),
            pl.BlockSpec((D_MODEL, D_MODEL), lambda j: (0, j)),
            pl.BlockSpec((1, D_MODEL), lambda j: (0, j)),
        ],
        out_specs=pl.BlockSpec((rows, D_MODEL), lambda j: (0, j)),
        out_shape=jax.ShapeDtypeStruct((rows, N_MOD * D_MODEL), F32),
        compiler_params=pltpu.CompilerParams(dimension_semantics=("arbitrary",)),
        name="mod_vectors",
    )(cvec, w_ada, b_ada)


def _cast_kernel(n_src_blocks, src_ref, dst_ref):
    i = pl.program_id(0)

    @pl.when(i < n_src_blocks)
    def _():
        dst_ref[...] = src_ref[...].astype(BF16)

    @pl.when(i >= n_src_blocks)
    def _():
        dst_ref[...] = jnp.zeros_like(dst_ref)


def _cast_bf16(w, cols, pad_row_blocks=0):
    n_src = w.shape[0] // ROW_BLOCK
    n_dst = n_src + pad_row_blocks
    return pl.pallas_call(
        functools.partial(_cast_kernel, n_src),
        grid=(n_dst,),
        in_specs=[pl.BlockSpec((ROW_BLOCK, cols), lambda i: (jnp.minimum(i, n_src - 1), 0))],
        out_specs=pl.BlockSpec((ROW_BLOCK, cols), lambda i: (i, 0)),
        out_shape=jax.ShapeDtypeStruct((n_dst * ROW_BLOCK, cols), BF16),
        compiler_params=pltpu.CompilerParams(dimension_semantics=("arbitrary",)),
        name="cast_bf16",
    )(w)


def _up_blocks_kernel(gate_ref, value_ref, dst_ref):
    j = pl.program_id(0)

    @pl.when(j < N_FF_BLOCKS)
    def _():
        dst_ref[0, :, 0:FF_BLOCK] = gate_ref[...].astype(BF16)
        dst_ref[0, :, FF_BLOCK:] = value_ref[...].astype(BF16)

    @pl.when(j >= N_FF_BLOCKS)
    def _():
        dst_ref[...] = jnp.zeros_like(dst_ref)


def _up_blocks(w_up):
    last = N_FF_BLOCKS - 1
    return pl.pallas_call(
        _up_blocks_kernel,
        grid=(N_FF_PADDED,),
        in_specs=[
            pl.BlockSpec((D_MODEL, FF_BLOCK), lambda j: (0, jnp.minimum(j, last))),
            pl.BlockSpec((D_MODEL, FF_BLOCK), lambda j: (0, N_FF_BLOCKS + jnp.minimum(j, last))),
        ],
        out_specs=pl.BlockSpec((1, D_MODEL, 2 * FF_BLOCK), lambda j: (j, 0, 0)),
        out_shape=jax.ShapeDtypeStruct((N_FF_PADDED, D_MODEL, 2 * FF_BLOCK), BF16),
        compiler_params=pltpu.CompilerParams(dimension_semantics=("arbitrary",)),
        name="up_blocks",
    )(w_up, w_up)


def _mixer_kernel(seg_chunks, has_init, *refs):
    refs = list(refs)
    x_ref, mod_ref = refs[:2]
    refs = refs[2:]
    if has_init:
        initf_ref, initb_ref = refs[:2]
        refs = refs[2:]
    (g1_ref, win_ref, wdt_ref, wcs_ref, wcx_ref, bcx_ref, dtb_ref, alog_ref, dskip_ref, gssd_ref, wout_ref,
     tri3_ref, exp_ref, exp3_ref) = refs[:14]
    refs = refs[14:]
    out_ref = refs[0]
    refs = refs[1:]
    if not has_init:
        sfo_ref, sbo_ref = refs[:2]
        refs = refs[2:]
    (gchc_ref, xbc_ref, gb_ref, z_ref, dt_ref, e_ref, xs_ref, bc_ref, ymix_ref, sfin_ref, csb_ref,
     sf_ref, sb_ref) = refs

    shift_m = mod_ref[0, 0:1, :]
    scale_m = mod_ref[0, 1:2, :]
    gate_m = mod_ref[0, 2:3, :]

    seg_len = seg_chunks * CHUNK
    zpad_a = jnp.zeros((PAD, CONV_WIDTH), F32)
    zpad_x = jnp.zeros((PAD, 2 * SSD_WIDTH), F32)
    for s in range(SLAB // seg_len + 1):
        gap = slice(s * (seg_len + PAD), s * (seg_len + PAD) + PAD)
        gchc_ref[gap, :] = zpad_a
        xbc_ref[gap, :] = zpad_x

    def staged_row(r):
        return pl.multiple_of(r + lax.div(r, seg_len) * PAD, PAD)

    def proj_body(i, carry):
        r0 = pl.multiple_of(i * ROW_BLOCK, ROW_BLOCK)
        rp = pl.multiple_of(staged_row(i * ROW_BLOCK) + PAD, PAD)
        xb = x_ref[pl.ds(r0, ROW_BLOCK), :]
        ms = jnp.mean(xb * xb, axis=-1, keepdims=True)
        h = (xb * lax.rsqrt(ms + EPS)) * g1_ref[...] * (1.0 + scale_m) + shift_m
        hb = h.astype(BF16)

        def proj(lo, hi):
            return _dot(hb, win_ref[:, lo:hi])

        gchc_ref[pl.ds(rp, ROW_BLOCK), :] = proj(0, 512) * proj(1024, 1536)
        gb_ref[pl.ds(r0, ROW_BLOCK), :] = proj(512, 1024)
        z_ref[pl.ds(r0, ROW_BLOCK), :] = proj(1536, 2048)
        xbc_ref[pl.ds(rp, ROW_BLOCK), :] = proj(2048, 3072)
        dt_ref[pl.ds(r0, ROW_BLOCK), :] = _softplus(_dot(hb, wdt_ref[...]) + dtb_ref[...])
        return carry

    lax.fori_loop(0, SLAB // ROW_BLOCK, proj_body, 0)

    a_row = -jnp.exp(alog_ref[...])
    lane_t = lax.broadcasted_iota(jnp.int32, (CHUNK, LANES), 1)
    lane_1 = lax.broadcasted_iota(jnp.int32, (1, LANES), 1)
    is_fwd_t = lane_t < SSD_HEADS
    is_fwd_1 = lane_1 < SSD_HEADS

    def chunk_decay_row(e):
        tot = jnp.where(is_fwd_1, e[CHUNK - 1:CHUNK, :], e[0:1, :])
        dec = jnp.broadcast_to(jnp.exp(tot), (8, LANES))
        return tot, _expand3(dec, exp3_ref[...])[0:1, :]

    pair_is_sequence = seg_chunks == 2
    assert pair_is_sequence != has_init and seg_chunks in (2, N_CHUNKS)
    n_pairs = N_CHUNKS // 2

    def fwd_convs(c):
        r0 = pl.multiple_of(c * CHUNK, CHUNK)
        rows = pl.ds(r0, CHUNK)
        w0 = staged_row(c * CHUNK)
        xc = _conv3_rows(xbc_ref[pl.ds(w0, CHUNK + 2 * PAD), :], wcx_ref)
        xc = _silu(xc + bcx_ref[...])
        xs_ref[rows, :] = xc[:, 0:SSD_WIDTH]
        bc_ref[rows, :] = xc[:, SSD_WIDTH:].astype(BF16)
        ca = _conv3_rows(gchc_ref[pl.ds(w0, CHUNK + 2 * PAD), :], wcs_ref)
        ymix_ref[rows, 0:CONV_WIDTH] = (gb_ref[rows, :] * ca).astype(BF16)
        return xc

    def fwd_decays(c):
        rows = pl.ds(pl.multiple_of(c * CHUNK, CHUNK), CHUNK)
        dtt = dt_ref[rows, :]
        da = dtt * a_row
        sums = _dot3(tri3_ref[...], da)
        e = jnp.where(is_fwd_t, sums[0:CHUNK], sums[CHUNK:])
        e_ref[rows, :] = e
        tot, dec_row = chunk_decay_row(e)
        wst = (dtt * jnp.exp(tot - e)).astype(BF16)
        return _dot(wst, exp_ref[...]), dec_row

    def fwd_chunk_states(xc, wexp):
        xs = xc[:, 0:SSD_WIDTH]
        xst_f = (xs * wexp[:, 0:SSD_WIDTH]).astype(BF16)
        xst_b = (xs * wexp[:, SSD_WIDTH:]).astype(BF16)
        cst_f, cst_b = [], []
        for g in range(SSD_GROUPS):
            bt = jnp.transpose(xc[:, SSD_WIDTH + g * D_STATE:SSD_WIDTH + (g + 1) * D_STATE]).astype(BF16)
            cst_f.append(_dot(bt, xst_f[:, g * 256:(g + 1) * 256]))
            cst_b.append(_dot(bt, xst_b[:, g * 256:(g + 1) * 256]))
        return jnp.concatenate(cst_f, axis=1), jnp.concatenate(cst_b, axis=1)

    if has_init:
        sf_ref[...] = jnp.transpose(initf_ref[0])
        sb_ref[...] = jnp.transpose(initb_ref[0])

    def fwd_pair(p, carry):
        cs_pair = (2 * p, 2 * p + 1)
        xcs = [fwd_convs(c) for c in cs_pair]
        decays = [fwd_decays(c) for c in cs_pair]
        states = [fwd_chunk_states(xc, wexp) for xc, (wexp, _) in zip(xcs, decays)]
        for c, (_, cst_b) in zip(cs_pair, states):
            csb_ref[c] = cst_b
        (cst0, _), (cst1, _) = states
        dec0, dec1 = decays[0][1][:, 0:SSD_WIDTH], decays[1][1][:, 0:SSD_WIDTH]
        if pair_is_sequence:
            s1 = cst0
        else:
            s0 = sf_ref[...]
            sfin_ref[cs_pair[0]] = s0
            s1 = s0 * dec0 + cst0
        sfin_ref[cs_pair[1]] = s1
        s2 = s1 * dec1 + cst1
        if pair_is_sequence:
            sfo_ref[p] = jnp.transpose(s2)
        else:
            sf_ref[...] = s2
        return carry

    lax.fori_loop(0, n_pairs, fwd_pair, 0)

    row_t = lax.broadcasted_iota(jnp.int32, (CHUNK, CHUNK), 0)
    col_t = lax.broadcasted_iota(jnp.int32, (CHUNK, CHUNK), 1)
    causal = row_t >= col_t
    anti = col_t >= row_t
    low_half = lane_t < SSD_HEAD_DIM

    def chunk_tables(c):
        rows = pl.ds(pl.multiple_of(c * CHUNK, CHUNK), CHUNK)
        e = e_ref[rows, :]
        xs = xs_ref[rows, :]
        dtexp = _dot(dt_ref[rows, :].astype(BF16), exp_ref[...])
        ecs = _dot(jnp.exp(e).astype(BF16), exp_ref[...])
        return dict(rows=rows, e=e, et=jnp.transpose(e), xs=xs, bc=bc_ref[rows, :], ecs=ecs,
                    xdt_f=xs * dtexp[:, 0:SSD_WIDTH], xdt_b=xs * dtexp[:, SSD_WIDTH:])

    def chunk_output(t, s_f, s_b):
        e, et, xs, bc, ecs, xdt_f, xdt_b = t["e"], t["et"], t["xs"], t["bc"], t["ecs"], t["xdt_f"], t["xdt_b"]
        y_parts = []
        for g in range(SSD_GROUPS):
            b_g = bc[:, g * D_STATE:(g + 1) * D_STATE]
            c_g = bc[:, 256 + g * D_STATE:256 + (g + 1) * D_STATE]
            gmat = lax.dot_general(c_g, b_g, (((1,), (1,)), ((), ())), preferred_element_type=F32)
            y_off = jnp.zeros((CHUNK, 256), F32)
            if s_f is not None:
                y_off = y_off + _dot(c_g, s_f[:, g * 256:(g + 1) * 256].astype(BF16)) \
                    * ecs[:, g * 256:(g + 1) * 256]
            if s_b is not None:
                y_off = y_off + _dot(c_g, s_b[:, g * 256:(g + 1) * 256].astype(BF16)) \
                    * ecs[:, SSD_WIDTH + g * 256:SSD_WIDTH + (g + 1) * 256]
            for pair in range(2):
                p0 = g * 256 + pair * LANES
                xf_pair = xdt_f[:, p0:p0 + LANES]
                xb_pair = xdt_b[:, p0:p0 + LANES]
                scores, rhs = [], []
                for sub in range(2):
                    head = g * 4 + pair * 2 + sub
                    keep = low_half if sub == 0 else jnp.logical_not(low_half)
                    rhs.append(jnp.where(keep, xf_pair, 0.0).astype(BF16))
                    rhs.append(jnp.where(keep, xb_pair, 0.0).astype(BF16))
                    jf, jb = head, SSD_HEADS + head
                    d_f = e[:, jf:jf + 1] - et[jf:jf + 1, :]
                    d_b = e[:, jb:jb + 1] - et[jb:jb + 1, :]
                    scores.append((gmat * jnp.where(causal, jnp.exp(d_f), 0.0)).astype(BF16))
                    scores.append((gmat * jnp.where(anti, jnp.exp(d_b), 0.0)).astype(BF16))
                y_diag = _dot(jnp.concatenate(scores, axis=1), jnp.concatenate(rhs, axis=0))
                y_parts.append(y_off[:, pair * LANES:(pair + 1) * LANES] + y_diag)
        y = jnp.concatenate(y_parts, axis=1) + xs * dskip_ref[...]
        yz = y * _silu(z_ref[t["rows"], :])
        ms = jnp.mean(yz * yz, axis=-1, keepdims=True)
        yn = (yz * lax.rsqrt(ms + EPS)) * gssd_ref[...]
        ymix_ref[t["rows"], CONV_WIDTH:] = yn.astype(BF16)

    def bwd_pair(k, carry):
        q = n_pairs - 1 - k
        c0, c1 = 2 * q, 2 * q + 1
        t1 = chunk_tables(c1)
        dec1 = chunk_decay_row(t1["e"])[1][:, SSD_WIDTH:]
        if pair_is_sequence:
            sb1 = None
            sb0 = csb_ref[c1]
            sf0 = None
        else:
            sb1 = sb_ref[...]
            sb0 = sb1 * dec1 + csb_ref[c1]
            sf0 = sfin_ref[c0]
        chunk_output(t1, sfin_ref[c1], sb1)
        t0 = chunk_tables(c0)
        dec0 = chunk_decay_row(t0["e"])[1][:, SSD_WIDTH:]
        chunk_output(t0, sf0, sb0)
        s_new = sb0 * dec0 + csb_ref[c0]
        if pair_is_sequence:
            sbo_ref[q] = jnp.transpose(s_new)
        else:
            sb_ref[...] = s_new
        return carry

    lax.fori_loop(0, n_pairs, bwd_pair, 0)

    def out_body(i, carry):
        r0 = pl.multiple_of(i * ROW_BLOCK, ROW_BLOCK)
        rows = pl.ds(r0, ROW_BLOCK)
        o = _dot(ymix_ref[rows, :], wout_ref[...])
        out_ref[rows, :] = x_ref[rows, :] + gate_m * o
        return carry

    lax.fori_loop(0, SLAB // ROW_BLOCK, out_body, 0)


def _const_spec(shape):
    nd = len(shape)
    return pl.BlockSpec(shape, lambda i: (0,) * nd)


def _mixer_call(x2d, mod, mod_per_slab, seg_len, init_f, init_b, consts):
    n_slabs = x2d.shape[0] // SLAB
    seg_chunks = seg_len // CHUNK
    segs_per_slab = SLAB // seg_len
    has_init = init_f is not None
    state_rows = SSD_HEADS * SSD_HEAD_DIM
    staged_rows = SLAB + PAD * (1 + segs_per_slab)

    mod_map = (lambda i: (i, 0, 0)) if mod_per_slab else (lambda i: (0, 0, 0))
    in_specs = [
        pl.BlockSpec((SLAB, D_MODEL), lambda i: (i, 0)),
        pl.BlockSpec((1, N_MOD, D_MODEL), mod_map),
    ]
    args = [x2d, mod]
    if has_init:
        in_specs += [pl.BlockSpec((1, state_rows, D_STATE), lambda i: (i, 0, 0))] * 2
        args += [init_f, init_b]
    in_specs += [_const_spec(a.shape) for a in consts]
    args += list(consts)

    out_specs = [pl.BlockSpec((SLAB, D_MODEL), lambda i: (i, 0))]
    out_shape = [jax.ShapeDtypeStruct(x2d.shape, F32)]
    if not has_init:
        n_seq = n_slabs * segs_per_slab
        out_specs += [pl.BlockSpec((segs_per_slab, state_rows, D_STATE), lambda i: (i, 0, 0))] * 2
        out_shape += [jax.ShapeDtypeStruct((n_seq, state_rows, D_STATE), F32)] * 2

    scratch = [
        pltpu.VMEM((staged_rows, CONV_WIDTH), F32),
        pltpu.VMEM((staged_rows, 2 * SSD_WIDTH), F32),
        pltpu.VMEM((SLAB, CONV_WIDTH), F32),
        pltpu.VMEM((SLAB, SSD_WIDTH), F32),
        pltpu.VMEM((SLAB, LANES), F32),
        pltpu.VMEM((SLAB, LANES), F32),
        pltpu.VMEM((SLAB, SSD_WIDTH), F32),
        pltpu.VMEM((SLAB, 2 * SSD_GROUPS * D_STATE), BF16),
        pltpu.VMEM((SLAB, D_MODEL), BF16),
        pltpu.VMEM((N_CHUNKS, D_STATE, SSD_WIDTH), F32),
        pltpu.VMEM((N_CHUNKS, D_STATE, SSD_WIDTH), F32),
        pltpu.VMEM((D_STATE, SSD_WIDTH), F32),
        pltpu.VMEM((D_STATE, SSD_WIDTH), F32),
    ]
    return pl.pallas_call(
        functools.partial(_mixer_kernel, seg_chunks, has_init),
        grid=(n_slabs,),
        in_specs=in_specs,
        out_specs=out_specs,
        out_shape=out_shape,
        scratch_shapes=scratch,
        compiler_params=pltpu.CompilerParams(
            dimension_semantics=("arbitrary",), vmem_limit_bytes=VMEM_LIMIT),
        name="mixer_latent" if has_init else "mixer_prompt",
    )(*args)


def _ffn_conv(u, wc_ref, bc_ref, is_grid):
    n = u.shape[0]
    row = lax.broadcasted_iota(jnp.int32, (n, 1), 0)
    period = GRID_W if is_grid else 256
    pos = jnp.bitwise_and(row, period - 1)
    um = jnp.where(pos == 0, 0.0, pltpu.roll(u, 1, 0)).astype(BF16)
    up = jnp.where(pos == period - 1, 0.0, pltpu.roll(u, n - 1, 0)).astype(BF16)
    u = u.astype(BF16)
    wc = wc_ref[...].astype(BF16)

    def taps(di):
        return um * wc[3 * di:3 * di + 1, :] + u * wc[3 * di + 1:3 * di + 2, :] + up * wc[3 * di + 2:3 * di + 3, :]

    out = taps(1) + bc_ref[...].astype(BF16)
    if is_grid:
        zeros = jnp.zeros((GRID_W, u.shape[1]), BF16)
        out = out + jnp.concatenate([zeros, taps(0)[:n - GRID_W]], axis=0)
        out = out + jnp.concatenate([taps(2)[GRID_W:], zeros], axis=0)
    return out.astype(F32)


def _ffn_kernel(is_grid, x_ref, mod_ref, g2_ref, wup_ref, wc_ref, bc_ref, wdn_ref, gfin_ref, out_ref,
                h2_ref, acc_ref):
    j = pl.program_id(1)

    @pl.when(j == 0)
    def _():
        shift_f = mod_ref[0, 3:4, :]
        scale_f = mod_ref[0, 4:5, :]
        x = x_ref[...]
        ms = jnp.mean(x * x, axis=-1, keepdims=True)
        h2 = (x * lax.rsqrt(ms + EPS)) * g2_ref[...] * (1.0 + scale_f) + shift_f
        h2_ref[...] = h2.astype(BF16)
        acc_ref[...] = jnp.zeros_like(acc_ref)

    def up_proj(b):
        return _dot(h2_ref[...], wup_ref[b])

    u_next = up_proj(0)
    for b in range(FF_BLOCKS_PER_STEP):
        u = u_next
        if b + 1 < FF_BLOCKS_PER_STEP:
            u_next = up_proj(b + 1)
        u = _ffn_conv(u, wc_ref.at[b], bc_ref.at[b], is_grid)
        a = (_silu(u[:, 0:FF_BLOCK]) * u[:, FF_BLOCK:]).astype(BF16)
        acc_ref[...] += _dot(a, wdn_ref[b])

    @pl.when(j == N_FF_STEPS - 1)
    def _():
        gate_f = mod_ref[0, 5:6, :]
        x2 = x_ref[...] + gate_f * acc_ref[...]
        ms = jnp.mean(x2 * x2, axis=-1, keepdims=True)
        out_ref[...] = (x2 * lax.rsqrt(ms + EPS)) * gfin_ref[...]


def _ffn_call(x2d, mod, mod_per_slab, is_grid, consts):
    n_slabs = x2d.shape[0] // SLAB
    g2, wup3, wc3, bc3, wdn3, gfin = consts
    mod_map = (lambda i, j: (i, 0, 0)) if mod_per_slab else (lambda i, j: (0, 0, 0))
    return pl.pallas_call(
        functools.partial(_ffn_kernel, is_grid),
        grid=(n_slabs, N_FF_STEPS),
        in_specs=[
            pl.BlockSpec((SLAB, D_MODEL), lambda i, j: (i, 0)),
            pl.BlockSpec((1, N_MOD, D_MODEL), mod_map),
            pl.BlockSpec((1, D_MODEL), lambda i, j: (0, 0)),
            pl.BlockSpec((FF_BLOCKS_PER_STEP, D_MODEL, 2 * FF_BLOCK), lambda i, j: (j, 0, 0)),
            pl.BlockSpec((FF_BLOCKS_PER_STEP, 9, 2 * FF_BLOCK), lambda i, j: (j, 0, 0)),
            pl.BlockSpec((FF_BLOCKS_PER_STEP, 1, 2 * FF_BLOCK), lambda i, j: (j, 0, 0)),
            pl.BlockSpec((FF_BLOCKS_PER_STEP, FF_BLOCK, D_MODEL), lambda i, j: (j, 0, 0)),
            pl.BlockSpec((1, D_MODEL), lambda i, j: (0, 0)),
        ],
        out_specs=pl.BlockSpec((SLAB, D_MODEL), lambda i, j: (i, 0)),
        out_shape=jax.ShapeDtypeStruct(x2d.shape, F32),
        scratch_shapes=[
            pltpu.VMEM((SLAB, D_MODEL), BF16),
            pltpu.VMEM((SLAB, D_MODEL), F32),
        ],
        compiler_params=pltpu.CompilerParams(
            dimension_semantics=("arbitrary", "arbitrary"), vmem_limit_bytes=VMEM_LIMIT),
        name="ffn_latent" if is_grid else "ffn_prompt",
    )(x2d, mod, g2, wup3, wc3, bc3, wdn3, gfin)


def _pad_lanes(v):
    return jnp.pad(v.reshape(1, -1), ((0, 0), (0, LANES - v.size)))


def kernel(x_prompt, x_sample, state_ssd_fwd, state_ssd_bwd, c, c_ctx, g_norm1, g_norm2, w_ada, b_ada, w_in, w_conv_short, w_conv_ssd, b_conv_ssd, dt_bias, a_log, d_skip, g_ssd_norm, w_out, w_up, w_ffn_conv, b_ffn_conv, w_down, g_final):
    depth = w_in.shape[0]
    assert depth == 1, "kernel is written for the single-layer problem"
    bp, seq = x_prompt.shape[0], x_prompt.shape[1]
    bd, dec_seq = x_sample.shape[0], x_sample.shape[1]
    assert seq == 256 and dec_seq == SLAB and (bp * seq) % SLAB == 0

    n_c = bd + 1
    rows = -(-n_c // 8) * 8
    cvec = jnp.concatenate([c, c_ctx[None], jnp.zeros((rows - n_c, D_MODEL), F32)], axis=0)
    mod = _mod_call(cvec, w_ada[0], b_ada[0].reshape(1, -1)).reshape(rows, N_MOD, D_MODEL)
    mod_lat = mod[:bd]
    mod_ctx = mod[bd:bd + 1]

    w_in_main = _cast_bf16(w_in[0], DT_COL)
    w_dt = jnp.pad(w_in[0][:, DT_COL:], ((0, 0), (0, LANES - 2 * SSD_HEADS))).astype(BF16)
    idx = jnp.arange(CHUNK)
    ltri = (idx[:, None] >= idx[None, :]).astype(BF16)
    utri = (idx[:, None] <= idx[None, :]).astype(BF16)
    tri3 = jnp.concatenate([jnp.concatenate([ltri] * 3, axis=1), jnp.concatenate([utri] * 3, axis=1)], axis=0)
    expand = (jnp.arange(LANES)[:, None] == (jnp.arange(2 * SSD_WIDTH)[None, :] // SSD_HEAD_DIM)).astype(BF16)
    mixer_consts = (
        g_norm1[0].reshape(1, -1), w_in_main, w_dt, w_conv_short[0], w_conv_ssd[0], b_conv_ssd[0].reshape(1, -1),
        _pad_lanes(dt_bias[0]), _pad_lanes(a_log[0]),
        jnp.repeat(d_skip[0], SSD_HEAD_DIM).reshape(1, -1), g_ssd_norm[0].reshape(1, -1),
        _cast_bf16(w_out[0], D_MODEL), tri3, expand, jnp.concatenate([expand] * 3, axis=0),
    )
    wup3 = _up_blocks(w_up[0])
    pad_blocks = ((0, N_FF_PADDED - N_FF_BLOCKS), (0, 0), (0, 0))
    wc3 = jnp.pad(w_ffn_conv[0].reshape(9, 2, N_FF_BLOCKS, FF_BLOCK).transpose(2, 0, 1, 3)
                  .reshape(N_FF_BLOCKS, 9, 2 * FF_BLOCK), pad_blocks)
    bc3 = jnp.pad(b_ffn_conv[0].reshape(2, N_FF_BLOCKS, FF_BLOCK).transpose(1, 0, 2)
                  .reshape(N_FF_BLOCKS, 1, 2 * FF_BLOCK), pad_blocks)
    wdn3 = _cast_bf16(w_down[0], D_MODEL, N_FF_PADDED - N_FF_BLOCKS).reshape(N_FF_PADDED, FF_BLOCK, D_MODEL)
    ffn_consts = (g_norm2[0].reshape(1, -1), wup3, wc3, bc3, wdn3, g_final.reshape(1, -1))

    state_rows = SSD_HEADS * SSD_HEAD_DIM
    init_f = state_ssd_fwd[:, 0].reshape(bd, state_rows, D_STATE)
    init_b = state_ssd_bwd[:, 0].reshape(bd, state_rows, D_STATE)

    xp2d = x_prompt.reshape(bp * seq, D_MODEL)
    xs2d = x_sample.reshape(bd * dec_seq, D_MODEL)

    xp1, s_f, s_b = _mixer_call(xp2d, mod_ctx, False, seq, None, None, mixer_consts)
    (xs1,) = _mixer_call(xs2d, mod_lat, True, dec_seq, init_f, init_b, mixer_consts)

    y_prompt = _ffn_call(xp1, mod_ctx, False, False, ffn_consts).reshape(x_prompt.shape)
    y_sample = _ffn_call(xs1, mod_lat, True, True, ffn_consts).reshape(x_sample.shape)

    state_shape = (bp, depth, SSD_HEADS, SSD_HEAD_DIM, D_STATE)
    return (y_prompt, y_sample, s_f.reshape(state_shape), s_b.reshape(state_shape))
```

```python
import functools

import jax
import jax.numpy as jnp
from jax import lax
from jax.experimental import pallas as pl
from jax.experimental.pallas import tpu as pltpu

F32 = jnp.float32
BF16 = jnp.bfloat16

D_MODEL = 1024
GRID_W = 64
CONV_WIDTH = 512
SSD_WIDTH = 512
SSD_HEAD_DIM = 64
SSD_HEADS = 8
SSD_GROUPS = 2
D_STATE = 128
CHUNK = 128
D_FF = 2816
N_MOD = 6
EPS = 1e-6

SLAB = 1024
N_CHUNKS = SLAB // CHUNK
ROW_BLOCK = 256
PAD = 8
FF_BLOCK = 256
N_FF_BLOCKS = D_FF // FF_BLOCK
FF_BLOCKS_PER_STEP = 3
N_FF_STEPS = -(-N_FF_BLOCKS // FF_BLOCKS_PER_STEP)
N_FF_PADDED = N_FF_STEPS * FF_BLOCKS_PER_STEP
LANES = 128
DT_COL = 3072
VMEM_LIMIT = 56 * 1024 * 1024


def _silu(v):
    return v / (1.0 + jnp.exp(-v))


def _softplus(v):
    return jnp.maximum(v, 0.0) + jnp.log1p(jnp.exp(-jnp.abs(v)))


def _split3(v):
    hi = v.astype(BF16)
    r1 = v - hi.astype(F32)
    mid = r1.astype(BF16)
    lo = (r1 - mid.astype(F32)).astype(BF16)
    return hi, mid, lo


def _dot(a, b):
    return jnp.dot(a, b, preferred_element_type=F32)


def _dot3(m3, v):
    return _dot(m3, jnp.concatenate(_split3(v), axis=0))


def _expand3(v, expand3):
    return _dot(jnp.concatenate(_split3(v), axis=1), expand3)


def _conv3_rows(win, w_ref):
    n = win.shape[0]
    prev = pltpu.roll(win, 1, 0)[PAD:PAD + CHUNK]
    cur = win[PAD:PAD + CHUNK]
    nxt = pltpu.roll(win, n - 1, 0)[PAD:PAD + CHUNK]
    return prev * w_ref[0:1, :] + cur * w_ref[1:2, :] + nxt * w_ref[2:3, :]


def _mod_kernel(c_ref, w_ref, b_ref, o_ref):
    s = _silu(c_ref[...]).astype(BF16)
    o_ref[...] = _dot(s, w_ref[...].astype(BF16)) + b_ref[...]


def _mod_call(cvec, w_ada, b_ada):
    rows = cvec.shape[0]
    return pl.pallas_call(
        _mod_kernel,
        grid=(N_MOD,),
        in_specs=[
            pl.BlockSpec((rows, D_MODEL), lambda j: (0, 0)),
            pl.BlockSpec((D_MODEL, D_MODEL), lambda j: (0, j)),
            pl.BlockSpec((1, D_MODEL), lambda j: (0, j)),
        ],
        out_specs=pl.BlockSpec((rows, D_MODEL), lambda j: (0, j)),
        out_shape=jax.ShapeDtypeStruct((rows, N_MOD * D_MODEL), F32),
        compiler_params=pltpu.CompilerParams(dimension_semantics=("arbitrary",)),
        name="mod_vectors",
    )(cvec, w_ada, b_ada)


def _cast_kernel(n_src_blocks, src_ref, dst_ref):
    i = pl.program_id(0)

    @pl.when(i < n_src_blocks)
    def _():
        dst_ref[...] = src_ref[...].astype(BF16)

    @pl.when(i >= n_src_blocks)
    def _():
        dst_ref[...] = jnp.zeros_like(dst_ref)


def _cast_bf16(w, cols, pad_row_blocks=0):
    n_src = w.shape[0] // ROW_BLOCK
    n_dst = n_src + pad_row_blocks
    return pl.pallas_call(
        functools.partial(_cast_kernel, n_src),
        grid=(n_dst,),
        in_specs=[pl.BlockSpec((ROW_BLOCK, cols), lambda i: (jnp.minimum(i, n_src - 1), 0))],
        out_specs=pl.BlockSpec((ROW_BLOCK, cols), lambda i: (i, 0)),
        out_shape=jax.ShapeDtypeStruct((n_dst * ROW_BLOCK, cols), BF16),
        compiler_params=pltpu.CompilerParams(dimension_semantics=("arbitrary",)),
        name="cast_bf16",
    )(w)


def _up_blocks_kernel(gate_ref, value_ref, dst_ref):
    j = pl.program_id(0)

    @pl.when(j < N_FF_BLOCKS)
    def _():
        dst_ref[0, :, 0:FF_BLOCK] = gate_ref[...].astype(BF16)
        dst_ref[0, :, FF_BLOCK:] = value_ref[...].astype(BF16)

    @pl.when(j >= N_FF_BLOCKS)
    def _():
        dst_ref[...] = jnp.zeros_like(dst_ref)


def _up_blocks(w_up):
    last = N_FF_BLOCKS - 1
    return pl.pallas_call(
        _up_blocks_kernel,
        grid=(N_FF_PADDED,),
        in_specs=[
            pl.BlockSpec((D_MODEL, FF_BLOCK), lambda j: (0, jnp.minimum(j, last))),
            pl.BlockSpec((D_MODEL, FF_BLOCK), lambda j: (0, N_FF_BLOCKS + jnp.minimum(j, last))),
        ],
        out_specs=pl.BlockSpec((1, D_MODEL, 2 * FF_BLOCK), lambda j: (j, 0, 0)),
        out_shape=jax.ShapeDtypeStruct((N_FF_PADDED, D_MODEL, 2 * FF_BLOCK), BF16),
        compiler_params=pltpu.CompilerParams(dimension_semantics=("arbitrary",)),
        name="up_blocks",
    )(w_up, w_up)


def _mixer_kernel(seg_chunks, has_init, *refs):
    refs = list(refs)
    x_ref, mod_ref = refs[:2]
    refs = refs[2:]
    if has_init:
        initf_ref, initb_ref = refs[:2]
        refs = refs[2:]
    (g1_ref, win_ref, wdt_ref, wcs_ref, wcx_ref, bcx_ref, dtb_ref, alog_ref, dskip_ref, gssd_ref, wout_ref,
     tri3_ref, exp_ref, exp3_ref) = refs[:14]
    refs = refs[14:]
    out_ref = refs[0]
    refs = refs[1:]
    if not has_init:
        sfo_ref, sbo_ref = refs[:2]
        refs = refs[2:]
    (gchc_ref, xbc_ref, gb_ref, z_ref, dt_ref, e_ref, xs_ref, bc_ref, ymix_ref, sfin_ref, csb_ref,
     sf_ref, sb_ref) = refs

    shift_m = mod_ref[0, 0:1, :]
    scale_m = mod_ref[0, 1:2, :]
    gate_m = mod_ref[0, 2:3, :]

    seg_len = seg_chunks * CHUNK
    zpad_a = jnp.zeros((PAD, CONV_WIDTH), F32)
    zpad_x = jnp.zeros((PAD, 2 * SSD_WIDTH), F32)
    for s in range(SLAB // seg_len + 1):
        gap = slice(s * (seg_len + PAD), s * (seg_len + PAD) + PAD)
        gchc_ref[gap, :] = zpad_a
        xbc_ref[gap, :] = zpad_x

    def staged_row(r):
        return pl.multiple_of(r + lax.div(r, seg_len) * PAD, PAD)

    def proj_body(i, carry):
        r0 = pl.multiple_of(i * ROW_BLOCK, ROW_BLOCK)
        rp = pl.multiple_of(staged_row(i * ROW_BLOCK) + PAD, PAD)
        xb = x_ref[pl.ds(r0, ROW_BLOCK), :]
        ms = jnp.mean(xb * xb, axis=-1, keepdims=True)
        h = (xb * lax.rsqrt(ms + EPS)) * g1_ref[...] * (1.0 + scale_m) + shift_m
        hb = h.astype(BF16)

        def proj(lo, hi):
            return _dot(hb, win_ref[:, lo:hi])

        gchc_ref[pl.ds(rp, ROW_BLOCK), :] = proj(0, 512) * proj(1024, 1536)
        gb_ref[pl.ds(r0, ROW_BLOCK), :] = proj(512, 1024)
        z_ref[pl.ds(r0, ROW_BLOCK), :] = proj(1536, 2048)
        xbc_ref[pl.ds(rp, ROW_BLOCK), :] = proj(2048, 3072)
        dt_ref[pl.ds(r0, ROW_BLOCK), :] = _softplus(_dot(hb, wdt_ref[...]) + dtb_ref[...])
        return carry

    lax.fori_loop(0, SLAB // ROW_BLOCK, proj_body, 0)

    a_row = -jnp.exp(alog_ref[...])
    lane_t = lax.broadcasted_iota(jnp.int32, (CHUNK, LANES), 1)
    lane_1 = lax.broadcasted_iota(jnp.int32, (1, LANES), 1)
    is_fwd_t = lane_t < SSD_HEADS
    is_fwd_1 = lane_1 < SSD_HEADS

    def chunk_decay_row(e):
        tot = jnp.where(is_fwd_1, e[CHUNK - 1:CHUNK, :], e[0:1, :])
        dec = jnp.broadcast_to(jnp.exp(tot), (8, LANES))
        return tot, _expand3(dec, exp3_ref[...])[0:1, :]

    pair_is_sequence = seg_chunks == 2
    assert pair_is_sequence != has_init and seg_chunks in (2, N_CHUNKS)
    n_pairs = N_CHUNKS // 2

    def fwd_convs(c):
        r0 = pl.multiple_of(c * CHUNK, CHUNK)
        rows = pl.ds(r0, CHUNK)
        w0 = staged_row(c * CHUNK)
        xc = _conv3_rows(xbc_ref[pl.ds(w0, CHUNK + 2 * PAD), :], wcx_ref)
        xc = _silu(xc + bcx_ref[...])
        xs_ref[rows, :] = xc[:, 0:SSD_WIDTH]
        bc_ref[rows, :] = xc[:, SSD_WIDTH:].astype(BF16)
        ca = _conv3_rows(gchc_ref[pl.ds(w0, CHUNK + 2 * PAD), :], wcs_ref)
        ymix_ref[rows, 0:CONV_WIDTH] = (gb_ref[rows, :] * ca).astype(BF16)
        return xc

    def fwd_decays(c):
        rows = pl.ds(pl.multiple_of(c * CHUNK, CHUNK), CHUNK)
        dtt = dt_ref[rows, :]
        da = dtt * a_row
        sums = _dot3(tri3_ref[...], da)
        e = jnp.where(is_fwd_t, sums[0:CHUNK], sums[CHUNK:])
        e_ref[rows, :] = e
        tot, dec_row = chunk_decay_row(e)
        wst = (dtt * jnp.exp(tot - e)).astype(BF16)
        return _dot(wst, exp_ref[...]), dec_row

    def fwd_chunk_states(xc, wexp):
        xs = xc[:, 0:SSD_WIDTH]
        xst_f = (xs * wexp[:, 0:SSD_WIDTH]).astype(BF16)
        xst_b = (xs * wexp[:, SSD_WIDTH:]).astype(BF16)
        cst_f, cst_b = [], []
        for g in range(SSD_GROUPS):
            bt = jnp.transpose(xc[:, SSD_WIDTH + g * D_STATE:SSD_WIDTH + (g + 1) * D_STATE]).astype(BF16)
            cst_f.append(_dot(bt, xst_f[:, g * 256:(g + 1) * 256]))
            cst_b.append(_dot(bt, xst_b[:, g * 256:(g + 1) * 256]))
        return jnp.concatenate(cst_f, axis=1), jnp.concatenate(cst_b, axis=1)

    if has_init:
        sf_ref[...] = jnp.transpose(initf_ref[0])
        sb_ref[...] = jnp.transpose(initb_ref[0])

    def fwd_pair(p, carry):
        cs_pair = (2 * p, 2 * p + 1)
        xcs = [fwd_convs(c) for c in cs_pair]
        decays = [fwd_decays(c) for c in cs_pair]
        states = [fwd_chunk_states(xc, wexp) for xc, (wexp, _) in zip(xcs, decays)]
        for c, (_, cst_b) in zip(cs_pair, states):
            csb_ref[c] = cst_b
        (cst0, _), (cst1, _) = states
        dec0, dec1 = decays[0][1][:, 0:SSD_WIDTH], decays[1][1][:, 0:SSD_WIDTH]
        if pair_is_sequence:
            s1 = cst0
        else:
            s0 = sf_ref[...]
            sfin_ref[cs_pair[0]] = s0
            s1 = s0 * dec0 + cst0
        sfin_ref[cs_pair[1]] = s1
        s2 = s1 * dec1 + cst1
        if pair_is_sequence:
            sfo_ref[p] = jnp.transpose(s2)
        else:
            sf_ref[...] = s2
        return carry

    lax.fori_loop(0, n_pairs, fwd_pair, 0)

    row_t = lax.broadcasted_iota(jnp.int32, (CHUNK, CHUNK), 0)
    col_t = lax.broadcasted_iota(jnp.int32, (CHUNK, CHUNK), 1)
    causal = row_t >= col_t
    anti = col_t >= row_t
    low_half = lane_t < SSD_HEAD_DIM

    def chunk_tables(c):
        rows = pl.ds(pl.multiple_of(c * CHUNK, CHUNK), CHUNK)
        e = e_ref[rows, :]
        xs = xs_ref[rows, :]
        dtexp = _dot(dt_ref[rows, :].astype(BF16), exp_ref[...])
        ecs = _dot(jnp.exp(e).astype(BF16), exp_ref[...])
        return dict(rows=rows, e=e, et=jnp.transpose(e), xs=xs, bc=bc_ref[rows, :], ecs=ecs,
                    xdt_f=xs * dtexp[:, 0:SSD_WIDTH], xdt_b=xs * dtexp[:, SSD_WIDTH:])

    def chunk_output(t, s_f, s_b):
        e, et, xs, bc, ecs, xdt_f, xdt_b = t["e"], t["et"], t["xs"], t["bc"], t["ecs"], t["xdt_f"], t["xdt_b"]
        y_parts = []
        for g in range(SSD_GROUPS):
            b_g = bc[:, g * D_STATE:(g + 1) * D_STATE]
            c_g = bc[:, 256 + g * D_STATE:256 + (g + 1) * D_STATE]
            gmat = lax.dot_general(c_g, b_g, (((1,), (1,)), ((), ())), preferred_element_type=F32)
            y_off = jnp.zeros((CHUNK, 256), F32)
            if s_f is not None:
                y_off = y_off + _dot(c_g, s_f[:, g * 256:(g + 1) * 256].astype(BF16)) \
                    * ecs[:, g * 256:(g + 1) * 256]
            if s_b is not None:
                y_off = y_off + _dot(c_g, s_b[:, g * 256:(g + 1) * 256].astype(BF16)) \
                    * ecs[:, SSD_WIDTH + g * 256:SSD_WIDTH + (g + 1) * 256]
            for pair in range(2):
                p0 = g * 256 + pair * LANES
                xf_pair = xdt_f[:, p0:p0 + LANES]
                xb_pair = xdt_b[:, p0:p0 + LANES]
                scores, rhs = [], []
                for sub in range(2):
                    head = g * 4 + pair * 2 + sub
                    keep = low_half if sub == 0 else jnp.logical_not(low_half)
                    rhs.append(jnp.where(keep, xf_pair, 0.0).astype(BF16))
                    rhs.append(jnp.where(keep, xb_pair, 0.0).astype(BF16))
                    jf, jb = head, SSD_HEADS + head
                    d_f = e[:, jf:jf + 1] - et[jf:jf + 1, :]
                    d_b = e[:, jb:jb + 1] - et[jb:jb + 1, :]
                    scores.append((gmat * jnp.where(causal, jnp.exp(d_f), 0.0)).astype(BF16))
                    scores.append((gmat * jnp.where(anti, jnp.exp(d_b), 0.0)).astype(BF16))
                y_diag = _dot(jnp.concatenate(scores, axis=1), jnp.concatenate(rhs, axis=0))
                y_parts.append(y_off[:, pair * LANES:(pair + 1) * LANES] + y_diag)
        y = jnp.concatenate(y_parts, axis=1) + xs * dskip_ref[...]
        yz = y * _silu(z_ref[t["rows"], :])
        ms = jnp.mean(yz * yz, axis=-1, keepdims=True)
        yn = (yz * lax.rsqrt(ms + EPS)) * gssd_ref[...]
        ymix_ref[t["rows"], CONV_WIDTH:] = yn.astype(BF16)

    def bwd_pair(k, carry):
        q = n_pairs - 1 - k
        c0, c1 = 2 * q, 2 * q + 1
        t1 = chunk_tables(c1)
        dec1 = chunk_decay_row(t1["e"])[1][:, SSD_WIDTH:]
        if pair_is_sequence:
            sb1 = None
            sb0 = csb_ref[c1]
            sf0 = None
        else:
            sb1 = sb_ref[...]
            sb0 = sb1 * dec1 + csb_ref[c1]
            sf0 = sfin_ref[c0]
        chunk_output(t1, sfin_ref[c1], sb1)
        t0 = chunk_tables(c0)
        dec0 = chunk_decay_row(t0["e"])[1][:, SSD_WIDTH:]
        chunk_output(t0, sf0, sb0)
        s_new = sb0 * dec0 + csb_ref[c0]
        if pair_is_sequence:
            sbo_ref[q] = jnp.transpose(s_new)
        else:
            sb_ref[...] = s_new
        return carry

    lax.fori_loop(0, n_pairs, bwd_pair, 0)

    def out_body(i, carry):
        r0 = pl.multiple_of(i * ROW_BLOCK, ROW_BLOCK)
        rows = pl.ds(r0, ROW_BLOCK)
        o = _dot(ymix_ref[rows, :], wout_ref[...])
        out_ref[rows, :] = x_ref[rows, :] + gate_m * o
        return carry

    lax.fori_loop(0, SLAB // ROW_BLOCK, out_body, 0)


def _const_spec(shape):
    nd = len(shape)
    return pl.BlockSpec(shape, lambda i: (0,) * nd)


def _mixer_call(x2d, mod, mod_per_slab, seg_len, init_f, init_b, consts):
    n_slabs = x2d.shape[0] // SLAB
    seg_chunks = seg_len // CHUNK
    segs_per_slab = SLAB // seg_len
    has_init = init_f is not None
    state_rows = SSD_HEADS * SSD_HEAD_DIM
    staged_rows = SLAB + PAD * (1 + segs_per_slab)

    mod_map = (lambda i: (i, 0, 0)) if mod_per_slab else (lambda i: (0, 0, 0))
    in_specs = [
        pl.BlockSpec((SLAB, D_MODEL), lambda i: (i, 0)),
        pl.BlockSpec((1, N_MOD, D_MODEL), mod_map),
    ]
    args = [x2d, mod]
    if has_init:
        in_specs += [pl.BlockSpec((1, state_rows, D_STATE), lambda i: (i, 0, 0))] * 2
        args += [init_f, init_b]
    in_specs += [_const_spec(a.shape) for a in consts]
    args += list(consts)

    out_specs = [pl.BlockSpec((SLAB, D_MODEL), lambda i: (i, 0))]
    out_shape = [jax.ShapeDtypeStruct(x2d.shape, F32)]
    if not has_init:
        n_seq = n_slabs * segs_per_slab
        out_specs += [pl.BlockSpec((segs_per_slab, state_rows, D_STATE), lambda i: (i, 0, 0))] * 2
        out_shape += [jax.ShapeDtypeStruct((n_seq, state_rows, D_STATE), F32)] * 2

    scratch = [
        pltpu.VMEM((staged_rows, CONV_WIDTH), F32),
        pltpu.VMEM((staged_rows, 2 * SSD_WIDTH), F32),
        pltpu.VMEM((SLAB, CONV_WIDTH), F32),
        pltpu.VMEM((SLAB, SSD_WIDTH), F32),
        pltpu.VMEM((SLAB, LANES), F32),
        pltpu.VMEM((SLAB, LANES), F32),
        pltpu.VMEM((SLAB, SSD_WIDTH), F32),
        pltpu.VMEM((SLAB, 2 * SSD_GROUPS * D_STATE), BF16),
        pltpu.VMEM((SLAB, D_MODEL), BF16),
        pltpu.VMEM((N_CHUNKS, D_STATE, SSD_WIDTH), F32),
        pltpu.VMEM((N_CHUNKS, D_STATE, SSD_WIDTH), F32),
        pltpu.VMEM((D_STATE, SSD_WIDTH), F32),
        pltpu.VMEM((D_STATE, SSD_WIDTH), F32),
    ]
    return pl.pallas_call(
        functools.partial(_mixer_kernel, seg_chunks, has_init),
        grid=(n_slabs,),
        in_specs=in_specs,
        out_specs=out_specs,
        out_shape=out_shape,
        scratch_shapes=scratch,
        compiler_params=pltpu.CompilerParams(
            dimension_semantics=("arbitrary",), vmem_limit_bytes=VMEM_LIMIT),
        name="mixer_latent" if has_init else "mixer_prompt",
    )(*args)


def _ffn_conv(u, wc_ref, bc_ref, is_grid):
    n = u.shape[0]
    row = lax.broadcasted_iota(jnp.int32, (n, 1), 0)
    period = GRID_W if is_grid else 256
    pos = jnp.bitwise_and(row, period - 1)
    um = jnp.where(pos == 0, 0.0, pltpu.roll(u, 1, 0)).astype(BF16)
    up = jnp.where(pos == period - 1, 0.0, pltpu.roll(u, n - 1, 0)).astype(BF16)
    u = u.astype(BF16)
    wc = wc_ref[...].astype(BF16)

    def taps(di):
        return um * wc[3 * di:3 * di + 1, :] + u * wc[3 * di + 1:3 * di + 2, :] + up * wc[3 * di + 2:3 * di + 3, :]

    out = taps(1) + bc_ref[...].astype(BF16)
    if is_grid:
        zeros = jnp.zeros((GRID_W, u.shape[1]), BF16)
        out = out + jnp.concatenate([zeros, taps(0)[:n - GRID_W]], axis=0)
        out = out + jnp.concatenate([taps(2)[GRID_W:], zeros], axis=0)
    return out.astype(F32)


def _ffn_kernel(is_grid, x_ref, mod_ref, g2_ref, wup_ref, wc_ref, bc_ref, wdn_ref, gfin_ref, out_ref,
                h2_ref, acc_ref):
    j = pl.program_id(1)

    @pl.when(j == 0)
    def _():
        shift_f = mod_ref[0, 3:4, :]
        scale_f = mod_ref[0, 4:5, :]
        x = x_ref[...]
        ms = jnp.mean(x * x, axis=-1, keepdims=True)
        h2 = (x * lax.rsqrt(ms + EPS)) * g2_ref[...] * (1.0 + scale_f) + shift_f
        h2_ref[...] = h2.astype(BF16)
        acc_ref[...] = jnp.zeros_like(acc_ref)

    def up_proj(b):
        return _dot(h2_ref[...], wup_ref[b])

    def run_blocks(n_blocks):
        u_next = up_proj(0)
        for b in range(n_blocks):
            u = u_next
            if b + 1 < n_blocks:
                u_next = up_proj(b + 1)
            u = _ffn_conv(u, wc_ref.at[b], bc_ref.at[b], is_grid)
            a = (_silu(u[:, 0:FF_BLOCK]) * u[:, FF_BLOCK:]).astype(BF16)
            acc_ref[...] += _dot(a, wdn_ref[b])

    n_last = N_FF_BLOCKS - (N_FF_STEPS - 1) * FF_BLOCKS_PER_STEP

    @pl.when(j < N_FF_STEPS - 1)
    def _():
        run_blocks(FF_BLOCKS_PER_STEP)

    @pl.when(j == N_FF_STEPS - 1)
    def _():
        run_blocks(n_last)

    @pl.when(j == N_FF_STEPS - 1)
    def _():
        gate_f = mod_ref[0, 5:6, :]
        x2 = x_ref[...] + gate_f * acc_ref[...]
        ms = jnp.mean(x2 * x2, axis=-1, keepdims=True)
        out_ref[...] = (x2 * lax.rsqrt(ms + EPS)) * gfin_ref[...]


def _ffn_call(x2d, mod, mod_per_slab, is_grid, consts):
    n_slabs = x2d.shape[0] // SLAB
    g2, wup3, wc3, bc3, wdn3, gfin = consts
    mod_map = (lambda i, j: (i, 0, 0)) if mod_per_slab else (lambda i, j: (0, 0, 0))
    return pl.pallas_call(
        functools.partial(_ffn_kernel, is_grid),
        grid=(n_slabs, N_FF_STEPS),
        in_specs=[
            pl.BlockSpec((SLAB, D_MODEL), lambda i, j: (i, 0)),
            pl.BlockSpec((1, N_MOD, D_MODEL), mod_map),
            pl.BlockSpec((1, D_MODEL), lambda i, j: (0, 0)),
            pl.BlockSpec((FF_BLOCKS_PER_STEP, D_MODEL, 2 * FF_BLOCK), lambda i, j: (j, 0, 0)),
            pl.BlockSpec((FF_BLOCKS_PER_STEP, 9, 2 * FF_BLOCK), lambda i, j: (j, 0, 0)),
            pl.BlockSpec((FF_BLOCKS_PER_STEP, 1, 2 * FF_BLOCK), lambda i, j: (j, 0, 0)),
            pl.BlockSpec((FF_BLOCKS_PER_STEP, FF_BLOCK, D_MODEL), lambda i, j: (j, 0, 0)),
            pl.BlockSpec((1, D_MODEL), lambda i, j: (0, 0)),
        ],
        out_specs=pl.BlockSpec((SLAB, D_MODEL), lambda i, j: (i, 0)),
        out_shape=jax.ShapeDtypeStruct(x2d.shape, F32),
        scratch_shapes=[
            pltpu.VMEM((SLAB, D_MODEL), BF16),
            pltpu.VMEM((SLAB, D_MODEL), F32),
        ],
        compiler_params=pltpu.CompilerParams(
            dimension_semantics=("arbitrary", "arbitrary"), vmem_limit_bytes=VMEM_LIMIT),
        name="ffn_latent" if is_grid else "ffn_prompt",
    )(x2d, mod, g2, wup3, wc3, bc3, wdn3, gfin)


def _pad_lanes(v):
    return jnp.pad(v.reshape(1, -1), ((0, 0), (0, LANES - v.size)))


def kernel(x_prompt, x_sample, state_ssd_fwd, state_ssd_bwd, c, c_ctx, g_norm1, g_norm2, w_ada, b_ada, w_in, w_conv_short, w_conv_ssd, b_conv_ssd, dt_bias, a_log, d_skip, g_ssd_norm, w_out, w_up, w_ffn_conv, b_ffn_conv, w_down, g_final):
    depth = w_in.shape[0]
    assert depth == 1, "kernel is written for the single-layer problem"
    bp, seq = x_prompt.shape[0], x_prompt.shape[1]
    bd, dec_seq = x_sample.shape[0], x_sample.shape[1]
    assert seq == 256 and dec_seq == SLAB and (bp * seq) % SLAB == 0

    n_c = bd + 1
    rows = -(-n_c // 8) * 8
    cvec = jnp.concatenate([c, c_ctx[None], jnp.zeros((rows - n_c, D_MODEL), F32)], axis=0)
    mod = _mod_call(cvec, w_ada[0], b_ada[0].reshape(1, -1)).reshape(rows, N_MOD, D_MODEL)
    mod_lat = mod[:bd]
    mod_ctx = mod[bd:bd + 1]

    w_in_main = _cast_bf16(w_in[0], DT_COL)
    w_dt = jnp.pad(w_in[0][:, DT_COL:], ((0, 0), (0, LANES - 2 * SSD_HEADS))).astype(BF16)
    idx = jnp.arange(CHUNK)
    ltri = (idx[:, None] >= idx[None, :]).astype(BF16)
    utri = (idx[:, None] <= idx[None, :]).astype(BF16)
    tri3 = jnp.concatenate([jnp.concatenate([ltri] * 3, axis=1), jnp.concatenate([utri] * 3, axis=1)], axis=0)
    expand = (jnp.arange(LANES)[:, None] == (jnp.arange(2 * SSD_WIDTH)[None, :] // SSD_HEAD_DIM)).astype(BF16)
    mixer_consts = (
        g_norm1[0].reshape(1, -1), w_in_main, w_dt, w_conv_short[0], w_conv_ssd[0], b_conv_ssd[0].reshape(1, -1),
        _pad_lanes(dt_bias[0]), _pad_lanes(a_log[0]),
        jnp.repeat(d_skip[0], SSD_HEAD_DIM).reshape(1, -1), g_ssd_norm[0].reshape(1, -1),
        _cast_bf16(w_out[0], D_MODEL), tri3, expand, jnp.concatenate([expand] * 3, axis=0),
    )
    wup3 = _up_blocks(w_up[0])
    pad_blocks = ((0, N_FF_PADDED - N_FF_BLOCKS), (0, 0), (0, 0))
    wc3 = jnp.pad(w_ffn_conv[0].reshape(9, 2, N_FF_BLOCKS, FF_BLOCK).transpose(2, 0, 1, 3)
                  .reshape(N_FF_BLOCKS, 9, 2 * FF_BLOCK), pad_blocks)
    bc3 = jnp.pad(b_ffn_conv[0].reshape(2, N_FF_BLOCKS, FF_BLOCK).transpose(1, 0, 2)
                  .reshape(N_FF_BLOCKS, 1, 2 * FF_BLOCK), pad_blocks)
    wdn3 = _cast_bf16(w_down[0], D_MODEL, N_FF_PADDED - N_FF_BLOCKS).reshape(N_FF_PADDED, FF_BLOCK, D_MODEL)
    ffn_consts = (g_norm2[0].reshape(1, -1), wup3, wc3, bc3, wdn3, g_final.reshape(1, -1))

    state_rows = SSD_HEADS * SSD_HEAD_DIM
    init_f = state_ssd_fwd[:, 0].reshape(bd, state_rows, D_STATE)
    init_b = state_ssd_bwd[:, 0].reshape(bd, state_rows, D_STATE)

    xp2d = x_prompt.reshape(bp * seq, D_MODEL)
    xs2d = x_sample.reshape(bd * dec_seq, D_MODEL)

    xp1, s_f, s_b = _mixer_call(xp2d, mod_ctx, False, seq, None, None, mixer_consts)
    (xs1,) = _mixer_call(xs2d, mod_lat, True, dec_seq, init_f, init_b, mixer_consts)

    y_prompt = _ffn_call(xp1, mod_ctx, False, False, ffn_consts).reshape(x_prompt.shape)
    y_sample = _ffn_call(xs1, mod_lat, True, True, ffn_consts).reshape(x_sample.shape)

    state_shape = (bp, depth, SSD_HEADS, SSD_HEAD_DIM, D_STATE)
    return (y_prompt, y_sample, s_f.reshape(state_shape), s_b.reshape(state_shape))
```

```python
import functools

import jax
import jax.numpy as jnp
from jax import lax
from jax.experimental import pallas as pl
from jax.experimental.pallas import tpu as pltpu

F32 = jnp.float32
BF16 = jnp.bfloat16

D_MODEL = 1024
GRID_W = 64
CONV_WIDTH = 512
SSD_WIDTH = 512
SSD_HEAD_DIM = 64
SSD_HEADS = 8
SSD_GROUPS = 2
D_STATE = 128
CHUNK = 128
D_FF = 2816
N_MOD = 6
EPS = 1e-6

SLAB = 1024
N_CHUNKS = SLAB // CHUNK
CHUNK_GROUP = 4
ROW_BLOCK = 256
PAD = 8
FF_BLOCK = 256
N_FF_BLOCKS = D_FF // FF_BLOCK
FF_BLOCKS_PER_STEP = 3
N_FF_STEPS = -(-N_FF_BLOCKS // FF_BLOCKS_PER_STEP)
N_FF_PADDED = N_FF_STEPS * FF_BLOCKS_PER_STEP
LANES = 128
DT_COL = 3072
VMEM_LIMIT = 56 * 1024 * 1024


def _silu(v):
    return v / (1.0 + jnp.exp(-v))


def _softplus(v):
    return jnp.maximum(v, 0.0) + jnp.log1p(jnp.exp(-jnp.abs(v)))


def _split3(v):
    hi = v.astype(BF16)
    r1 = v - hi.astype(F32)
    mid = r1.astype(BF16)
    lo = (r1 - mid.astype(F32)).astype(BF16)
    return hi, mid, lo


def _dot(a, b):
    return jnp.dot(a, b, preferred_element_type=F32)


def _dot3(m3, v):
    return _dot(m3, jnp.concatenate(_split3(v), axis=0))


def _expand3(v, expand3):
    return _dot(jnp.concatenate(_split3(v), axis=1), expand3)


def _conv3_rows(win, w_ref):
    n = win.shape[0]
    prev = pltpu.roll(win, 1, 0)[PAD:PAD + CHUNK]
    cur = win[PAD:PAD + CHUNK]
    nxt = pltpu.roll(win, n - 1, 0)[PAD:PAD + CHUNK]
    return prev * w_ref[0:1, :] + cur * w_ref[1:2, :] + nxt * w_ref[2:3, :]


def _mod_kernel(c_ref, w_ref, b_ref, o_ref):
    s = _silu(c_ref[...]).astype(BF16)
    o_ref[...] = _dot(s, w_ref[...].astype(BF16)) + b_ref[...]


def _mod_call(cvec, w_ada, b_ada):
    rows = cvec.shape[0]
    return pl.pallas_call(
        _mod_kernel,
        grid=(N_MOD,),
        in_specs=[
            pl.BlockSpec((rows, D_MODEL), lambda j: (0, 0)),
            pl.BlockSpec((D_MODEL, D_MODEL), lambda j: (0, j)),
            pl.BlockSpec((1, D_MODEL), lambda j: (0, j)),
        ],
        out_specs=pl.BlockSpec((rows, D_MODEL), lambda j: (0, j)),
        out_shape=jax.ShapeDtypeStruct((rows, N_MOD * D_MODEL), F32),
        compiler_params=pltpu.CompilerParams(dimension_semantics=("arbitrary",)),
        name="mod_vectors",
    )(cvec, w_ada, b_ada)


def _cast_kernel(n_src_blocks, src_ref, dst_ref):
    i = pl.program_id(0)

    @pl.when(i < n_src_blocks)
    def _():
        dst_ref[...] = src_ref[...].astype(BF16)

    @pl.when(i >= n_src_blocks)
    def _():
        dst_ref[...] = jnp.zeros_like(dst_ref)


def _cast_bf16(w, cols, pad_row_blocks=0):
    n_src = w.shape[0] // ROW_BLOCK
    n_dst = n_src + pad_row_blocks
    return pl.pallas_call(
        functools.partial(_cast_kernel, n_src),
        grid=(n_dst,),
        in_specs=[pl.BlockSpec((ROW_BLOCK, cols), lambda i: (jnp.minimum(i, n_src - 1), 0))],
        out_specs=pl.BlockSpec((ROW_BLOCK, cols), lambda i: (i, 0)),
        out_shape=jax.ShapeDtypeStruct((n_dst * ROW_BLOCK, cols), BF16),
        compiler_params=pltpu.CompilerParams(dimension_semantics=("arbitrary",)),
        name="cast_bf16",
    )(w)


def _up_blocks_kernel(gate_ref, value_ref, dst_ref):
    j = pl.program_id(0)

    @pl.when(j < N_FF_BLOCKS)
    def _():
        dst_ref[0, :, 0:FF_BLOCK] = gate_ref[...].astype(BF16)
        dst_ref[0, :, FF_BLOCK:] = value_ref[...].astype(BF16)

    @pl.when(j >= N_FF_BLOCKS)
    def _():
        dst_ref[...] = jnp.zeros_like(dst_ref)


def _up_blocks(w_up):
    last = N_FF_BLOCKS - 1
    return pl.pallas_call(
        _up_blocks_kernel,
        grid=(N_FF_PADDED,),
        in_specs=[
            pl.BlockSpec((D_MODEL, FF_BLOCK), lambda j: (0, jnp.minimum(j, last))),
            pl.BlockSpec((D_MODEL, FF_BLOCK), lambda j: (0, N_FF_BLOCKS + jnp.minimum(j, last))),
        ],
        out_specs=pl.BlockSpec((1, D_MODEL, 2 * FF_BLOCK), lambda j: (j, 0, 0)),
        out_shape=jax.ShapeDtypeStruct((N_FF_PADDED, D_MODEL, 2 * FF_BLOCK), BF16),
        compiler_params=pltpu.CompilerParams(dimension_semantics=("arbitrary",)),
        name="up_blocks",
    )(w_up, w_up)


def _mixer_kernel(seg_chunks, has_init, *refs):
    refs = list(refs)
    x_ref, mod_ref = refs[:2]
    refs = refs[2:]
    if has_init:
        initf_ref, initb_ref = refs[:2]
        refs = refs[2:]
    (g1_ref, win_ref, wdt_ref, wcs_ref, wcx_ref, bcx_ref, dtb_ref, alog_ref, dskip_ref, gssd_ref, wout_ref,
     tri3_ref, exp_ref, exp3_ref) = refs[:14]
    refs = refs[14:]
    out_ref = refs[0]
    refs = refs[1:]
    if not has_init:
        sfo_ref, sbo_ref = refs[:2]
        refs = refs[2:]
    (gchc_ref, xbc_ref, gb_ref, z_ref, dt_ref, e_ref, xs_ref, bc_ref, ymix_ref, sfin_ref, csb_ref,
     sf_ref, sb_ref) = refs

    shift_m = mod_ref[0, 0:1, :]
    scale_m = mod_ref[0, 1:2, :]
    gate_m = mod_ref[0, 2:3, :]

    seg_len = seg_chunks * CHUNK
    zpad_a = jnp.zeros((PAD, CONV_WIDTH), F32)
    zpad_x = jnp.zeros((PAD, 2 * SSD_WIDTH), F32)
    for s in range(SLAB // seg_len + 1):
        gap = slice(s * (seg_len + PAD), s * (seg_len + PAD) + PAD)
        gchc_ref[gap, :] = zpad_a
        xbc_ref[gap, :] = zpad_x

    def staged_row(r):
        return pl.multiple_of(r + lax.div(r, seg_len) * PAD, PAD)

    def proj_body(i, carry):
        r0 = pl.multiple_of(i * ROW_BLOCK, ROW_BLOCK)
        rp = pl.multiple_of(staged_row(i * ROW_BLOCK) + PAD, PAD)
        xb = x_ref[pl.ds(r0, ROW_BLOCK), :]
        ms = jnp.mean(xb * xb, axis=-1, keepdims=True)
        h = (xb * lax.rsqrt(ms + EPS)) * g1_ref[...] * (1.0 + scale_m) + shift_m
        hb = h.astype(BF16)

        def proj(lo, hi):
            return _dot(hb, win_ref[:, lo:hi])

        gchc_ref[pl.ds(rp, ROW_BLOCK), :] = proj(0, 512) * proj(1024, 1536)
        gb_ref[pl.ds(r0, ROW_BLOCK), :] = proj(512, 1024)
        z_ref[pl.ds(r0, ROW_BLOCK), :] = proj(1536, 2048)
        xbc_ref[pl.ds(rp, ROW_BLOCK), :] = proj(2048, 3072)
        dt_ref[pl.ds(r0, ROW_BLOCK), :] = _softplus(_dot(hb, wdt_ref[...]) + dtb_ref[...])
        return carry

    lax.fori_loop(0, SLAB // ROW_BLOCK, proj_body, 0)

    a_row = -jnp.exp(alog_ref[...])
    lane_t = lax.broadcasted_iota(jnp.int32, (CHUNK, LANES), 1)
    lane_1 = lax.broadcasted_iota(jnp.int32, (1, LANES), 1)
    is_fwd_t = lane_t < SSD_HEADS
    is_fwd_1 = lane_1 < SSD_HEADS

    def chunk_decay_row(e):
        tot = jnp.where(is_fwd_1, e[CHUNK - 1:CHUNK, :], e[0:1, :])
        dec = jnp.broadcast_to(jnp.exp(tot), (8, LANES))
        return tot, _expand3(dec, exp3_ref[...])[0:1, :]

    pair_is_sequence = seg_chunks == 2
    assert pair_is_sequence != has_init and seg_chunks in (2, N_CHUNKS)
    n_groups = N_CHUNKS // CHUNK_GROUP

    def fwd_convs(c):
        r0 = pl.multiple_of(c * CHUNK, CHUNK)
        rows = pl.ds(r0, CHUNK)
        w0 = staged_row(c * CHUNK)
        xc = _conv3_rows(xbc_ref[pl.ds(w0, CHUNK + 2 * PAD), :], wcx_ref)
        xc = _silu(xc + bcx_ref[...])
        xs_ref[rows, :] = xc[:, 0:SSD_WIDTH]
        bc_ref[rows, :] = xc[:, SSD_WIDTH:].astype(BF16)
        ca = _conv3_rows(gchc_ref[pl.ds(w0, CHUNK + 2 * PAD), :], wcs_ref)
        ymix_ref[rows, 0:CONV_WIDTH] = (gb_ref[rows, :] * ca).astype(BF16)
        return xc

    def fwd_decays(c):
        rows = pl.ds(pl.multiple_of(c * CHUNK, CHUNK), CHUNK)
        dtt = dt_ref[rows, :]
        da = dtt * a_row
        sums = _dot3(tri3_ref[...], da)
        e = jnp.where(is_fwd_t, sums[0:CHUNK], sums[CHUNK:])
        e_ref[rows, :] = e
        tot, dec_row = chunk_decay_row(e)
        wst = (dtt * jnp.exp(tot - e)).astype(BF16)
        return _dot(wst, exp_ref[...]), dec_row

    def fwd_chunk_states(xc, wexp):
        xs = xc[:, 0:SSD_WIDTH]
        xst_f = (xs * wexp[:, 0:SSD_WIDTH]).astype(BF16)
        xst_b = (xs * wexp[:, SSD_WIDTH:]).astype(BF16)
        cst_f, cst_b = [], []
        for g in range(SSD_GROUPS):
            bt = jnp.transpose(xc[:, SSD_WIDTH + g * D_STATE:SSD_WIDTH + (g + 1) * D_STATE]).astype(BF16)
            cst_f.append(_dot(bt, xst_f[:, g * 256:(g + 1) * 256]))
            cst_b.append(_dot(bt, xst_b[:, g * 256:(g + 1) * 256]))
        return jnp.concatenate(cst_f, axis=1), jnp.concatenate(cst_b, axis=1)

    if has_init:
        sf_ref[...] = jnp.transpose(initf_ref[0])
        sb_ref[...] = jnp.transpose(initb_ref[0])

    def fwd_group(p, carry):
        chunks = [CHUNK_GROUP * p + i for i in range(CHUNK_GROUP)]
        xcs = [fwd_convs(c) for c in chunks]
        decays = [fwd_decays(c) for c in chunks]
        states = [fwd_chunk_states(xc, wexp) for xc, (wexp, _) in zip(xcs, decays)]
        state = None if pair_is_sequence else sf_ref[...]
        for i, c in enumerate(chunks):
            cst_f, cst_b = states[i]
            csb_ref[c] = cst_b
            s_in = None if (pair_is_sequence and i % 2 == 0) else state
            if s_in is None:
                state = cst_f
            else:
                sfin_ref[c] = s_in
                state = s_in * decays[i][1][:, 0:SSD_WIDTH] + cst_f
            if pair_is_sequence and i % 2 == 1:
                sfo_ref[p * (CHUNK_GROUP // 2) + i // 2] = jnp.transpose(state)
        if not pair_is_sequence:
            sf_ref[...] = state
        return carry

    lax.fori_loop(0, n_groups, fwd_group, 0)

    row_t = lax.broadcasted_iota(jnp.int32, (CHUNK, CHUNK), 0)
    col_t = lax.broadcasted_iota(jnp.int32, (CHUNK, CHUNK), 1)
    causal = row_t >= col_t
    anti = col_t >= row_t
    low_half = lane_t < SSD_HEAD_DIM

    def chunk_tables(c):
        rows = pl.ds(pl.multiple_of(c * CHUNK, CHUNK), CHUNK)
        e = e_ref[rows, :]
        xs = xs_ref[rows, :]
        dtexp = _dot(dt_ref[rows, :].astype(BF16), exp_ref[...])
        ecs = _dot(jnp.exp(e).astype(BF16), exp_ref[...])
        return dict(rows=rows, e=e, et=jnp.transpose(e), xs=xs, bc=bc_ref[rows, :], ecs=ecs,
                    xdt_f=xs * dtexp[:, 0:SSD_WIDTH], xdt_b=xs * dtexp[:, SSD_WIDTH:])

    def chunk_output(t, s_f, s_b):
        e, et, xs, bc, ecs, xdt_f, xdt_b = t["e"], t["et"], t["xs"], t["bc"], t["ecs"], t["xdt_f"], t["xdt_b"]
        y_parts = []
        for g in range(SSD_GROUPS):
            b_g = bc[:, g * D_STATE:(g + 1) * D_STATE]
            c_g = bc[:, 256 + g * D_STATE:256 + (g + 1) * D_STATE]
            gmat = lax.dot_general(c_g, b_g, (((1,), (1,)), ((), ())), preferred_element_type=F32)
            y_off = jnp.zeros((CHUNK, 256), F32)
            if s_f is not None:
                y_off = y_off + _dot(c_g, s_f[:, g * 256:(g + 1) * 256].astype(BF16)) \
                    * ecs[:, g * 256:(g + 1) * 256]
            if s_b is not None:
                y_off = y_off + _dot(c_g, s_b[:, g * 256:(g + 1) * 256].astype(BF16)) \
                    * ecs[:, SSD_WIDTH + g * 256:SSD_WIDTH + (g + 1) * 256]
            for pair in range(2):
                p0 = g * 256 + pair * LANES
                xf_pair = xdt_f[:, p0:p0 + LANES]
                xb_pair = xdt_b[:, p0:p0 + LANES]
                scores, rhs = [], []
                for sub in range(2):
                    head = g * 4 + pair * 2 + sub
                    keep = low_half if sub == 0 else jnp.logical_not(low_half)
                    rhs.append(jnp.where(keep, xf_pair, 0.0).astype(BF16))
                    rhs.append(jnp.where(keep, xb_pair, 0.0).astype(BF16))
                    jf, jb = head, SSD_HEADS + head
                    d_f = e[:, jf:jf + 1] - et[jf:jf + 1, :]
                    d_b = e[:, jb:jb + 1] - et[jb:jb + 1, :]
                    scores.append((gmat * jnp.where(causal, jnp.exp(d_f), 0.0)).astype(BF16))
                    scores.append((gmat * jnp.where(anti, jnp.exp(d_b), 0.0)).astype(BF16))
                y_diag = _dot(jnp.concatenate(scores, axis=1), jnp.concatenate(rhs, axis=0))
                y_parts.append(y_off[:, pair * LANES:(pair + 1) * LANES] + y_diag)
        y = jnp.concatenate(y_parts, axis=1) + xs * dskip_ref[...]
        yz = y * _silu(z_ref[t["rows"], :])
        ms = jnp.mean(yz * yz, axis=-1, keepdims=True)
        yn = (yz * lax.rsqrt(ms + EPS)) * gssd_ref[...]
        ymix_ref[t["rows"], CONV_WIDTH:] = yn.astype(BF16)

    def bwd_group(k, carry):
        q = n_groups - 1 - k
        state = None if pair_is_sequence else sb_ref[...]
        for i in reversed(range(CHUNK_GROUP)):
            c = CHUNK_GROUP * q + i
            starts_sequence = pair_is_sequence and i % 2 == 0
            s_b = None if (pair_is_sequence and i % 2 == 1) else state
            t = chunk_tables(c)
            chunk_output(t, None if starts_sequence else sfin_ref[c], s_b)
            if s_b is None:
                state = csb_ref[c]
            else:
                state = s_b * chunk_decay_row(t["e"])[1][:, SSD_WIDTH:] + csb_ref[c]
            if starts_sequence:
                sbo_ref[q * (CHUNK_GROUP // 2) + i // 2] = jnp.transpose(state)
        if not pair_is_sequence:
            sb_ref[...] = state
        return carry

    lax.fori_loop(0, n_groups, bwd_group, 0)

    def out_body(i, carry):
        r0 = pl.multiple_of(i * ROW_BLOCK, ROW_BLOCK)
        rows = pl.ds(r0, ROW_BLOCK)
        o = _dot(ymix_ref[rows, :], wout_ref[...])
        out_ref[rows, :] = x_ref[rows, :] + gate_m * o
        return carry

    lax.fori_loop(0, SLAB // ROW_BLOCK, out_body, 0)


def _const_spec(shape):
    nd = len(shape)
    return pl.BlockSpec(shape, lambda i: (0,) * nd)


def _mixer_call(x2d, mod, mod_per_slab, seg_len, init_f, init_b, consts):
    n_slabs = x2d.shape[0] // SLAB
    seg_chunks = seg_len // CHUNK
    segs_per_slab = SLAB // seg_len
    has_init = init_f is not None
    state_rows = SSD_HEADS * SSD_HEAD_DIM
    staged_rows = SLAB + PAD * (1 + segs_per_slab)

    mod_map = (lambda i: (i, 0, 0)) if mod_per_slab else (lambda i: (0, 0, 0))
    in_specs = [
        pl.BlockSpec((SLAB, D_MODEL), lambda i: (i, 0)),
        pl.BlockSpec((1, N_MOD, D_MODEL), mod_map),
    ]
    args = [x2d, mod]
    if has_init:
        in_specs += [pl.BlockSpec((1, state_rows, D_STATE), lambda i: (i, 0, 0))] * 2
        args += [init_f, init_b]
    in_specs += [_const_spec(a.shape) for a in consts]
    args += list(consts)

    out_specs = [pl.BlockSpec((SLAB, D_MODEL), lambda i: (i, 0))]
    out_shape = [jax.ShapeDtypeStruct(x2d.shape, F32)]
    if not has_init:
        n_seq = n_slabs * segs_per_slab
        out_specs += [pl.BlockSpec((segs_per_slab, state_rows, D_STATE), lambda i: (i, 0, 0))] * 2
        out_shape += [jax.ShapeDtypeStruct((n_seq, state_rows, D_STATE), F32)] * 2

    scratch = [
        pltpu.VMEM((staged_rows, CONV_WIDTH), F32),
        pltpu.VMEM((staged_rows, 2 * SSD_WIDTH), F32),
        pltpu.VMEM((SLAB, CONV_WIDTH), F32),
        pltpu.VMEM((SLAB, SSD_WIDTH), F32),
        pltpu.VMEM((SLAB, LANES), F32),
        pltpu.VMEM((SLAB, LANES), F32),
        pltpu.VMEM((SLAB, SSD_WIDTH), F32),
        pltpu.VMEM((SLAB, 2 * SSD_GROUPS * D_STATE), BF16),
        pltpu.VMEM((SLAB, D_MODEL), BF16),
        pltpu.VMEM((N_CHUNKS, D_STATE, SSD_WIDTH), F32),
        pltpu.VMEM((N_CHUNKS, D_STATE, SSD_WIDTH), F32),
        pltpu.VMEM((D_STATE, SSD_WIDTH), F32),
        pltpu.VMEM((D_STATE, SSD_WIDTH), F32),
    ]
    return pl.pallas_call(
        functools.partial(_mixer_kernel, seg_chunks, has_init),
        grid=(n_slabs,),
        in_specs=in_specs,
        out_specs=out_specs,
        out_shape=out_shape,
        scratch_shapes=scratch,
        compiler_params=pltpu.CompilerParams(
            dimension_semantics=("arbitrary",), vmem_limit_bytes=VMEM_LIMIT),
        name="mixer_latent" if has_init else "mixer_prompt",
    )(*args)


def _ffn_conv(u, wc_ref, bc_ref, is_grid):
    n = u.shape[0]
    row = lax.broadcasted_iota(jnp.int32, (n, 1), 0)
    period = GRID_W if is_grid else 256
    pos = jnp.bitwise_and(row, period - 1)
    um = jnp.where(pos == 0, 0.0, pltpu.roll(u, 1, 0)).astype(BF16)
    up = jnp.where(pos == period - 1, 0.0, pltpu.roll(u, n - 1, 0)).astype(BF16)
    u = u.astype(BF16)
    wc = wc_ref[...].astype(BF16)

    def taps(di):
        return um * wc[3 * di:3 * di + 1, :] + u * wc[3 * di + 1:3 * di + 2, :] + up * wc[3 * di + 2:3 * di + 3, :]

    out = taps(1) + bc_ref[...].astype(BF16)
    if is_grid:
        zeros = jnp.zeros((GRID_W, u.shape[1]), BF16)
        out = out + jnp.concatenate([zeros, taps(0)[:n - GRID_W]], axis=0)
        out = out + jnp.concatenate([taps(2)[GRID_W:], zeros], axis=0)
    return out.astype(F32)


def _ffn_kernel(is_grid, x_ref, mod_ref, g2_ref, wup_ref, wc_ref, bc_ref, wdn_ref, gfin_ref, out_ref,
                h2_ref, acc_ref):
    j = pl.program_id(1)

    @pl.when(j == 0)
    def _():
        shift_f = mod_ref[0, 3:4, :]
        scale_f = mod_ref[0, 4:5, :]
        x = x_ref[...]
        ms = jnp.mean(x * x, axis=-1, keepdims=True)
        h2 = (x * lax.rsqrt(ms + EPS)) * g2_ref[...] * (1.0 + scale_f) + shift_f
        h2_ref[...] = h2.astype(BF16)
        acc_ref[...] = jnp.zeros_like(acc_ref)

    def up_proj(b):
        return _dot(h2_ref[...], wup_ref[b])

    def run_blocks(n_blocks):
        u_next = up_proj(0)
        for b in range(n_blocks):
            u = u_next
            if b + 1 < n_blocks:
                u_next = up_proj(b + 1)
            u = _ffn_conv(u, wc_ref.at[b], bc_ref.at[b], is_grid)
            a = (_silu(u[:, 0:FF_BLOCK]) * u[:, FF_BLOCK:]).astype(BF16)
            acc_ref[...] += _dot(a, wdn_ref[b])

    n_last = N_FF_BLOCKS - (N_FF_STEPS - 1) * FF_BLOCKS_PER_STEP

    @pl.when(j < N_FF_STEPS - 1)
    def _():
        run_blocks(FF_BLOCKS_PER_STEP)

    @pl.when(j == N_FF_STEPS - 1)
    def _():
        run_blocks(n_last)

    @pl.when(j == N_FF_STEPS - 1)
    def _():
        gate_f = mod_ref[0, 5:6, :]
        x2 = x_ref[...] + gate_f * acc_ref[...]
        ms = jnp.mean(x2 * x2, axis=-1, keepdims=True)
        out_ref[...] = (x2 * lax.rsqrt(ms + EPS)) * gfin_ref[...]


def _ffn_call(x2d, mod, mod_per_slab, is_grid, consts):
    n_slabs = x2d.shape[0] // SLAB
    g2, wup3, wc3, bc3, wdn3, gfin = consts
    mod_map = (lambda i, j: (i, 0, 0)) if mod_per_slab else (lambda i, j: (0, 0, 0))
    return pl.pallas_call(
        functools.partial(_ffn_kernel, is_grid),
        grid=(n_slabs, N_FF_STEPS),
        in_specs=[
            pl.BlockSpec((SLAB, D_MODEL), lambda i, j: (i, 0)),
            pl.BlockSpec((1, N_MOD, D_MODEL), mod_map),
            pl.BlockSpec((1, D_MODEL), lambda i, j: (0, 0)),
            pl.BlockSpec((FF_BLOCKS_PER_STEP, D_MODEL, 2 * FF_BLOCK), lambda i, j: (j, 0, 0)),
            pl.BlockSpec((FF_BLOCKS_PER_STEP, 9, 2 * FF_BLOCK), lambda i, j: (j, 0, 0)),
            pl.BlockSpec((FF_BLOCKS_PER_STEP, 1, 2 * FF_BLOCK), lambda i, j: (j, 0, 0)),
            pl.BlockSpec((FF_BLOCKS_PER_STEP, FF_BLOCK, D_MODEL), lambda i, j: (j, 0, 0)),
            pl.BlockSpec((1, D_MODEL), lambda i, j: (0, 0)),
        ],
        out_specs=pl.BlockSpec((SLAB, D_MODEL), lambda i, j: (i, 0)),
        out_shape=jax.ShapeDtypeStruct(x2d.shape, F32),
        scratch_shapes=[
            pltpu.VMEM((SLAB, D_MODEL), BF16),
            pltpu.VMEM((SLAB, D_MODEL), F32),
        ],
        compiler_params=pltpu.CompilerParams(
            dimension_semantics=("arbitrary", "arbitrary"), vmem_limit_bytes=VMEM_LIMIT),
        name="ffn_latent" if is_grid else "ffn_prompt",
    )(x2d, mod, g2, wup3, wc3, bc3, wdn3, gfin)


def _pad_lanes(v):
    return jnp.pad(v.reshape(1, -1), ((0, 0), (0, LANES - v.size)))


def kernel(x_prompt, x_sample, state_ssd_fwd, state_ssd_bwd, c, c_ctx, g_norm1, g_norm2, w_ada, b_ada, w_in, w_conv_short, w_conv_ssd, b_conv_ssd, dt_bias, a_log, d_skip, g_ssd_norm, w_out, w_up, w_ffn_conv, b_ffn_conv, w_down, g_final):
    depth = w_in.shape[0]
    assert depth == 1, "kernel is written for the single-layer problem"
    bp, seq = x_prompt.shape[0], x_prompt.shape[1]
    bd, dec_seq = x_sample.shape[0], x_sample.shape[1]
    assert seq == 256 and dec_seq == SLAB and (bp * seq) % SLAB == 0

    n_c = bd + 1
    rows = -(-n_c // 8) * 8
    cvec = jnp.concatenate([c, c_ctx[None], jnp.zeros((rows - n_c, D_MODEL), F32)], axis=0)
    mod = _mod_call(cvec, w_ada[0], b_ada[0].reshape(1, -1)).reshape(rows, N_MOD, D_MODEL)
    mod_lat = mod[:bd]
    mod_ctx = mod[bd:bd + 1]

    w_in_main = _cast_bf16(w_in[0], DT_COL)
    w_dt = jnp.pad(w_in[0][:, DT_COL:], ((0, 0), (0, LANES - 2 * SSD_HEADS))).astype(BF16)
    idx = jnp.arange(CHUNK)
    ltri = (idx[:, None] >= idx[None, :]).astype(BF16)
    utri = (idx[:, None] <= idx[None, :]).astype(BF16)
    tri3 = jnp.concatenate([jnp.concatenate([ltri] * 3, axis=1), jnp.concatenate([utri] * 3, axis=1)], axis=0)
    expand = (jnp.arange(LANES)[:, None] == (jnp.arange(2 * SSD_WIDTH)[None, :] // SSD_HEAD_DIM)).astype(BF16)
    mixer_consts = (
        g_norm1[0].reshape(1, -1), w_in_main, w_dt, w_conv_short[0], w_conv_ssd[0], b_conv_ssd[0].reshape(1, -1),
        _pad_lanes(dt_bias[0]), _pad_lanes(a_log[0]),
        jnp.repeat(d_skip[0], SSD_HEAD_DIM).reshape(1, -1), g_ssd_norm[0].reshape(1, -1),
        _cast_bf16(w_out[0], D_MODEL), tri3, expand, jnp.concatenate([expand] * 3, axis=0),
    )
    wup3 = _up_blocks(w_up[0])
    pad_blocks = ((0, N_FF_PADDED - N_FF_BLOCKS), (0, 0), (0, 0))
    wc3 = jnp.pad(w_ffn_conv[0].reshape(9, 2, N_FF_BLOCKS, FF_BLOCK).transpose(2, 0, 1, 3)
                  .reshape(N_FF_BLOCKS, 9, 2 * FF_BLOCK), pad_blocks)
    bc3 = jnp.pad(b_ffn_conv[0].reshape(2, N_FF_BLOCKS, FF_BLOCK).transpose(1, 0, 2)
                  .reshape(N_FF_BLOCKS, 1, 2 * FF_BLOCK), pad_blocks)
    wdn3 = _cast_bf16(w_down[0], D_MODEL, N_FF_PADDED - N_FF_BLOCKS).reshape(N_FF_PADDED, FF_BLOCK, D_MODEL)
    ffn_consts = (g_norm2[0].reshape(1, -1), wup3, wc3, bc3, wdn3, g_final.reshape(1, -1))

    state_rows = SSD_HEADS * SSD_HEAD_DIM
    init_f = state_ssd_fwd[:, 0].reshape(bd, state_rows, D_STATE)
    init_b = state_ssd_bwd[:, 0].reshape(bd, state_rows, D_STATE)

    xp2d = x_prompt.reshape(bp * seq, D_MODEL)
    xs2d = x_sample.reshape(bd * dec_seq, D_MODEL)

    xp1, s_f, s_b = _mixer_call(xp2d, mod_ctx, False, seq, None, None, mixer_consts)
    (xs1,) = _mixer_call(xs2d, mod_lat, True, dec_seq, init_f, init_b, mixer_consts)

    y_prompt = _ffn_call(xp1, mod_ctx, False, False, ffn_consts).reshape(x_prompt.shape)
    y_sample = _ffn_call(xs1, mod_lat, True, True, ffn_consts).reshape(x_sample.shape)

    state_shape = (bp, depth, SSD_HEADS, SSD_HEAD_DIM, D_STATE)
    return (y_prompt, y_sample, s_f.reshape(state_shape), s_b.reshape(state_shape))
```

```python
import functools

import jax
import jax.numpy as jnp
from jax import lax
from jax.experimental import pallas as pl
from jax.experimental.pallas import tpu as pltpu

F32 = jnp.float32
BF16 = jnp.bfloat16

D_MODEL = 1024
GRID_W = 64
CONV_WIDTH = 512
SSD_WIDTH = 512
SSD_HEAD_DIM = 64
SSD_HEADS = 8
SSD_GROUPS = 2
D_STATE = 128
CHUNK = 128
D_FF = 2816
N_MOD = 6
EPS = 1e-6

SLAB = 1024
N_CHUNKS = SLAB // CHUNK
CHUNK_GROUP = 8
ROW_BLOCK = 256
PAD = 8
FF_BLOCK = 256
N_FF_BLOCKS = D_FF // FF_BLOCK
FF_BLOCKS_PER_STEP = 3
N_FF_STEPS = -(-N_FF_BLOCKS // FF_BLOCKS_PER_STEP)
N_FF_PADDED = N_FF_STEPS * FF_BLOCKS_PER_STEP
LANES = 128
DT_COL = 3072
VMEM_LIMIT = 58 * 1024 * 1024


def _silu(v):
    return v / (1.0 + jnp.exp(-v))


def _softplus(v):
    return jnp.maximum(v, 0.0) + jnp.log1p(jnp.exp(-jnp.abs(v)))


def _split3(v):
    hi = v.astype(BF16)
    r1 = v - hi.astype(F32)
    mid = r1.astype(BF16)
    lo = (r1 - mid.astype(F32)).astype(BF16)
    return hi, mid, lo


def _dot(a, b):
    return jnp.dot(a, b, preferred_element_type=F32)


def _dot3(m3, v):
    return _dot(m3, jnp.concatenate(_split3(v), axis=0))


def _expand3(v, expand3):
    return _dot(jnp.concatenate(_split3(v), axis=1), expand3)


def _conv3_rows(win, w_ref):
    n = win.shape[0]
    prev = pltpu.roll(win, 1, 0)[PAD:PAD + CHUNK]
    cur = win[PAD:PAD + CHUNK]
    nxt = pltpu.roll(win, n - 1, 0)[PAD:PAD + CHUNK]
    return prev * w_ref[0:1, :] + cur * w_ref[1:2, :] + nxt * w_ref[2:3, :]


def _mod_kernel(c_ref, w_ref, b_ref, o_ref):
    s = _silu(c_ref[...]).astype(BF16)
    o_ref[...] = _dot(s, w_ref[...].astype(BF16)) + b_ref[...]


def _mod_call(cvec, w_ada, b_ada):
    rows = cvec.shape[0]
    return pl.pallas_call(
        _mod_kernel,
        grid=(N_MOD,),
        in_specs=[
            pl.BlockSpec((rows, D_MODEL), lambda j: (0, 0)),
            pl.BlockSpec((D_MODEL, D_MODEL), lambda j: (0, j)),
            pl.BlockSpec((1, D_MODEL), lambda j: (0, j)),
        ],
        out_specs=pl.BlockSpec((rows, D_MODEL), lambda j: (0, j)),
        out_shape=jax.ShapeDtypeStruct((rows, N_MOD * D_MODEL), F32),
        compiler_params=pltpu.CompilerParams(dimension_semantics=("arbitrary",)),
        name="mod_vectors",
    )(cvec, w_ada, b_ada)


def _cast_kernel(n_src_blocks, src_ref, dst_ref):
    i = pl.program_id(0)

    @pl.when(i < n_src_blocks)
    def _():
        dst_ref[...] = src_ref[...].astype(BF16)

    @pl.when(i >= n_src_blocks)
    def _():
        dst_ref[...] = jnp.zeros_like(dst_ref)


def _cast_bf16(w, cols, pad_row_blocks=0):
    n_src = w.shape[0] // ROW_BLOCK
    n_dst = n_src + pad_row_blocks
    return pl.pallas_call(
        functools.partial(_cast_kernel, n_src),
        grid=(n_dst,),
        in_specs=[pl.BlockSpec((ROW_BLOCK, cols), lambda i: (jnp.minimum(i, n_src - 1), 0))],
        out_specs=pl.BlockSpec((ROW_BLOCK, cols), lambda i: (i, 0)),
        out_shape=jax.ShapeDtypeStruct((n_dst * ROW_BLOCK, cols), BF16),
        compiler_params=pltpu.CompilerParams(dimension_semantics=("arbitrary",)),
        name="cast_bf16",
    )(w)


def _up_blocks_kernel(gate_ref, value_ref, dst_ref):
    j = pl.program_id(0)

    @pl.when(j < N_FF_BLOCKS)
    def _():
        dst_ref[0, :, 0:FF_BLOCK] = gate_ref[...].astype(BF16)
        dst_ref[0, :, FF_BLOCK:] = value_ref[...].astype(BF16)

    @pl.when(j >= N_FF_BLOCKS)
    def _():
        dst_ref[...] = jnp.zeros_like(dst_ref)


def _up_blocks(w_up):
    last = N_FF_BLOCKS - 1
    return pl.pallas_call(
        _up_blocks_kernel,
        grid=(N_FF_PADDED,),
        in_specs=[
            pl.BlockSpec((D_MODEL, FF_BLOCK), lambda j: (0, jnp.minimum(j, last))),
            pl.BlockSpec((D_MODEL, FF_BLOCK), lambda j: (0, N_FF_BLOCKS + jnp.minimum(j, last))),
        ],
        out_specs=pl.BlockSpec((1, D_MODEL, 2 * FF_BLOCK), lambda j: (j, 0, 0)),
        out_shape=jax.ShapeDtypeStruct((N_FF_PADDED, D_MODEL, 2 * FF_BLOCK), BF16),
        compiler_params=pltpu.CompilerParams(dimension_semantics=("arbitrary",)),
        name="up_blocks",
    )(w_up, w_up)


def _mixer_kernel(seg_chunks, has_init, *refs):
    refs = list(refs)
    x_ref, mod_ref = refs[:2]
    refs = refs[2:]
    if has_init:
        initf_ref, initb_ref = refs[:2]
        refs = refs[2:]
    (g1_ref, win_ref, wdt_ref, wcs_ref, wcx_ref, bcx_ref, dtb_ref, alog_ref, dskip_ref, gssd_ref, wout_ref,
     tri3_ref, exp_ref, exp3_ref) = refs[:14]
    refs = refs[14:]
    out_ref = refs[0]
    refs = refs[1:]
    if not has_init:
        sfo_ref, sbo_ref = refs[:2]
        refs = refs[2:]
    (gchc_ref, xbc_ref, gb_ref, z_ref, dt_ref, e_ref, xs_ref, bc_ref, ymix_ref, sfin_ref, csb_ref,
     sf_ref, sb_ref) = refs

    shift_m = mod_ref[0, 0:1, :]
    scale_m = mod_ref[0, 1:2, :]
    gate_m = mod_ref[0, 2:3, :]

    seg_len = seg_chunks * CHUNK
    zpad_a = jnp.zeros((PAD, CONV_WIDTH), F32)
    zpad_x = jnp.zeros((PAD, 2 * SSD_WIDTH), F32)
    for s in range(SLAB // seg_len + 1):
        gap = slice(s * (seg_len + PAD), s * (seg_len + PAD) + PAD)
        gchc_ref[gap, :] = zpad_a
        xbc_ref[gap, :] = zpad_x

    def staged_row(r):
        return pl.multiple_of(r + lax.div(r, seg_len) * PAD, PAD)

    def proj_body(i, carry):
        r0 = pl.multiple_of(i * ROW_BLOCK, ROW_BLOCK)
        rp = pl.multiple_of(staged_row(i * ROW_BLOCK) + PAD, PAD)
        xb = x_ref[pl.ds(r0, ROW_BLOCK), :]
        ms = jnp.mean(xb * xb, axis=-1, keepdims=True)
        h = (xb * lax.rsqrt(ms + EPS)) * g1_ref[...] * (1.0 + scale_m) + shift_m
        hb = h.astype(BF16)

        def proj(lo, hi):
            return _dot(hb, win_ref[:, lo:hi])

        gchc_ref[pl.ds(rp, ROW_BLOCK), :] = proj(0, 512) * proj(1024, 1536)
        gb_ref[pl.ds(r0, ROW_BLOCK), :] = proj(512, 1024)
        z_ref[pl.ds(r0, ROW_BLOCK), :] = proj(1536, 2048)
        xbc_ref[pl.ds(rp, ROW_BLOCK), :] = proj(2048, 3072)
        dt_ref[pl.ds(r0, ROW_BLOCK), :] = _softplus(_dot(hb, wdt_ref[...]) + dtb_ref[...])
        return carry

    lax.fori_loop(0, SLAB // ROW_BLOCK, proj_body, 0)

    a_row = -jnp.exp(alog_ref[...])
    lane_t = lax.broadcasted_iota(jnp.int32, (CHUNK, LANES), 1)
    lane_1 = lax.broadcasted_iota(jnp.int32, (1, LANES), 1)
    is_fwd_t = lane_t < SSD_HEADS
    is_fwd_1 = lane_1 < SSD_HEADS

    def chunk_decay_row(e):
        tot = jnp.where(is_fwd_1, e[CHUNK - 1:CHUNK, :], e[0:1, :])
        dec = jnp.broadcast_to(jnp.exp(tot), (8, LANES))
        return tot, _expand3(dec, exp3_ref[...])[0:1, :]

    pair_is_sequence = seg_chunks == 2
    assert pair_is_sequence != has_init and seg_chunks in (2, N_CHUNKS)
    n_groups = N_CHUNKS // CHUNK_GROUP

    def fwd_convs(c):
        r0 = pl.multiple_of(c * CHUNK, CHUNK)
        rows = pl.ds(r0, CHUNK)
        w0 = staged_row(c * CHUNK)
        xc = _conv3_rows(xbc_ref[pl.ds(w0, CHUNK + 2 * PAD), :], wcx_ref)
        xc = _silu(xc + bcx_ref[...])
        xs_ref[rows, :] = xc[:, 0:SSD_WIDTH]
        bc_ref[rows, :] = xc[:, SSD_WIDTH:].astype(BF16)
        ca = _conv3_rows(gchc_ref[pl.ds(w0, CHUNK + 2 * PAD), :], wcs_ref)
        ymix_ref[rows, 0:CONV_WIDTH] = (gb_ref[rows, :] * ca).astype(BF16)
        return xc

    def fwd_decays(c):
        rows = pl.ds(pl.multiple_of(c * CHUNK, CHUNK), CHUNK)
        dtt = dt_ref[rows, :]
        da = dtt * a_row
        sums = _dot3(tri3_ref[...], da)
        e = jnp.where(is_fwd_t, sums[0:CHUNK], sums[CHUNK:])
        e_ref[rows, :] = e
        tot, dec_row = chunk_decay_row(e)
        wst = (dtt * jnp.exp(tot - e)).astype(BF16)
        return _dot(wst, exp_ref[...]), dec_row

    def fwd_chunk_states(xc, wexp):
        xs = xc[:, 0:SSD_WIDTH]
        xst_f = (xs * wexp[:, 0:SSD_WIDTH]).astype(BF16)
        xst_b = (xs * wexp[:, SSD_WIDTH:]).astype(BF16)
        cst_f, cst_b = [], []
        for g in range(SSD_GROUPS):
            bt = jnp.transpose(xc[:, SSD_WIDTH + g * D_STATE:SSD_WIDTH + (g + 1) * D_STATE]).astype(BF16)
            cst_f.append(_dot(bt, xst_f[:, g * 256:(g + 1) * 256]))
            cst_b.append(_dot(bt, xst_b[:, g * 256:(g + 1) * 256]))
        return jnp.concatenate(cst_f, axis=1), jnp.concatenate(cst_b, axis=1)

    if has_init:
        sf_ref[...] = jnp.transpose(initf_ref[0])
        sb_ref[...] = jnp.transpose(initb_ref[0])

    def fwd_group(p, carry):
        chunks = [CHUNK_GROUP * p + i for i in range(CHUNK_GROUP)]
        xcs = [fwd_convs(c) for c in chunks]
        decays = [fwd_decays(c) for c in chunks]
        states = [fwd_chunk_states(xc, wexp) for xc, (wexp, _) in zip(xcs, decays)]
        state = None if pair_is_sequence else sf_ref[...]
        for i, c in enumerate(chunks):
            cst_f, cst_b = states[i]
            csb_ref[c] = cst_b
            s_in = None if (pair_is_sequence and i % 2 == 0) else state
            if s_in is None:
                state = cst_f
            else:
                sfin_ref[c] = s_in
                state = s_in * decays[i][1][:, 0:SSD_WIDTH] + cst_f
            if pair_is_sequence and i % 2 == 1:
                sfo_ref[p * (CHUNK_GROUP // 2) + i // 2] = jnp.transpose(state)
        if not pair_is_sequence:
            sf_ref[...] = state
        return carry

    lax.fori_loop(0, n_groups, fwd_group, 0)

    row_t = lax.broadcasted_iota(jnp.int32, (CHUNK, CHUNK), 0)
    col_t = lax.broadcasted_iota(jnp.int32, (CHUNK, CHUNK), 1)
    causal = row_t >= col_t
    anti = col_t >= row_t
    low_half = lane_t < SSD_HEAD_DIM

    def chunk_tables(c):
        rows = pl.ds(pl.multiple_of(c * CHUNK, CHUNK), CHUNK)
        e = e_ref[rows, :]
        xs = xs_ref[rows, :]
        dtexp = _dot(dt_ref[rows, :].astype(BF16), exp_ref[...])
        ecs = _dot(jnp.exp(e).astype(BF16), exp_ref[...])
        return dict(rows=rows, e=e, et=jnp.transpose(e), xs=xs, bc=bc_ref[rows, :], ecs=ecs,
                    xdt_f=xs * dtexp[:, 0:SSD_WIDTH], xdt_b=xs * dtexp[:, SSD_WIDTH:])

    def chunk_output(t, s_f, s_b):
        e, et, xs, bc, ecs, xdt_f, xdt_b = t["e"], t["et"], t["xs"], t["bc"], t["ecs"], t["xdt_f"], t["xdt_b"]
        y_parts = []
        for g in range(SSD_GROUPS):
            b_g = bc[:, g * D_STATE:(g + 1) * D_STATE]
            c_g = bc[:, 256 + g * D_STATE:256 + (g + 1) * D_STATE]
            gmat = lax.dot_general(c_g, b_g, (((1,), (1,)), ((), ())), preferred_element_type=F32)
            y_off = jnp.zeros((CHUNK, 256), F32)
            if s_f is not None:
                y_off = y_off + _dot(c_g, s_f[:, g * 256:(g + 1) * 256].astype(BF16)) \
                    * ecs[:, g * 256:(g + 1) * 256]
            if s_b is not None:
                y_off = y_off + _dot(c_g, s_b[:, g * 256:(g + 1) * 256].astype(BF16)) \
                    * ecs[:, SSD_WIDTH + g * 256:SSD_WIDTH + (g + 1) * 256]
            for pair in range(2):
                p0 = g * 256 + pair * LANES
                xf_pair = xdt_f[:, p0:p0 + LANES]
                xb_pair = xdt_b[:, p0:p0 + LANES]
                scores, rhs = [], []
                for sub in range(2):
                    head = g * 4 + pair * 2 + sub
                    keep = low_half if sub == 0 else jnp.logical_not(low_half)
                    rhs.append(jnp.where(keep, xf_pair, 0.0).astype(BF16))
                    rhs.append(jnp.where(keep, xb_pair, 0.0).astype(BF16))
                    jf, jb = head, SSD_HEADS + head
                    d_f = e[:, jf:jf + 1] - et[jf:jf + 1, :]
                    d_b = e[:, jb:jb + 1] - et[jb:jb + 1, :]
                    scores.append((gmat * jnp.where(causal, jnp.exp(d_f), 0.0)).astype(BF16))
                    scores.append((gmat * jnp.where(anti, jnp.exp(d_b), 0.0)).astype(BF16))
                y_diag = _dot(jnp.concatenate(scores, axis=1), jnp.concatenate(rhs, axis=0))
                y_parts.append(y_off[:, pair * LANES:(pair + 1) * LANES] + y_diag)
        y = jnp.concatenate(y_parts, axis=1) + xs * dskip_ref[...]
        yz = y * _silu(z_ref[t["rows"], :])
        ms = jnp.mean(yz * yz, axis=-1, keepdims=True)
        yn = (yz * lax.rsqrt(ms + EPS)) * gssd_ref[...]
        ymix_ref[t["rows"], CONV_WIDTH:] = yn.astype(BF16)

    def bwd_group(k, carry):
        q = n_groups - 1 - k
        state = None if pair_is_sequence else sb_ref[...]
        for i in reversed(range(CHUNK_GROUP)):
            c = CHUNK_GROUP * q + i
            starts_sequence = pair_is_sequence and i % 2 == 0
            s_b = None if (pair_is_sequence and i % 2 == 1) else state
            t = chunk_tables(c)
            chunk_output(t, None if starts_sequence else sfin_ref[c], s_b)
            if s_b is None:
                state = csb_ref[c]
            else:
                state = s_b * chunk_decay_row(t["e"])[1][:, SSD_WIDTH:] + csb_ref[c]
            if starts_sequence:
                sbo_ref[q * (CHUNK_GROUP // 2) + i // 2] = jnp.transpose(state)
        if not pair_is_sequence:
            sb_ref[...] = state
        return carry

    lax.fori_loop(0, n_groups, bwd_group, 0)

    def out_body(i, carry):
        r0 = pl.multiple_of(i * ROW_BLOCK, ROW_BLOCK)
        rows = pl.ds(r0, ROW_BLOCK)
        o = _dot(ymix_ref[rows, :], wout_ref[...])
        out_ref[rows, :] = x_ref[rows, :] + gate_m * o
        return carry

    lax.fori_loop(0, SLAB // ROW_BLOCK, out_body, 0)


def _const_spec(shape):
    nd = len(shape)
    return pl.BlockSpec(shape, lambda i: (0,) * nd)


def _mixer_call(x2d, mod, mod_per_slab, seg_len, init_f, init_b, consts):
    n_slabs = x2d.shape[0] // SLAB
    seg_chunks = seg_len // CHUNK
    segs_per_slab = SLAB // seg_len
    has_init = init_f is not None
    state_rows = SSD_HEADS * SSD_HEAD_DIM
    staged_rows = SLAB + PAD * (1 + segs_per_slab)

    mod_map = (lambda i: (i, 0, 0)) if mod_per_slab else (lambda i: (0, 0, 0))
    in_specs = [
        pl.BlockSpec((SLAB, D_MODEL), lambda i: (i, 0)),
        pl.BlockSpec((1, N_MOD, D_MODEL), mod_map),
    ]
    args = [x2d, mod]
    if has_init:
        in_specs += [pl.BlockSpec((1, state_rows, D_STATE), lambda i: (i, 0, 0))] * 2
        args += [init_f, init_b]
    in_specs += [_const_spec(a.shape) for a in consts]
    args += list(consts)

    out_specs = [pl.BlockSpec((SLAB, D_MODEL), lambda i: (i, 0))]
    out_shape = [jax.ShapeDtypeStruct(x2d.shape, F32)]
    if not has_init:
        n_seq = n_slabs * segs_per_slab
        out_specs += [pl.BlockSpec((segs_per_slab, state_rows, D_STATE), lambda i: (i, 0, 0))] * 2
        out_shape += [jax.ShapeDtypeStruct((n_seq, state_rows, D_STATE), F32)] * 2

    scratch = [
        pltpu.VMEM((staged_rows, CONV_WIDTH), F32),
        pltpu.VMEM((staged_rows, 2 * SSD_WIDTH), F32),
        pltpu.VMEM((SLAB, CONV_WIDTH), F32),
        pltpu.VMEM((SLAB, SSD_WIDTH), F32),
        pltpu.VMEM((SLAB, LANES), F32),
        pltpu.VMEM((SLAB, LANES), F32),
        pltpu.VMEM((SLAB, SSD_WIDTH), F32),
        pltpu.VMEM((SLAB, 2 * SSD_GROUPS * D_STATE), BF16),
        pltpu.VMEM((SLAB, D_MODEL), BF16),
        pltpu.VMEM((N_CHUNKS, D_STATE, SSD_WIDTH), F32),
        pltpu.VMEM((N_CHUNKS, D_STATE, SSD_WIDTH), F32),
        pltpu.VMEM((D_STATE, SSD_WIDTH), F32),
        pltpu.VMEM((D_STATE, SSD_WIDTH), F32),
    ]
    return pl.pallas_call(
        functools.partial(_mixer_kernel, seg_chunks, has_init),
        grid=(n_slabs,),
        in_specs=in_specs,
        out_specs=out_specs,
        out_shape=out_shape,
        scratch_shapes=scratch,
        compiler_params=pltpu.CompilerParams(
            dimension_semantics=("arbitrary",), vmem_limit_bytes=VMEM_LIMIT),
        name="mixer_latent" if has_init else "mixer_prompt",
    )(*args)


def _ffn_conv(u, wc_ref, bc_ref, is_grid):
    n = u.shape[0]
    row = lax.broadcasted_iota(jnp.int32, (n, 1), 0)
    period = GRID_W if is_grid else 256
    pos = jnp.bitwise_and(row, period - 1)
    um = jnp.where(pos == 0, 0.0, pltpu.roll(u, 1, 0)).astype(BF16)
    up = jnp.where(pos == period - 1, 0.0, pltpu.roll(u, n - 1, 0)).astype(BF16)
    u = u.astype(BF16)
    wc = wc_ref[...].astype(BF16)

    def taps(di):
        return um * wc[3 * di:3 * di + 1, :] + u * wc[3 * di + 1:3 * di + 2, :] + up * wc[3 * di + 2:3 * di + 3, :]

    out = taps(1) + bc_ref[...].astype(BF16)
    if is_grid:
        zeros = jnp.zeros((GRID_W, u.shape[1]), BF16)
        out = out + jnp.concatenate([zeros, taps(0)[:n - GRID_W]], axis=0)
        out = out + jnp.concatenate([taps(2)[GRID_W:], zeros], axis=0)
    return out.astype(F32)


def _ffn_kernel(is_grid, x_ref, mod_ref, g2_ref, wup_ref, wc_ref, bc_ref, wdn_ref, gfin_ref, out_ref,
                h2_ref, acc_ref):
    j = pl.program_id(1)

    @pl.when(j == 0)
    def _():
        shift_f = mod_ref[0, 3:4, :]
        scale_f = mod_ref[0, 4:5, :]
        x = x_ref[...]
        ms = jnp.mean(x * x, axis=-1, keepdims=True)
        h2 = (x * lax.rsqrt(ms + EPS)) * g2_ref[...] * (1.0 + scale_f) + shift_f
        h2_ref[...] = h2.astype(BF16)
        acc_ref[...] = jnp.zeros_like(acc_ref)

    def up_proj(b):
        return _dot(h2_ref[...], wup_ref[b])

    def run_blocks(n_blocks):
        u_next = up_proj(0)
        for b in range(n_blocks):
            u = u_next
            if b + 1 < n_blocks:
                u_next = up_proj(b + 1)
            u = _ffn_conv(u, wc_ref.at[b], bc_ref.at[b], is_grid)
            a = (_silu(u[:, 0:FF_BLOCK]) * u[:, FF_BLOCK:]).astype(BF16)
            acc_ref[...] += _dot(a, wdn_ref[b])

    n_last = N_FF_BLOCKS - (N_FF_STEPS - 1) * FF_BLOCKS_PER_STEP

    @pl.when(j < N_FF_STEPS - 1)
    def _():
        run_blocks(FF_BLOCKS_PER_STEP)

    @pl.when(j == N_FF_STEPS - 1)
    def _():
        run_blocks(n_last)

    @pl.when(j == N_FF_STEPS - 1)
    def _():
        gate_f = mod_ref[0, 5:6, :]
        x2 = x_ref[...] + gate_f * acc_ref[...]
        ms = jnp.mean(x2 * x2, axis=-1, keepdims=True)
        out_ref[...] = (x2 * lax.rsqrt(ms + EPS)) * gfin_ref[...]


def _ffn_call(x2d, mod, mod_per_slab, is_grid, consts):
    n_slabs = x2d.shape[0] // SLAB
    g2, wup3, wc3, bc3, wdn3, gfin = consts
    mod_map = (lambda i, j: (i, 0, 0)) if mod_per_slab else (lambda i, j: (0, 0, 0))
    return pl.pallas_call(
        functools.partial(_ffn_kernel, is_grid),
        grid=(n_slabs, N_FF_STEPS),
        in_specs=[
            pl.BlockSpec((SLAB, D_MODEL), lambda i, j: (i, 0)),
            pl.BlockSpec((1, N_MOD, D_MODEL), mod_map),
            pl.BlockSpec((1, D_MODEL), lambda i, j: (0, 0)),
            pl.BlockSpec((FF_BLOCKS_PER_STEP, D_MODEL, 2 * FF_BLOCK), lambda i, j: (j, 0, 0)),
            pl.BlockSpec((FF_BLOCKS_PER_STEP, 9, 2 * FF_BLOCK), lambda i, j: (j, 0, 0)),
            pl.BlockSpec((FF_BLOCKS_PER_STEP, 1, 2 * FF_BLOCK), lambda i, j: (j, 0, 0)),
            pl.BlockSpec((FF_BLOCKS_PER_STEP, FF_BLOCK, D_MODEL), lambda i, j: (j, 0, 0)),
            pl.BlockSpec((1, D_MODEL), lambda i, j: (0, 0)),
        ],
        out_specs=pl.BlockSpec((SLAB, D_MODEL), lambda i, j: (i, 0)),
        out_shape=jax.ShapeDtypeStruct(x2d.shape, F32),
        scratch_shapes=[
            pltpu.VMEM((SLAB, D_MODEL), BF16),
            pltpu.VMEM((SLAB, D_MODEL), F32),
        ],
        compiler_params=pltpu.CompilerParams(
            dimension_semantics=("arbitrary", "arbitrary"), vmem_limit_bytes=VMEM_LIMIT),
        name="ffn_latent" if is_grid else "ffn_prompt",
    )(x2d, mod, g2, wup3, wc3, bc3, wdn3, gfin)


def _pad_lanes(v):
    return jnp.pad(v.reshape(1, -1), ((0, 0), (0, LANES - v.size)))


def kernel(x_prompt, x_sample, state_ssd_fwd, state_ssd_bwd, c, c_ctx, g_norm1, g_norm2, w_ada, b_ada, w_in, w_conv_short, w_conv_ssd, b_conv_ssd, dt_bias, a_log, d_skip, g_ssd_norm, w_out, w_up, w_ffn_conv, b_ffn_conv, w_down, g_final):
    depth = w_in.shape[0]
    assert depth == 1, "kernel is written for the single-layer problem"
    bp, seq = x_prompt.shape[0], x_prompt.shape[1]
    bd, dec_seq = x_sample.shape[0], x_sample.shape[1]
    assert seq == 256 and dec_seq == SLAB and (bp * seq) % SLAB == 0

    n_c = bd + 1
    rows = -(-n_c // 8) * 8
    cvec = jnp.concatenate([c, c_ctx[None], jnp.zeros((rows - n_c, D_MODEL), F32)], axis=0)
    mod = _mod_call(cvec, w_ada[0], b_ada[0].reshape(1, -1)).reshape(rows, N_MOD, D_MODEL)
    mod_lat = mod[:bd]
    mod_ctx = mod[bd:bd + 1]

    w_in_main = _cast_bf16(w_in[0], DT_COL)
    w_dt = jnp.pad(w_in[0][:, DT_COL:], ((0, 0), (0, LANES - 2 * SSD_HEADS))).astype(BF16)
    idx = jnp.arange(CHUNK)
    ltri = (idx[:, None] >= idx[None, :]).astype(BF16)
    utri = (idx[:, None] <= idx[None, :]).astype(BF16)
    tri3 = jnp.concatenate([jnp.concatenate([ltri] * 3, axis=1), jnp.concatenate([utri] * 3, axis=1)], axis=0)
    expand = (jnp.arange(LANES)[:, None] == (jnp.arange(2 * SSD_WIDTH)[None, :] // SSD_HEAD_DIM)).astype(BF16)
    mixer_consts = (
        g_norm1[0].reshape(1, -1), w_in_main, w_dt, w_conv_short[0], w_conv_ssd[0], b_conv_ssd[0].reshape(1, -1),
        _pad_lanes(dt_bias[0]), _pad_lanes(a_log[0]),
        jnp.repeat(d_skip[0], SSD_HEAD_DIM).reshape(1, -1), g_ssd_norm[0].reshape(1, -1),
        _cast_bf16(w_out[0], D_MODEL), tri3, expand, jnp.concatenate([expand] * 3, axis=0),
    )
    wup3 = _up_blocks(w_up[0])
    pad_blocks = ((0, N_FF_PADDED - N_FF_BLOCKS), (0, 0), (0, 0))
    wc3 = jnp.pad(w_ffn_conv[0].reshape(9, 2, N_FF_BLOCKS, FF_BLOCK).transpose(2, 0, 1, 3)
                  .reshape(N_FF_BLOCKS, 9, 2 * FF_BLOCK), pad_blocks)
    bc3 = jnp.pad(b_ffn_conv[0].reshape(2, N_FF_BLOCKS, FF_BLOCK).transpose(1, 0, 2)
                  .reshape(N_FF_BLOCKS, 1, 2 * FF_BLOCK), pad_blocks)
    wdn3 = _cast_bf16(w_down[0], D_MODEL, N_FF_PADDED - N_FF_BLOCKS).reshape(N_FF_PADDED, FF_BLOCK, D_MODEL)
    ffn_consts = (g_norm2[0].reshape(1, -1), wup3, wc3, bc3, wdn3, g_final.reshape(1, -1))

    state_rows = SSD_HEADS * SSD_HEAD_DIM
    init_f = state_ssd_fwd[:, 0].reshape(bd, state_rows, D_STATE)
    init_b = state_ssd_bwd[:, 0].reshape(bd, state_rows, D_STATE)

    xp2d = x_prompt.reshape(bp * seq, D_MODEL)
    xs2d = x_sample.reshape(bd * dec_seq, D_MODEL)

    xp1, s_f, s_b = _mixer_call(xp2d, mod_ctx, False, seq, None, None, mixer_consts)
    (xs1,) = _mixer_call(xs2d, mod_lat, True, dec_seq, init_f, init_b, mixer_consts)

    y_prompt = _ffn_call(xp1, mod_ctx, False, False, ffn_consts).reshape(x_prompt.shape)
    y_sample = _ffn_call(xs1, mod_lat, True, True, ffn_consts).reshape(x_sample.shape)

    state_shape = (bp, depth, SSD_HEADS, SSD_HEAD_DIM, D_STATE)
    return (y_prompt, y_sample, s_f.reshape(state_shape), s_b.reshape(state_shape))
```

```python
import functools

import jax
import jax.numpy as jnp
from jax import lax
from jax.experimental import pallas as pl
from jax.experimental.pallas import tpu as pltpu

F32 = jnp.float32
BF16 = jnp.bfloat16

D_MODEL = 1024
GRID_W = 64
CONV_WIDTH = 512
SSD_WIDTH = 512
SSD_HEAD_DIM = 64
SSD_HEADS = 8
SSD_GROUPS = 2
D_STATE = 128
CHUNK = 128
D_FF = 2816
N_MOD = 6
EPS = 1e-6

SLAB = 1024
N_CHUNKS = SLAB // CHUNK
ROW_BLOCK = 256
PAD = 8
FF_BLOCK = 256
N_FF_BLOCKS = D_FF // FF_BLOCK
FF_BLOCKS_PER_STEP = 3
N_FF_STEPS = -(-N_FF_BLOCKS // FF_BLOCKS_PER_STEP)
N_FF_PADDED = N_FF_STEPS * FF_BLOCKS_PER_STEP
LANES = 128
DT_COL = 3072
VMEM_LIMIT = 58 * 1024 * 1024


def _silu(v):
    return v / (1.0 + jnp.exp(-v))


def _softplus(v):
    return jnp.maximum(v, 0.0) + jnp.log1p(jnp.exp(-jnp.abs(v)))


def _split3(v):
    hi = v.astype(BF16)
    r1 = v - hi.astype(F32)
    mid = r1.astype(BF16)
    lo = (r1 - mid.astype(F32)).astype(BF16)
    return hi, mid, lo


def _dot(a, b):
    return jnp.dot(a, b, preferred_element_type=F32)


def _dot3(m3, v):
    return _dot(m3, jnp.concatenate(_split3(v), axis=0))


def _expand3(v, expand3):
    return _dot(jnp.concatenate(_split3(v), axis=1), expand3)


def _conv3_rows(win, w_ref):
    n = win.shape[0]
    prev = pltpu.roll(win, 1, 0)[PAD:PAD + CHUNK]
    cur = win[PAD:PAD + CHUNK]
    nxt = pltpu.roll(win, n - 1, 0)[PAD:PAD + CHUNK]
    return prev * w_ref[0:1, :] + cur * w_ref[1:2, :] + nxt * w_ref[2:3, :]


def _mod_kernel(c_ref, w_ref, b_ref, o_ref):
    s = _silu(c_ref[...]).astype(BF16)
    o_ref[...] = _dot(s, w_ref[...].astype(BF16)) + b_ref[...]


def _mod_call(cvec, w_ada, b_ada):
    rows = cvec.shape[0]
    return pl.pallas_call(
        _mod_kernel,
        grid=(N_MOD,),
        in_specs=[
            pl.BlockSpec((rows, D_MODEL), lambda j: (0, 0)),
            pl.BlockSpec((D_MODEL, D_MODEL), lambda j: (0, j)),
            pl.BlockSpec((1, D_MODEL), lambda j: (0, j)),
        ],
        out_specs=pl.BlockSpec((rows, D_MODEL), lambda j: (0, j)),
        out_shape=jax.ShapeDtypeStruct((rows, N_MOD * D_MODEL), F32),
        compiler_params=pltpu.CompilerParams(dimension_semantics=("arbitrary",)),
        name="mod_vectors",
    )(cvec, w_ada, b_ada)


def _cast_kernel(n_src_blocks, src_ref, dst_ref):
    i = pl.program_id(0)

    @pl.when(i < n_src_blocks)
    def _():
        dst_ref[...] = src_ref[...].astype(BF16)

    @pl.when(i >= n_src_blocks)
    def _():
        dst_ref[...] = jnp.zeros_like(dst_ref)


def _cast_bf16(w, cols, pad_row_blocks=0):
    n_src = w.shape[0] // ROW_BLOCK
    n_dst = n_src + pad_row_blocks
    return pl.pallas_call(
        functools.partial(_cast_kernel, n_src),
        grid=(n_dst,),
        in_specs=[pl.BlockSpec((ROW_BLOCK, cols), lambda i: (jnp.minimum(i, n_src - 1), 0))],
        out_specs=pl.BlockSpec((ROW_BLOCK, cols), lambda i: (i, 0)),
        out_shape=jax.ShapeDtypeStruct((n_dst * ROW_BLOCK, cols), BF16),
        compiler_params=pltpu.CompilerParams(dimension_semantics=("arbitrary",)),
        name="cast_bf16",
    )(w)


def _up_blocks_kernel(gate_ref, value_ref, dst_ref):
    j = pl.program_id(0)

    @pl.when(j < N_FF_BLOCKS)
    def _():
        dst_ref[0, :, 0:FF_BLOCK] = gate_ref[...].astype(BF16)
        dst_ref[0, :, FF_BLOCK:] = value_ref[...].astype(BF16)

    @pl.when(j >= N_FF_BLOCKS)
    def _():
        dst_ref[...] = jnp.zeros_like(dst_ref)


def _up_blocks(w_up):
    last = N_FF_BLOCKS - 1
    return pl.pallas_call(
        _up_blocks_kernel,
        grid=(N_FF_PADDED,),
        in_specs=[
            pl.BlockSpec((D_MODEL, FF_BLOCK), lambda j: (0, jnp.minimum(j, last))),
            pl.BlockSpec((D_MODEL, FF_BLOCK), lambda j: (0, N_FF_BLOCKS + jnp.minimum(j, last))),
        ],
        out_specs=pl.BlockSpec((1, D_MODEL, 2 * FF_BLOCK), lambda j: (j, 0, 0)),
        out_shape=jax.ShapeDtypeStruct((N_FF_PADDED, D_MODEL, 2 * FF_BLOCK), BF16),
        compiler_params=pltpu.CompilerParams(dimension_semantics=("arbitrary",)),
        name="up_blocks",
    )(w_up, w_up)


def _mixer_kernel(seg_chunks, has_init, *refs):
    refs = list(refs)
    x_ref, mod_ref = refs[:2]
    refs = refs[2:]
    if has_init:
        initf_ref, initb_ref = refs[:2]
        refs = refs[2:]
    (g1_ref, win_ref, wdt_ref, wcs_ref, wcx_ref, bcx_ref, dtb_ref, alog_ref, dskip_ref, gssd_ref, wout_ref,
     tri3_ref, exp_ref, exp3_ref) = refs[:14]
    refs = refs[14:]
    out_ref = refs[0]
    refs = refs[1:]
    if not has_init:
        sfo_ref, sbo_ref = refs[:2]
        refs = refs[2:]
    (gchc_ref, xbc_ref, gb_ref, z_ref, dt_ref, e_ref, xs_ref, bc_ref, ymix_ref, sfin_ref, csb_ref,
     sf_ref, sb_ref) = refs

    shift_m = mod_ref[0, 0:1, :]
    scale_m = mod_ref[0, 1:2, :]
    gate_m = mod_ref[0, 2:3, :]

    seg_len = seg_chunks * CHUNK
    zpad_a = jnp.zeros((PAD, CONV_WIDTH), F32)
    zpad_x = jnp.zeros((PAD, 2 * SSD_WIDTH), F32)
    for s in range(SLAB // seg_len + 1):
        gap = slice(s * (seg_len + PAD), s * (seg_len + PAD) + PAD)
        gchc_ref[gap, :] = zpad_a
        xbc_ref[gap, :] = zpad_x

    def staged_row(r):
        return r + (r // seg_len) * PAD

    def proj_block(i):
        rows = slice(i * ROW_BLOCK, (i + 1) * ROW_BLOCK)
        rp = staged_row(i * ROW_BLOCK) + PAD
        staged = slice(rp, rp + ROW_BLOCK)
        xb = x_ref[rows, :]
        ms = jnp.mean(xb * xb, axis=-1, keepdims=True)
        h = (xb * lax.rsqrt(ms + EPS)) * g1_ref[...] * (1.0 + scale_m) + shift_m
        hb = h.astype(BF16)

        def proj(lo, hi):
            return _dot(hb, win_ref[:, lo:hi])

        gchc_ref[staged, :] = proj(0, 512) * proj(1024, 1536)
        gb_ref[rows, :] = proj(512, 1024)
        z_ref[rows, :] = proj(1536, 2048)
        xbc_ref[staged, :] = proj(2048, 3072)
        dt_ref[rows, :] = _softplus(_dot(hb, wdt_ref[...]) + dtb_ref[...])

    a_row = -jnp.exp(alog_ref[...])
    lane_t = lax.broadcasted_iota(jnp.int32, (CHUNK, LANES), 1)
    lane_1 = lax.broadcasted_iota(jnp.int32, (1, LANES), 1)
    is_fwd_t = lane_t < SSD_HEADS
    is_fwd_1 = lane_1 < SSD_HEADS

    def chunk_decay_row(e):
        tot = jnp.where(is_fwd_1, e[CHUNK - 1:CHUNK, :], e[0:1, :])
        dec = jnp.broadcast_to(jnp.exp(tot), (8, LANES))
        return tot, _expand3(dec, exp3_ref[...])[0:1, :]

    pair_is_sequence = seg_chunks == 2
    assert pair_is_sequence != has_init and seg_chunks in (2, N_CHUNKS)

    def fwd_convs(c):
        rows = slice(c * CHUNK, (c + 1) * CHUNK)
        w0 = staged_row(c * CHUNK)
        window = slice(w0, w0 + CHUNK + 2 * PAD)
        xc = _conv3_rows(xbc_ref[window, :], wcx_ref)
        xc = _silu(xc + bcx_ref[...])
        xs_ref[rows, :] = xc[:, 0:SSD_WIDTH]
        bc_ref[rows, :] = xc[:, SSD_WIDTH:].astype(BF16)
        ca = _conv3_rows(gchc_ref[window, :], wcs_ref)
        ymix_ref[rows, 0:CONV_WIDTH] = (gb_ref[rows, :] * ca).astype(BF16)
        return xc

    def fwd_decays(c):
        rows = slice(c * CHUNK, (c + 1) * CHUNK)
        dtt = dt_ref[rows, :]
        da = dtt * a_row
        sums = _dot3(tri3_ref[...], da)
        e = jnp.where(is_fwd_t, sums[0:CHUNK], sums[CHUNK:])
        e_ref[rows, :] = e
        tot, dec_row = chunk_decay_row(e)
        wst = (dtt * jnp.exp(tot - e)).astype(BF16)
        return _dot(wst, exp_ref[...]), dec_row

    def fwd_chunk_states(xc, wexp):
        xs = xc[:, 0:SSD_WIDTH]
        xst_f = (xs * wexp[:, 0:SSD_WIDTH]).astype(BF16)
        xst_b = (xs * wexp[:, SSD_WIDTH:]).astype(BF16)
        cst_f, cst_b = [], []
        for g in range(SSD_GROUPS):
            bt = jnp.transpose(xc[:, SSD_WIDTH + g * D_STATE:SSD_WIDTH + (g + 1) * D_STATE]).astype(BF16)
            cst_f.append(_dot(bt, xst_f[:, g * 256:(g + 1) * 256]))
            cst_b.append(_dot(bt, xst_b[:, g * 256:(g + 1) * 256]))
        return jnp.concatenate(cst_f, axis=1), jnp.concatenate(cst_b, axis=1)

    if has_init:
        sf_ref[...] = jnp.transpose(initf_ref[0])
        sb_ref[...] = jnp.transpose(initb_ref[0])

    chunks_per_block = ROW_BLOCK // CHUNK
    xcs = []
    for i in range(SLAB // ROW_BLOCK):
        proj_block(i)
        if i >= 1:
            xcs += [fwd_convs(c) for c in range((i - 1) * chunks_per_block, i * chunks_per_block)]
    xcs += [fwd_convs(c) for c in range(N_CHUNKS - chunks_per_block, N_CHUNKS)]
    decays = [fwd_decays(c) for c in range(N_CHUNKS)]
    states = [fwd_chunk_states(xc, wexp) for xc, (wexp, _) in zip(xcs, decays)]
    state = None if pair_is_sequence else sf_ref[...]
    for c in range(N_CHUNKS):
        cst_f, cst_b = states[c]
        csb_ref[c] = cst_b
        s_in = None if (pair_is_sequence and c % 2 == 0) else state
        if s_in is None:
            state = cst_f
        else:
            sfin_ref[c] = s_in
            state = s_in * decays[c][1][:, 0:SSD_WIDTH] + cst_f
        if pair_is_sequence and c % 2 == 1:
            sfo_ref[c // 2] = jnp.transpose(state)

    row_t = lax.broadcasted_iota(jnp.int32, (CHUNK, CHUNK), 0)
    col_t = lax.broadcasted_iota(jnp.int32, (CHUNK, CHUNK), 1)
    causal = row_t >= col_t
    anti = col_t >= row_t
    low_half = lane_t < SSD_HEAD_DIM

    def chunk_tables(c):
        rows = slice(c * CHUNK, (c + 1) * CHUNK)
        e = e_ref[rows, :]
        xs = xs_ref[rows, :]
        dtexp = _dot(dt_ref[rows, :].astype(BF16), exp_ref[...])
        ecs = _dot(jnp.exp(e).astype(BF16), exp_ref[...])
        return dict(rows=rows, e=e, et=jnp.transpose(e), xs=xs, bc=bc_ref[rows, :], ecs=ecs,
                    xdt_f=xs * dtexp[:, 0:SSD_WIDTH], xdt_b=xs * dtexp[:, SSD_WIDTH:])

    def chunk_output(t, s_f, s_b):
        e, et, xs, bc, ecs, xdt_f, xdt_b = t["e"], t["et"], t["xs"], t["bc"], t["ecs"], t["xdt_f"], t["xdt_b"]
        y_parts = []
        for g in range(SSD_GROUPS):
            b_g = bc[:, g * D_STATE:(g + 1) * D_STATE]
            c_g = bc[:, 256 + g * D_STATE:256 + (g + 1) * D_STATE]
            gmat = lax.dot_general(c_g, b_g, (((1,), (1,)), ((), ())), preferred_element_type=F32)
            y_off = jnp.zeros((CHUNK, 256), F32)
            if s_f is not None:
                y_off = y_off + _dot(c_g, s_f[:, g * 256:(g + 1) * 256].astype(BF16)) \
                    * ecs[:, g * 256:(g + 1) * 256]
            if s_b is not None:
                y_off = y_off + _dot(c_g, s_b[:, g * 256:(g + 1) * 256].astype(BF16)) \
                    * ecs[:, SSD_WIDTH + g * 256:SSD_WIDTH + (g + 1) * 256]
            for pair in range(2):
                p0 = g * 256 + pair * LANES
                xf_pair = xdt_f[:, p0:p0 + LANES]
                xb_pair = xdt_b[:, p0:p0 + LANES]
                scores, rhs = [], []
                for sub in range(2):
                    head = g * 4 + pair * 2 + sub
                    keep = low_half if sub == 0 else jnp.logical_not(low_half)
                    rhs.append(jnp.where(keep, xf_pair, 0.0).astype(BF16))
                    rhs.append(jnp.where(keep, xb_pair, 0.0).astype(BF16))
                    jf, jb = head, SSD_HEADS + head
                    d_f = e[:, jf:jf + 1] - et[jf:jf + 1, :]
                    d_b = e[:, jb:jb + 1] - et[jb:jb + 1, :]
                    scores.append((gmat * jnp.where(causal, jnp.exp(d_f), 0.0)).astype(BF16))
                    scores.append((gmat * jnp.where(anti, jnp.exp(d_b), 0.0)).astype(BF16))
                y_diag = _dot(jnp.concatenate(scores, axis=1), jnp.concatenate(rhs, axis=0))
                y_parts.append(y_off[:, pair * LANES:(pair + 1) * LANES] + y_diag)
        y = jnp.concatenate(y_parts, axis=1) + xs * dskip_ref[...]
        yz = y * _silu(z_ref[t["rows"], :])
        ms = jnp.mean(yz * yz, axis=-1, keepdims=True)
        yn = (yz * lax.rsqrt(ms + EPS)) * gssd_ref[...]
        ymix_ref[t["rows"], CONV_WIDTH:] = yn.astype(BF16)

    state = None if pair_is_sequence else sb_ref[...]
    for c in reversed(range(N_CHUNKS)):
        starts_sequence = pair_is_sequence and c % 2 == 0
        s_b = None if (pair_is_sequence and c % 2 == 1) else state
        t = chunk_tables(c)
        chunk_output(t, None if starts_sequence else sfin_ref[c], s_b)
        if s_b is None:
            state = csb_ref[c]
        else:
            state = s_b * chunk_decay_row(t["e"])[1][:, SSD_WIDTH:] + csb_ref[c]
        if starts_sequence:
            sbo_ref[c // 2] = jnp.transpose(state)

    def out_body(i, carry):
        r0 = pl.multiple_of(i * ROW_BLOCK, ROW_BLOCK)
        rows = pl.ds(r0, ROW_BLOCK)
        o = _dot(ymix_ref[rows, :], wout_ref[...])
        out_ref[rows, :] = x_ref[rows, :] + gate_m * o
        return carry

    lax.fori_loop(0, SLAB // ROW_BLOCK, out_body, 0)


def _const_spec(shape):
    nd = len(shape)
    return pl.BlockSpec(shape, lambda i: (0,) * nd)


def _mixer_call(x2d, mod, mod_per_slab, seg_len, init_f, init_b, consts):
    n_slabs = x2d.shape[0] // SLAB
    seg_chunks = seg_len // CHUNK
    segs_per_slab = SLAB // seg_len
    has_init = init_f is not None
    state_rows = SSD_HEADS * SSD_HEAD_DIM
    staged_rows = SLAB + PAD * (1 + segs_per_slab)

    mod_map = (lambda i: (i, 0, 0)) if mod_per_slab else (lambda i: (0, 0, 0))
    in_specs = [
        pl.BlockSpec((SLAB, D_MODEL), lambda i: (i, 0)),
        pl.BlockSpec((1, N_MOD, D_MODEL), mod_map),
    ]
    args = [x2d, mod]
    if has_init:
        in_specs += [pl.BlockSpec((1, state_rows, D_STATE), lambda i: (i, 0, 0))] * 2
        args += [init_f, init_b]
    in_specs += [_const_spec(a.shape) for a in consts]
    args += list(consts)

    out_specs = [pl.BlockSpec((SLAB, D_MODEL), lambda i: (i, 0))]
    out_shape = [jax.ShapeDtypeStruct(x2d.shape, F32)]
    if not has_init:
        n_seq = n_slabs * segs_per_slab
        out_specs += [pl.BlockSpec((segs_per_slab, state_rows, D_STATE), lambda i: (i, 0, 0))] * 2
        out_shape += [jax.ShapeDtypeStruct((n_seq, state_rows, D_STATE), F32)] * 2

    scratch = [
        pltpu.VMEM((staged_rows, CONV_WIDTH), F32),
        pltpu.VMEM((staged_rows, 2 * SSD_WIDTH), F32),
        pltpu.VMEM((SLAB, CONV_WIDTH), F32),
        pltpu.VMEM((SLAB, SSD_WIDTH), F32),
        pltpu.VMEM((SLAB, LANES), F32),
        pltpu.VMEM((SLAB, LANES), F32),
        pltpu.VMEM((SLAB, SSD_WIDTH), F32),
        pltpu.VMEM((SLAB, 2 * SSD_GROUPS * D_STATE), BF16),
        pltpu.VMEM((SLAB, D_MODEL), BF16),
        pltpu.VMEM((N_CHUNKS, D_STATE, SSD_WIDTH), F32),
        pltpu.VMEM((N_CHUNKS, D_STATE, SSD_WIDTH), F32),
        pltpu.VMEM((D_STATE, SSD_WIDTH), F32),
        pltpu.VMEM((D_STATE, SSD_WIDTH), F32),
    ]
    return pl.pallas_call(
        functools.partial(_mixer_kernel, seg_chunks, has_init),
        grid=(n_slabs,),
        in_specs=in_specs,
        out_specs=out_specs,
        out_shape=out_shape,
        scratch_shapes=scratch,
        compiler_params=pltpu.CompilerParams(
            dimension_semantics=("arbitrary",), vmem_limit_bytes=VMEM_LIMIT),
        name="mixer_latent" if has_init else "mixer_prompt",
    )(*args)


def _ffn_conv(u, wc_ref, bc_ref, is_grid):
    n = u.shape[0]
    row = lax.broadcasted_iota(jnp.int32, (n, 1), 0)
    period = GRID_W if is_grid else 256
    pos = jnp.bitwise_and(row, period - 1)
    um = jnp.where(pos == 0, 0.0, pltpu.roll(u, 1, 0)).astype(BF16)
    up = jnp.where(pos == period - 1, 0.0, pltpu.roll(u, n - 1, 0)).astype(BF16)
    u = u.astype(BF16)
    wc = wc_ref[...].astype(BF16)

    def taps(di):
        return um * wc[3 * di:3 * di + 1, :] + u * wc[3 * di + 1:3 * di + 2, :] + up * wc[3 * di + 2:3 * di + 3, :]

    out = taps(1) + bc_ref[...].astype(BF16)
    if is_grid:
        zeros = jnp.zeros((GRID_W, u.shape[1]), BF16)
        out = out + jnp.concatenate([zeros, taps(0)[:n - GRID_W]], axis=0)
        out = out + jnp.concatenate([taps(2)[GRID_W:], zeros], axis=0)
    return out.astype(F32)


def _ffn_kernel(is_grid, x_ref, mod_ref, g2_ref, wup_ref, wc_ref, bc_ref, wdn_ref, gfin_ref, out_ref,
                h2_ref, acc_ref):
    j = pl.program_id(1)

    @pl.when(j == 0)
    def _():
        shift_f = mod_ref[0, 3:4, :]
        scale_f = mod_ref[0, 4:5, :]
        x = x_ref[...]
        ms = jnp.mean(x * x, axis=-1, keepdims=True)
        h2 = (x * lax.rsqrt(ms + EPS)) * g2_ref[...] * (1.0 + scale_f) + shift_f
        h2_ref[...] = h2.astype(BF16)
        acc_ref[...] = jnp.zeros_like(acc_ref)

    def up_proj(b):
        return _dot(h2_ref[...], wup_ref[b])

    def run_blocks(n_blocks):
        u_next = up_proj(0)
        for b in range(n_blocks):
            u = u_next
            if b + 1 < n_blocks:
                u_next = up_proj(b + 1)
            u = _ffn_conv(u, wc_ref.at[b], bc_ref.at[b], is_grid)
            a = (_silu(u[:, 0:FF_BLOCK]) * u[:, FF_BLOCK:]).astype(BF16)
            acc_ref[...] += _dot(a, wdn_ref[b])

    n_last = N_FF_BLOCKS - (N_FF_STEPS - 1) * FF_BLOCKS_PER_STEP

    @pl.when(j < N_FF_STEPS - 1)
    def _():
        run_blocks(FF_BLOCKS_PER_STEP)

    @pl.when(j == N_FF_STEPS - 1)
    def _():
        run_blocks(n_last)

    @pl.when(j == N_FF_STEPS - 1)
    def _():
        gate_f = mod_ref[0, 5:6, :]
        x2 = x_ref[...] + gate_f * acc_ref[...]
        ms = jnp.mean(x2 * x2, axis=-1, keepdims=True)
        out_ref[...] = (x2 * lax.rsqrt(ms + EPS)) * gfin_ref[...]


def _ffn_call(x2d, mod, mod_per_slab, is_grid, consts):
    n_slabs = x2d.shape[0] // SLAB
    g2, wup3, wc3, bc3, wdn3, gfin = consts
    mod_map = (lambda i, j: (i, 0, 0)) if mod_per_slab else (lambda i, j: (0, 0, 0))
    return pl.pallas_call(
        functools.partial(_ffn_kernel, is_grid),
        grid=(n_slabs, N_FF_STEPS),
        in_specs=[
            pl.BlockSpec((SLAB, D_MODEL), lambda i, j: (i, 0)),
            pl.BlockSpec((1, N_MOD, D_MODEL), mod_map),
            pl.BlockSpec((1, D_MODEL), lambda i, j: (0, 0)),
            pl.BlockSpec((FF_BLOCKS_PER_STEP, D_MODEL, 2 * FF_BLOCK), lambda i, j: (j, 0, 0)),
            pl.BlockSpec((FF_BLOCKS_PER_STEP, 9, 2 * FF_BLOCK), lambda i, j: (j, 0, 0)),
            pl.BlockSpec((FF_BLOCKS_PER_STEP, 1, 2 * FF_BLOCK), lambda i, j: (j, 0, 0)),
            pl.BlockSpec((FF_BLOCKS_PER_STEP, FF_BLOCK, D_MODEL), lambda i, j: (j, 0, 0)),
            pl.BlockSpec((1, D_MODEL), lambda i, j: (0, 0)),
        ],
        out_specs=pl.BlockSpec((SLAB, D_MODEL), lambda i, j: (i, 0)),
        out_shape=jax.ShapeDtypeStruct(x2d.shape, F32),
        scratch_shapes=[
            pltpu.VMEM((SLAB, D_MODEL), BF16),
            pltpu.VMEM((SLAB, D_MODEL), F32),
        ],
        compiler_params=pltpu.CompilerParams(
            dimension_semantics=("arbitrary", "arbitrary"), vmem_limit_bytes=VMEM_LIMIT),
        name="ffn_latent" if is_grid else "ffn_prompt",
    )(x2d, mod, g2, wup3, wc3, bc3, wdn3, gfin)


def _pad_lanes(v):
    return jnp.pad(v.reshape(1, -1), ((0, 0), (0, LANES - v.size)))


def kernel(x_prompt, x_sample, state_ssd_fwd, state_ssd_bwd, c, c_ctx, g_norm1, g_norm2, w_ada, b_ada, w_in, w_conv_short, w_conv_ssd, b_conv_ssd, dt_bias, a_log, d_skip, g_ssd_norm, w_out, w_up, w_ffn_conv, b_ffn_conv, w_down, g_final):
    depth = w_in.shape[0]
    assert depth == 1, "kernel is written for the single-layer problem"
    bp, seq = x_prompt.shape[0], x_prompt.shape[1]
    bd, dec_seq = x_sample.shape[0], x_sample.shape[1]
    assert seq == 256 and dec_seq == SLAB and (bp * seq) % SLAB == 0

    n_c = bd + 1
    rows = -(-n_c // 8) * 8
    cvec = jnp.concatenate([c, c_ctx[None], jnp.zeros((rows - n_c, D_MODEL), F32)], axis=0)
    mod = _mod_call(cvec, w_ada[0], b_ada[0].reshape(1, -1)).reshape(rows, N_MOD, D_MODEL)
    mod_lat = mod[:bd]
    mod_ctx = mod[bd:bd + 1]

    w_in_main = _cast_bf16(w_in[0], DT_COL)
    w_dt = jnp.pad(w_in[0][:, DT_COL:], ((0, 0), (0, LANES - 2 * SSD_HEADS))).astype(BF16)
    idx = jnp.arange(CHUNK)
    ltri = (idx[:, None] >= idx[None, :]).astype(BF16)
    utri = (idx[:, None] <= idx[None, :]).astype(BF16)
    tri3 = jnp.concatenate([jnp.concatenate([ltri] * 3, axis=1), jnp.concatenate([utri] * 3, axis=1)], axis=0)
    expand = (jnp.arange(LANES)[:, None] == (jnp.arange(2 * SSD_WIDTH)[None, :] // SSD_HEAD_DIM)).astype(BF16)
    mixer_consts = (
        g_norm1[0].reshape(1, -1), w_in_main, w_dt, w_conv_short[0], w_conv_ssd[0], b_conv_ssd[0].reshape(1, -1),
        _pad_lanes(dt_bias[0]), _pad_lanes(a_log[0]),
        jnp.repeat(d_skip[0], SSD_HEAD_DIM).reshape(1, -1), g_ssd_norm[0].reshape(1, -1),
        _cast_bf16(w_out[0], D_MODEL), tri3, expand, jnp.concatenate([expand] * 3, axis=0),
    )
    wup3 = _up_blocks(w_up[0])
    pad_blocks = ((0, N_FF_PADDED - N_FF_BLOCKS), (0, 0), (0, 0))
    wc3 = jnp.pad(w_ffn_conv[0].reshape(9, 2, N_FF_BLOCKS, FF_BLOCK).transpose(2, 0, 1, 3)
                  .reshape(N_FF_BLOCKS, 9, 2 * FF_BLOCK), pad_blocks)
    bc3 = jnp.pad(b_ffn_conv[0].reshape(2, N_FF_BLOCKS, FF_BLOCK).transpose(1, 0, 2)
                  .reshape(N_FF_BLOCKS, 1, 2 * FF_BLOCK), pad_blocks)
    wdn3 = _cast_bf16(w_down[0], D_MODEL, N_FF_PADDED - N_FF_BLOCKS).reshape(N_FF_PADDED, FF_BLOCK, D_MODEL)
    ffn_consts = (g_norm2[0].reshape(1, -1), wup3, wc3, bc3, wdn3, g_final.reshape(1, -1))

    state_rows = SSD_HEADS * SSD_HEAD_DIM
    init_f = state_ssd_fwd[:, 0].reshape(bd, state_rows, D_STATE)
    init_b = state_ssd_bwd[:, 0].reshape(bd, state_rows, D_STATE)

    xp2d = x_prompt.reshape(bp * seq, D_MODEL)
    xs2d = x_sample.reshape(bd * dec_seq, D_MODEL)

    xp1, s_f, s_b = _mixer_call(xp2d, mod_ctx, False, seq, None, None, mixer_consts)
    (xs1,) = _mixer_call(xs2d, mod_lat, True, dec_seq, init_f, init_b, mixer_consts)

    y_prompt = _ffn_call(xp1, mod_ctx, False, False, ffn_consts).reshape(x_prompt.shape)
    y_sample = _ffn_call(xs1, mod_lat, True, True, ffn_consts).reshape(x_sample.shape)

    state_shape = (bp, depth, SSD_HEADS, SSD_HEAD_DIM, D_STATE)
    return (y_prompt, y_sample, s_f.reshape(state_shape), s_b.reshape(state_shape))
```

```python
import functools

import jax
import jax.numpy as jnp
from jax import lax
from jax.experimental import pallas as pl
from jax.experimental.pallas import tpu as pltpu

F32 = jnp.float32
BF16 = jnp.bfloat16

D_MODEL = 1024
GRID_W = 64
CONV_WIDTH = 512
SSD_WIDTH = 512
SSD_HEAD_DIM = 64
SSD_HEADS = 8
SSD_GROUPS = 2
D_STATE = 128
CHUNK = 128
D_FF = 2816
N_MOD = 6
EPS = 1e-6

SLAB = 1024
N_CHUNKS = SLAB // CHUNK
ROW_BLOCK = 256
PAD = 8
FF_BLOCK = 256
N_FF_BLOCKS = D_FF // FF_BLOCK
FF_BLOCKS_PER_STEP = 3
N_FF_STEPS = -(-N_FF_BLOCKS // FF_BLOCKS_PER_STEP)
N_FF_PADDED = N_FF_STEPS * FF_BLOCKS_PER_STEP
LANES = 128
DT_COL = 3072
VMEM_LIMIT = 58 * 1024 * 1024


def _silu(v):
    return v / (1.0 + jnp.exp(-v))


def _softplus(v):
    return jnp.maximum(v, 0.0) + jnp.log1p(jnp.exp(-jnp.abs(v)))


def _split3(v):
    hi = v.astype(BF16)
    r1 = v - hi.astype(F32)
    mid = r1.astype(BF16)
    lo = (r1 - mid.astype(F32)).astype(BF16)
    return hi, mid, lo


def _dot(a, b):
    return jnp.dot(a, b, preferred_element_type=F32)


def _dot3(m3, v):
    return _dot(m3, jnp.concatenate(_split3(v), axis=0))


def _expand3(v, expand3):
    return _dot(jnp.concatenate(_split3(v), axis=1), expand3)


def _conv3_rows(win, w_ref):
    n = win.shape[0]
    prev = pltpu.roll(win, 1, 0)[PAD:PAD + CHUNK]
    cur = win[PAD:PAD + CHUNK]
    nxt = pltpu.roll(win, n - 1, 0)[PAD:PAD + CHUNK]
    return prev * w_ref[0:1, :] + cur * w_ref[1:2, :] + nxt * w_ref[2:3, :]


def _mod_kernel(c_ref, w_ref, b_ref, o_ref):
    s = _silu(c_ref[...]).astype(BF16)
    o_ref[...] = _dot(s, w_ref[...].astype(BF16)) + b_ref[...]


def _mod_call(cvec, w_ada, b_ada):
    rows = cvec.shape[0]
    return pl.pallas_call(
        _mod_kernel,
        grid=(N_MOD,),
        in_specs=[
            pl.BlockSpec((rows, D_MODEL), lambda j: (0, 0)),
            pl.BlockSpec((D_MODEL, D_MODEL), lambda j: (0, j)),
            pl.BlockSpec((1, D_MODEL), lambda j: (0, j)),
        ],
        out_specs=pl.BlockSpec((rows, D_MODEL), lambda j: (0, j)),
        out_shape=jax.ShapeDtypeStruct((rows, N_MOD * D_MODEL), F32),
        compiler_params=pltpu.CompilerParams(dimension_semantics=("arbitrary",)),
        name="mod_vectors",
    )(cvec, w_ada, b_ada)


def _cast_kernel(n_src_blocks, src_ref, dst_ref):
    i = pl.program_id(0)

    @pl.when(i < n_src_blocks)
    def _():
        dst_ref[...] = src_ref[...].astype(BF16)

    @pl.when(i >= n_src_blocks)
    def _():
        dst_ref[...] = jnp.zeros_like(dst_ref)


def _cast_bf16(w, cols, pad_row_blocks=0):
    n_src = w.shape[0] // ROW_BLOCK
    n_dst = n_src + pad_row_blocks
    return pl.pallas_call(
        functools.partial(_cast_kernel, n_src),
        grid=(n_dst,),
        in_specs=[pl.BlockSpec((ROW_BLOCK, cols), lambda i: (jnp.minimum(i, n_src - 1), 0))],
        out_specs=pl.BlockSpec((ROW_BLOCK, cols), lambda i: (i, 0)),
        out_shape=jax.ShapeDtypeStruct((n_dst * ROW_BLOCK, cols), BF16),
        compiler_params=pltpu.CompilerParams(dimension_semantics=("arbitrary",)),
        name="cast_bf16",
    )(w)


def _up_blocks_kernel(gate_ref, value_ref, dst_ref):
    j = pl.program_id(0)

    @pl.when(j < N_FF_BLOCKS)
    def _():
        dst_ref[0, :, 0:FF_BLOCK] = gate_ref[...].astype(BF16)
        dst_ref[0, :, FF_BLOCK:] = value_ref[...].astype(BF16)

    @pl.when(j >= N_FF_BLOCKS)
    def _():
        dst_ref[...] = jnp.zeros_like(dst_ref)


def _up_blocks(w_up):
    last = N_FF_BLOCKS - 1
    return pl.pallas_call(
        _up_blocks_kernel,
        grid=(N_FF_PADDED,),
        in_specs=[
            pl.BlockSpec((D_MODEL, FF_BLOCK), lambda j: (0, jnp.minimum(j, last))),
            pl.BlockSpec((D_MODEL, FF_BLOCK), lambda j: (0, N_FF_BLOCKS + jnp.minimum(j, last))),
        ],
        out_specs=pl.BlockSpec((1, D_MODEL, 2 * FF_BLOCK), lambda j: (j, 0, 0)),
        out_shape=jax.ShapeDtypeStruct((N_FF_PADDED, D_MODEL, 2 * FF_BLOCK), BF16),
        compiler_params=pltpu.CompilerParams(dimension_semantics=("arbitrary",)),
        name="up_blocks",
    )(w_up, w_up)


def _mixer_kernel(seg_chunks, has_init, *refs):
    refs = list(refs)
    x_ref, mod_ref = refs[:2]
    refs = refs[2:]
    if has_init:
        initf_ref, initb_ref = refs[:2]
        refs = refs[2:]
    (g1_ref, win_ref, wdt_ref, wcs_ref, wcx_ref, bcx_ref, dtb_ref, alog_ref, dskip_ref, gssd_ref, wout_ref,
     tri3_ref, exp_ref, exp3_ref) = refs[:14]
    refs = refs[14:]
    out_ref = refs[0]
    refs = refs[1:]
    if not has_init:
        sfo_ref, sbo_ref = refs[:2]
        refs = refs[2:]
    (gchc_ref, xbc_ref, gb_ref, z_ref, dt_ref, e_ref, xs_ref, bc_ref, ymix_ref, sfin_ref, csb_ref,
     sf_ref, sb_ref) = refs

    shift_m = mod_ref[0, 0:1, :]
    scale_m = mod_ref[0, 1:2, :]
    gate_m = mod_ref[0, 2:3, :]

    seg_len = seg_chunks * CHUNK
    zpad_a = jnp.zeros((PAD, CONV_WIDTH), F32)
    zpad_x = jnp.zeros((PAD, 2 * SSD_WIDTH), F32)
    for s in range(SLAB // seg_len + 1):
        gap = slice(s * (seg_len + PAD), s * (seg_len + PAD) + PAD)
        gchc_ref[gap, :] = zpad_a
        xbc_ref[gap, :] = zpad_x

    def staged_row(r):
        return r + (r // seg_len) * PAD

    def proj_block(i):
        rows = slice(i * ROW_BLOCK, (i + 1) * ROW_BLOCK)
        rp = staged_row(i * ROW_BLOCK) + PAD
        staged = slice(rp, rp + ROW_BLOCK)
        xb = x_ref[rows, :]
        ms = jnp.mean(xb * xb, axis=-1, keepdims=True)
        h = (xb * lax.rsqrt(ms + EPS)) * g1_ref[...] * (1.0 + scale_m) + shift_m
        hb = h.astype(BF16)

        def proj(lo, hi):
            return _dot(hb, win_ref[:, lo:hi])

        gchc_ref[staged, :] = proj(0, 512) * proj(1024, 1536)
        gb_ref[rows, :] = proj(512, 1024)
        z_ref[rows, :] = proj(1536, 2048)
        xbc_ref[staged, :] = proj(2048, 3072)
        dt_ref[rows, :] = _softplus(_dot(hb, wdt_ref[...]) + dtb_ref[...])

    a_row = -jnp.exp(alog_ref[...])
    lane_t = lax.broadcasted_iota(jnp.int32, (CHUNK, LANES), 1)
    lane_1 = lax.broadcasted_iota(jnp.int32, (1, LANES), 1)
    is_fwd_t = lane_t < SSD_HEADS
    is_fwd_1 = lane_1 < SSD_HEADS

    def chunk_decay_row(e):
        tot = jnp.where(is_fwd_1, e[CHUNK - 1:CHUNK, :], e[0:1, :])
        dec = jnp.broadcast_to(jnp.exp(tot), (8, LANES))
        return tot, _expand3(dec, exp3_ref[...])[0:1, :]

    pair_is_sequence = seg_chunks == 2
    assert pair_is_sequence != has_init and seg_chunks in (2, N_CHUNKS)

    def fwd_convs(c):
        rows = slice(c * CHUNK, (c + 1) * CHUNK)
        w0 = staged_row(c * CHUNK)
        window = slice(w0, w0 + CHUNK + 2 * PAD)
        xc = _conv3_rows(xbc_ref[window, :], wcx_ref)
        xc = _silu(xc + bcx_ref[...])
        xs_ref[rows, :] = xc[:, 0:SSD_WIDTH]
        bc_ref[rows, :] = xc[:, SSD_WIDTH:].astype(BF16)
        ca = _conv3_rows(gchc_ref[window, :], wcs_ref)
        ymix_ref[rows, 0:CONV_WIDTH] = (gb_ref[rows, :] * ca).astype(BF16)
        return xc

    def fwd_decays(c):
        rows = slice(c * CHUNK, (c + 1) * CHUNK)
        dtt = dt_ref[rows, :]
        da = dtt * a_row
        sums = _dot3(tri3_ref[...], da)
        e = jnp.where(is_fwd_t, sums[0:CHUNK], sums[CHUNK:])
        e_ref[rows, :] = e
        tot, dec_row = chunk_decay_row(e)
        wst = (dtt * jnp.exp(tot - e)).astype(BF16)
        return _dot(wst, exp_ref[...]), dec_row

    def fwd_chunk_states(xc, wexp):
        xs = xc[:, 0:SSD_WIDTH]
        xst_f = (xs * wexp[:, 0:SSD_WIDTH]).astype(BF16)
        xst_b = (xs * wexp[:, SSD_WIDTH:]).astype(BF16)
        cst_f, cst_b = [], []
        for g in range(SSD_GROUPS):
            bt = jnp.transpose(xc[:, SSD_WIDTH + g * D_STATE:SSD_WIDTH + (g + 1) * D_STATE]).astype(BF16)
            cst_f.append(_dot(bt, xst_f[:, g * 256:(g + 1) * 256]))
            cst_b.append(_dot(bt, xst_b[:, g * 256:(g + 1) * 256]))
        return jnp.concatenate(cst_f, axis=1), jnp.concatenate(cst_b, axis=1)

    if has_init:
        sf_ref[...] = jnp.transpose(initf_ref[0])
        sb_ref[...] = jnp.transpose(initb_ref[0])

    chunks_per_block = ROW_BLOCK // CHUNK
    xcs = []
    for i in range(SLAB // ROW_BLOCK):
        proj_block(i)
        if i >= 1:
            xcs += [fwd_convs(c) for c in range((i - 1) * chunks_per_block, i * chunks_per_block)]
    xcs += [fwd_convs(c) for c in range(N_CHUNKS - chunks_per_block, N_CHUNKS)]
    decays = [fwd_decays(c) for c in range(N_CHUNKS)]
    states = [fwd_chunk_states(xc, wexp) for xc, (wexp, _) in zip(xcs, decays)]
    state = None if pair_is_sequence else sf_ref[...]
    for c in range(N_CHUNKS):
        cst_f, cst_b = states[c]
        csb_ref[c] = cst_b
        s_in = None if (pair_is_sequence and c % 2 == 0) else state
        if s_in is None:
            state = cst_f
        else:
            sfin_ref[c] = s_in
            state = s_in * decays[c][1][:, 0:SSD_WIDTH] + cst_f
        if pair_is_sequence and c % 2 == 1:
            sfo_ref[c // 2] = jnp.transpose(state)

    row_t = lax.broadcasted_iota(jnp.int32, (CHUNK, CHUNK), 0)
    col_t = lax.broadcasted_iota(jnp.int32, (CHUNK, CHUNK), 1)
    causal = row_t >= col_t
    anti = col_t >= row_t
    low_half = lane_t < SSD_HEAD_DIM

    def chunk_tables(c):
        rows = slice(c * CHUNK, (c + 1) * CHUNK)
        e = e_ref[rows, :]
        xs = xs_ref[rows, :]
        dtexp = _dot(dt_ref[rows, :].astype(BF16), exp_ref[...])
        ecs = _dot(jnp.exp(e).astype(BF16), exp_ref[...])
        return dict(rows=rows, e=e, et=jnp.transpose(e), xs=xs, bc=bc_ref[rows, :], ecs=ecs,
                    xdt_f=xs * dtexp[:, 0:SSD_WIDTH], xdt_b=xs * dtexp[:, SSD_WIDTH:])

    def chunk_output(t, s_f, s_b):
        e, et, xs, bc, ecs, xdt_f, xdt_b = t["e"], t["et"], t["xs"], t["bc"], t["ecs"], t["xdt_f"], t["xdt_b"]
        y_parts = []
        for g in range(SSD_GROUPS):
            b_g = bc[:, g * D_STATE:(g + 1) * D_STATE]
            c_g = bc[:, 256 + g * D_STATE:256 + (g + 1) * D_STATE]
            gmat = lax.dot_general(c_g, b_g, (((1,), (1,)), ((), ())), preferred_element_type=F32)
            y_off = jnp.zeros((CHUNK, 256), F32)
            if s_f is not None:
                y_off = y_off + _dot(c_g, s_f[:, g * 256:(g + 1) * 256].astype(BF16)) \
                    * ecs[:, g * 256:(g + 1) * 256]
            if s_b is not None:
                y_off = y_off + _dot(c_g, s_b[:, g * 256:(g + 1) * 256].astype(BF16)) \
                    * ecs[:, SSD_WIDTH + g * 256:SSD_WIDTH + (g + 1) * 256]
            for pair in range(2):
                p0 = g * 256 + pair * LANES
                xf_pair = xdt_f[:, p0:p0 + LANES]
                xb_pair = xdt_b[:, p0:p0 + LANES]
                scores, rhs = [], []
                for sub in range(2):
                    head = g * 4 + pair * 2 + sub
                    keep = low_half if sub == 0 else jnp.logical_not(low_half)
                    rhs.append(jnp.where(keep, xf_pair, 0.0).astype(BF16))
                    rhs.append(jnp.where(keep, xb_pair, 0.0).astype(BF16))
                    jf, jb = head, SSD_HEADS + head
                    d_f = e[:, jf:jf + 1] - et[jf:jf + 1, :]
                    d_b = e[:, jb:jb + 1] - et[jb:jb + 1, :]
                    scores.append((gmat * jnp.where(causal, jnp.exp(d_f), 0.0)).astype(BF16))
                    scores.append((gmat * jnp.where(anti, jnp.exp(d_b), 0.0)).astype(BF16))
                y_diag = _dot(jnp.concatenate(scores, axis=1), jnp.concatenate(rhs, axis=0))
                y_parts.append(y_off[:, pair * LANES:(pair + 1) * LANES] + y_diag)
        y = jnp.concatenate(y_parts, axis=1) + xs * dskip_ref[...]
        yz = y * _silu(z_ref[t["rows"], :])
        ms = jnp.mean(yz * yz, axis=-1, keepdims=True)
        yn = (yz * lax.rsqrt(ms + EPS)) * gssd_ref[...]
        ymix_ref[t["rows"], CONV_WIDTH:] = yn.astype(BF16)

    state = None if pair_is_sequence else sb_ref[...]
    for c in reversed(range(N_CHUNKS)):
        starts_sequence = pair_is_sequence and c % 2 == 0
        s_b = None if (pair_is_sequence and c % 2 == 1) else state
        t = chunk_tables(c)
        chunk_output(t, None if starts_sequence else sfin_ref[c], s_b)
        if s_b is None:
            state = csb_ref[c]
        else:
            state = s_b * chunk_decay_row(t["e"])[1][:, SSD_WIDTH:] + csb_ref[c]
        if starts_sequence:
            sbo_ref[c // 2] = jnp.transpose(state)
        if c % chunks_per_block == 0:
            rows = slice(c * CHUNK, c * CHUNK + ROW_BLOCK)
            o = _dot(ymix_ref[rows, :], wout_ref[...])
            out_ref[rows, :] = x_ref[rows, :] + gate_m * o


def _const_spec(shape):
    nd = len(shape)
    return pl.BlockSpec(shape, lambda i: (0,) * nd)


def _mixer_call(x2d, mod, mod_per_slab, seg_len, init_f, init_b, consts):
    n_slabs = x2d.shape[0] // SLAB
    seg_chunks = seg_len // CHUNK
    segs_per_slab = SLAB // seg_len
    has_init = init_f is not None
    state_rows = SSD_HEADS * SSD_HEAD_DIM
    staged_rows = SLAB + PAD * (1 + segs_per_slab)

    mod_map = (lambda i: (i, 0, 0)) if mod_per_slab else (lambda i: (0, 0, 0))
    in_specs = [
        pl.BlockSpec((SLAB, D_MODEL), lambda i: (i, 0)),
        pl.BlockSpec((1, N_MOD, D_MODEL), mod_map),
    ]
    args = [x2d, mod]
    if has_init:
        in_specs += [pl.BlockSpec((1, state_rows, D_STATE), lambda i: (i, 0, 0))] * 2
        args += [init_f, init_b]
    in_specs += [_const_spec(a.shape) for a in consts]
    args += list(consts)

    out_specs = [pl.BlockSpec((SLAB, D_MODEL), lambda i: (i, 0))]
    out_shape = [jax.ShapeDtypeStruct(x2d.shape, F32)]
    if not has_init:
        n_seq = n_slabs * segs_per_slab
        out_specs += [pl.BlockSpec((segs_per_slab, state_rows, D_STATE), lambda i: (i, 0, 0))] * 2
        out_shape += [jax.ShapeDtypeStruct((n_seq, state_rows, D_STATE), F32)] * 2

    scratch = [
        pltpu.VMEM((staged_rows, CONV_WIDTH), F32),
        pltpu.VMEM((staged_rows, 2 * SSD_WIDTH), F32),
        pltpu.VMEM((SLAB, CONV_WIDTH), F32),
        pltpu.VMEM((SLAB, SSD_WIDTH), F32),
        pltpu.VMEM((SLAB, LANES), F32),
        pltpu.VMEM((SLAB, LANES), F32),
        pltpu.VMEM((SLAB, SSD_WIDTH), F32),
        pltpu.VMEM((SLAB, 2 * SSD_GROUPS * D_STATE), BF16),
        pltpu.VMEM((SLAB, D_MODEL), BF16),
        pltpu.VMEM((N_CHUNKS, D_STATE, SSD_WIDTH), F32),
        pltpu.VMEM((N_CHUNKS, D_STATE, SSD_WIDTH), F32),
        pltpu.VMEM((D_STATE, SSD_WIDTH), F32),
        pltpu.VMEM((D_STATE, SSD_WIDTH), F32),
    ]
    return pl.pallas_call(
        functools.partial(_mixer_kernel, seg_chunks, has_init),
        grid=(n_slabs,),
        in_specs=in_specs,
        out_specs=out_specs,
        out_shape=out_shape,
        scratch_shapes=scratch,
        compiler_params=pltpu.CompilerParams(
            dimension_semantics=("arbitrary",), vmem_limit_bytes=VMEM_LIMIT),
        name="mixer_latent" if has_init else "mixer_prompt",
    )(*args)


def _ffn_conv(u, wc_ref, bc_ref, is_grid):
    n = u.shape[0]
    row = lax.broadcasted_iota(jnp.int32, (n, 1), 0)
    period = GRID_W if is_grid else 256
    pos = jnp.bitwise_and(row, period - 1)
    um = jnp.where(pos == 0, 0.0, pltpu.roll(u, 1, 0)).astype(BF16)
    up = jnp.where(pos == period - 1, 0.0, pltpu.roll(u, n - 1, 0)).astype(BF16)
    u = u.astype(BF16)
    wc = wc_ref[...].astype(BF16)

    def taps(di):
        return um * wc[3 * di:3 * di + 1, :] + u * wc[3 * di + 1:3 * di + 2, :] + up * wc[3 * di + 2:3 * di + 3, :]

    out = taps(1) + bc_ref[...].astype(BF16)
    if is_grid:
        zeros = jnp.zeros((GRID_W, u.shape[1]), BF16)
        out = out + jnp.concatenate([zeros, taps(0)[:n - GRID_W]], axis=0)
        out = out + jnp.concatenate([taps(2)[GRID_W:], zeros], axis=0)
    return out.astype(F32)


def _ffn_kernel(is_grid, x_ref, mod_ref, g2_ref, wup_ref, wc_ref, bc_ref, wdn_ref, gfin_ref, out_ref,
                h2_ref, acc_ref):
    j = pl.program_id(1)

    @pl.when(j == 0)
    def _():
        shift_f = mod_ref[0, 3:4, :]
        scale_f = mod_ref[0, 4:5, :]
        x = x_ref[...]
        ms = jnp.mean(x * x, axis=-1, keepdims=True)
        h2 = (x * lax.rsqrt(ms + EPS)) * g2_ref[...] * (1.0 + scale_f) + shift_f
        h2_ref[...] = h2.astype(BF16)
        acc_ref[...] = jnp.zeros_like(acc_ref)

    def up_proj(b):
        return _dot(h2_ref[...], wup_ref[b])

    def run_blocks(n_blocks):
        u_next = up_proj(0)
        for b in range(n_blocks):
            u = u_next
            if b + 1 < n_blocks:
                u_next = up_proj(b + 1)
            u = _ffn_conv(u, wc_ref.at[b], bc_ref.at[b], is_grid)
            a = (_silu(u[:, 0:FF_BLOCK]) * u[:, FF_BLOCK:]).astype(BF16)
            acc_ref[...] += _dot(a, wdn_ref[b])

    n_last = N_FF_BLOCKS - (N_FF_STEPS - 1) * FF_BLOCKS_PER_STEP

    @pl.when(j < N_FF_STEPS - 1)
    def _():
        run_blocks(FF_BLOCKS_PER_STEP)

    @pl.when(j == N_FF_STEPS - 1)
    def _():
        run_blocks(n_last)

    @pl.when(j == N_FF_STEPS - 1)
    def _():
        gate_f = mod_ref[0, 5:6, :]
        x2 = x_ref[...] + gate_f * acc_ref[...]
        ms = jnp.mean(x2 * x2, axis=-1, keepdims=True)
        out_ref[...] = (x2 * lax.rsqrt(ms + EPS)) * gfin_ref[...]


def _ffn_call(x2d, mod, mod_per_slab, is_grid, consts):
    n_slabs = x2d.shape[0] // SLAB
    g2, wup3, wc3, bc3, wdn3, gfin = consts
    mod_map = (lambda i, j: (i, 0, 0)) if mod_per_slab else (lambda i, j: (0, 0, 0))
    return pl.pallas_call(
        functools.partial(_ffn_kernel, is_grid),
        grid=(n_slabs, N_FF_STEPS),
        in_specs=[
            pl.BlockSpec((SLAB, D_MODEL), lambda i, j: (i, 0)),
            pl.BlockSpec((1, N_MOD, D_MODEL), mod_map),
            pl.BlockSpec((1, D_MODEL), lambda i, j: (0, 0)),
            pl.BlockSpec((FF_BLOCKS_PER_STEP, D_MODEL, 2 * FF_BLOCK), lambda i, j: (j, 0, 0)),
            pl.BlockSpec((FF_BLOCKS_PER_STEP, 9, 2 * FF_BLOCK), lambda i, j: (j, 0, 0)),
            pl.BlockSpec((FF_BLOCKS_PER_STEP, 1, 2 * FF_BLOCK), lambda i, j: (j, 0, 0)),
            pl.BlockSpec((FF_BLOCKS_PER_STEP, FF_BLOCK, D_MODEL), lambda i, j: (j, 0, 0)),
            pl.BlockSpec((1, D_MODEL), lambda i, j: (0, 0)),
        ],
        out_specs=pl.BlockSpec((SLAB, D_MODEL), lambda i, j: (i, 0)),
        out_shape=jax.ShapeDtypeStruct(x2d.shape, F32),
        scratch_shapes=[
            pltpu.VMEM((SLAB, D_MODEL), BF16),
            pltpu.VMEM((SLAB, D_MODEL), F32),
        ],
        compiler_params=pltpu.CompilerParams(
            dimension_semantics=("arbitrary", "arbitrary"), vmem_limit_bytes=VMEM_LIMIT),
        name="ffn_latent" if is_grid else "ffn_prompt",
    )(x2d, mod, g2, wup3, wc3, bc3, wdn3, gfin)


def _pad_lanes(v):
    return jnp.pad(v.reshape(1, -1), ((0, 0), (0, LANES - v.size)))


def kernel(x_prompt, x_sample, state_ssd_fwd, state_ssd_bwd, c, c_ctx, g_norm1, g_norm2, w_ada, b_ada, w_in, w_conv_short, w_conv_ssd, b_conv_ssd, dt_bias, a_log, d_skip, g_ssd_norm, w_out, w_up, w_ffn_conv, b_ffn_conv, w_down, g_final):
    depth = w_in.shape[0]
    assert depth == 1, "kernel is written for the single-layer problem"
    bp, seq = x_prompt.shape[0], x_prompt.shape[1]
    bd, dec_seq = x_sample.shape[0], x_sample.shape[1]
    assert seq == 256 and dec_seq == SLAB and (bp * seq) % SLAB == 0

    n_c = bd + 1
    rows = -(-n_c // 8) * 8
    cvec = jnp.concatenate([c, c_ctx[None], jnp.zeros((rows - n_c, D_MODEL), F32)], axis=0)
    mod = _mod_call(cvec, w_ada[0], b_ada[0].reshape(1, -1)).reshape(rows, N_MOD, D_MODEL)
    mod_lat = mod[:bd]
    mod_ctx = mod[bd:bd + 1]

    w_in_main = _cast_bf16(w_in[0], DT_COL)
    w_dt = jnp.pad(w_in[0][:, DT_COL:], ((0, 0), (0, LANES - 2 * SSD_HEADS))).astype(BF16)
    idx = jnp.arange(CHUNK)
    ltri = (idx[:, None] >= idx[None, :]).astype(BF16)
    utri = (idx[:, None] <= idx[None, :]).astype(BF16)
    tri3 = jnp.concatenate([jnp.concatenate([ltri] * 3, axis=1), jnp.concatenate([utri] * 3, axis=1)], axis=0)
    expand = (jnp.arange(LANES)[:, None] == (jnp.arange(2 * SSD_WIDTH)[None, :] // SSD_HEAD_DIM)).astype(BF16)
    mixer_consts = (
        g_norm1[0].reshape(1, -1), w_in_main, w_dt, w_conv_short[0], w_conv_ssd[0], b_conv_ssd[0].reshape(1, -1),
        _pad_lanes(dt_bias[0]), _pad_lanes(a_log[0]),
        jnp.repeat(d_skip[0], SSD_HEAD_DIM).reshape(1, -1), g_ssd_norm[0].reshape(1, -1),
        _cast_bf16(w_out[0], D_MODEL), tri3, expand, jnp.concatenate([expand] * 3, axis=0),
    )
    wup3 = _up_blocks(w_up[0])
    pad_blocks = ((0, N_FF_PADDED - N_FF_BLOCKS), (0, 0), (0, 0))
    wc3 = jnp.pad(w_ffn_conv[0].reshape(9, 2, N_FF_BLOCKS, FF_BLOCK).transpose(2, 0, 1, 3)
                  .reshape(N_FF_BLOCKS, 9, 2 * FF_BLOCK), pad_blocks)
    bc3 = jnp.pad(b_ffn_conv[0].reshape(2, N_FF_BLOCKS, FF_BLOCK).transpose(1, 0, 2)
                  .reshape(N_FF_BLOCKS, 1, 2 * FF_BLOCK), pad_blocks)
    wdn3 = _cast_bf16(w_down[0], D_MODEL, N_FF_PADDED - N_FF_BLOCKS).reshape(N_FF_PADDED, FF_BLOCK, D_MODEL)
    ffn_consts = (g_norm2[0].reshape(1, -1), wup3, wc3, bc3, wdn3, g_final.reshape(1, -1))

    state_rows = SSD_HEADS * SSD_HEAD_DIM
    init_f = state_ssd_fwd[:, 0].reshape(bd, state_rows, D_STATE)
    init_b = state_ssd_bwd[:, 0].reshape(bd, state_rows, D_STATE)

    xp2d = x_prompt.reshape(bp * seq, D_MODEL)
    xs2d = x_sample.reshape(bd * dec_seq, D_MODEL)

    xp1, s_f, s_b = _mixer_call(xp2d, mod_ctx, False, seq, None, None, mixer_consts)
    (xs1,) = _mixer_call(xs2d, mod_lat, True, dec_seq, init_f, init_b, mixer_consts)

    y_prompt = _ffn_call(xp1, mod_ctx, False, False, ffn_consts).reshape(x_prompt.shape)
    y_sample = _ffn_call(xs1, mod_lat, True, True, ffn_consts).reshape(x_sample.shape)

    state_shape = (bp, depth, SSD_HEADS, SSD_HEAD_DIM, D_STATE)
    return (y_prompt, y_sample, s_f.reshape(state_shape), s_b.reshape(state_shape))
```

```python
import functools

import jax
import jax.numpy as jnp
from jax import lax
from jax.experimental import pallas as pl
from jax.experimental.pallas import tpu as pltpu

F32 = jnp.float32
BF16 = jnp.bfloat16

D_MODEL = 1024
GRID_W = 64
CONV_WIDTH = 512
SSD_WIDTH = 512
SSD_HEAD_DIM = 64
SSD_HEADS = 8
SSD_GROUPS = 2
D_STATE = 128
CHUNK = 128
D_FF = 2816
N_MOD = 6
EPS = 1e-6

SLAB = 1024
N_CHUNKS = SLAB // CHUNK
ROW_BLOCK = 256
PAD = 8
FF_BLOCK = 256
N_FF_BLOCKS = D_FF // FF_BLOCK
FF_BLOCKS_PER_STEP = 3
N_FF_STEPS = -(-N_FF_BLOCKS // FF_BLOCKS_PER_STEP)
N_FF_PADDED = N_FF_STEPS * FF_BLOCKS_PER_STEP
LANES = 128
DT_COL = 3072
VMEM_LIMIT = 58 * 1024 * 1024


def _silu(v):
    return v / (1.0 + jnp.exp(-v))


def _softplus(v):
    return jnp.maximum(v, 0.0) + jnp.log1p(jnp.exp(-jnp.abs(v)))


def _split3(v):
    hi = v.astype(BF16)
    r1 = v - hi.astype(F32)
    mid = r1.astype(BF16)
    lo = (r1 - mid.astype(F32)).astype(BF16)
    return hi, mid, lo


def _dot(a, b):
    return jnp.dot(a, b, preferred_element_type=F32)


def _dot3(m3, v):
    return _dot(m3, jnp.concatenate(_split3(v), axis=0))


def _expand3(v, expand3):
    return _dot(jnp.concatenate(_split3(v), axis=1), expand3)


def _conv3_rows(win, w_ref):
    n = win.shape[0]
    prev = pltpu.roll(win, 1, 0)[PAD:PAD + CHUNK]
    cur = win[PAD:PAD + CHUNK]
    nxt = pltpu.roll(win, n - 1, 0)[PAD:PAD + CHUNK]
    return prev * w_ref[0:1, :] + cur * w_ref[1:2, :] + nxt * w_ref[2:3, :]


def _mod_kernel(c_ref, w_ref, b_ref, o_ref):
    s = _silu(c_ref[...]).astype(BF16)
    o_ref[...] = _dot(s, w_ref[...].astype(BF16)) + b_ref[...]


def _mod_call(cvec, w_ada, b_ada):
    rows = cvec.shape[0]
    return pl.pallas_call(
        _mod_kernel,
        grid=(N_MOD,),
        in_specs=[
            pl.BlockSpec((rows, D_MODEL), lambda j: (0, 0)),
            pl.BlockSpec((D_MODEL, D_MODEL), lambda j: (0, j)),
            pl.BlockSpec((1, D_MODEL), lambda j: (0, j)),
        ],
        out_specs=pl.BlockSpec((rows, D_MODEL), lambda j: (0, j)),
        out_shape=jax.ShapeDtypeStruct((rows, N_MOD * D_MODEL), F32),
        compiler_params=pltpu.CompilerParams(dimension_semantics=("arbitrary",)),
        name="mod_vectors",
    )(cvec, w_ada, b_ada)


def _cast_kernel(n_src_blocks, src_ref, dst_ref):
    i = pl.program_id(0)

    @pl.when(i < n_src_blocks)
    def _():
        dst_ref[...] = src_ref[...].astype(BF16)

    @pl.when(i >= n_src_blocks)
    def _():
        dst_ref[...] = jnp.zeros_like(dst_ref)


def _cast_bf16(w, cols, pad_row_blocks=0):
    n_src = w.shape[0] // ROW_BLOCK
    n_dst = n_src + pad_row_blocks
    return pl.pallas_call(
        functools.partial(_cast_kernel, n_src),
        grid=(n_dst,),
        in_specs=[pl.BlockSpec((ROW_BLOCK, cols), lambda i: (jnp.minimum(i, n_src - 1), 0))],
        out_specs=pl.BlockSpec((ROW_BLOCK, cols), lambda i: (i, 0)),
        out_shape=jax.ShapeDtypeStruct((n_dst * ROW_BLOCK, cols), BF16),
        compiler_params=pltpu.CompilerParams(dimension_semantics=("arbitrary",)),
        name="cast_bf16",
    )(w)


def _up_blocks_kernel(gate_ref, value_ref, dst_ref):
    j = pl.program_id(0)

    @pl.when(j < N_FF_BLOCKS)
    def _():
        dst_ref[0, :, 0:FF_BLOCK] = gate_ref[...].astype(BF16)
        dst_ref[0, :, FF_BLOCK:] = value_ref[...].astype(BF16)

    @pl.when(j >= N_FF_BLOCKS)
    def _():
        dst_ref[...] = jnp.zeros_like(dst_ref)


def _up_blocks(w_up):
    last = N_FF_BLOCKS - 1
    return pl.pallas_call(
        _up_blocks_kernel,
        grid=(N_FF_PADDED,),
        in_specs=[
            pl.BlockSpec((D_MODEL, FF_BLOCK), lambda j: (0, jnp.minimum(j, last))),
            pl.BlockSpec((D_MODEL, FF_BLOCK), lambda j: (0, N_FF_BLOCKS + jnp.minimum(j, last))),
        ],
        out_specs=pl.BlockSpec((1, D_MODEL, 2 * FF_BLOCK), lambda j: (j, 0, 0)),
        out_shape=jax.ShapeDtypeStruct((N_FF_PADDED, D_MODEL, 2 * FF_BLOCK), BF16),
        compiler_params=pltpu.CompilerParams(dimension_semantics=("arbitrary",)),
        name="up_blocks",
    )(w_up, w_up)


def _mixer_kernel(seg_chunks, has_init, *refs):
    refs = list(refs)
    x_ref, mod_ref = refs[:2]
    refs = refs[2:]
    if has_init:
        initf_ref, initb_ref = refs[:2]
        refs = refs[2:]
    (g1_ref, win_ref, wdt_ref, wcs_ref, wcx_ref, bcx_ref, dtb_ref, alog_ref, dskip_ref, gssd_ref, wout_ref,
     tri3_ref, exp_ref, exp3_ref) = refs[:14]
    refs = refs[14:]
    out_ref = refs[0]
    refs = refs[1:]
    if not has_init:
        sfo_ref, sbo_ref = refs[:2]
        refs = refs[2:]
    (gchc_ref, xbc_ref, gb_ref, z_ref, dt_ref, e_ref, xs_ref, bc_ref, ymix_ref, sfin_ref, csb_ref,
     sf_ref, sb_ref) = refs

    shift_m = mod_ref[0, 0:1, :]
    scale_m = mod_ref[0, 1:2, :]
    gate_m = mod_ref[0, 2:3, :]

    seg_len = seg_chunks * CHUNK
    zpad_a = jnp.zeros((PAD, CONV_WIDTH), F32)
    zpad_x = jnp.zeros((PAD, 2 * SSD_WIDTH), F32)
    for s in range(SLAB // seg_len + 1):
        gap = slice(s * (seg_len + PAD), s * (seg_len + PAD) + PAD)
        gchc_ref[gap, :] = zpad_a
        xbc_ref[gap, :] = zpad_x

    def staged_row(r):
        return r + (r // seg_len) * PAD

    def proj_block(i):
        rows = slice(i * ROW_BLOCK, (i + 1) * ROW_BLOCK)
        rp = staged_row(i * ROW_BLOCK) + PAD
        staged = slice(rp, rp + ROW_BLOCK)
        xb = x_ref[rows, :]
        ms = jnp.mean(xb * xb, axis=-1, keepdims=True)
        h = (xb * lax.rsqrt(ms + EPS)) * g1_ref[...] * (1.0 + scale_m) + shift_m
        hb = h.astype(BF16)

        def proj(lo, hi):
            return _dot(hb, win_ref[:, lo:hi])

        gchc_ref[staged, :] = proj(0, 512) * proj(1024, 1536)
        gb_ref[rows, :] = proj(512, 1024)
        z_ref[rows, :] = proj(1536, 2048)
        xbc_ref[staged, :] = proj(2048, 3072)
        dt_ref[rows, :] = _softplus(_dot(hb, wdt_ref[...]) + dtb_ref[...])

    a_row = -jnp.exp(alog_ref[...])
    lane_t = lax.broadcasted_iota(jnp.int32, (CHUNK, LANES), 1)
    lane_1 = lax.broadcasted_iota(jnp.int32, (1, LANES), 1)
    is_fwd_t = lane_t < SSD_HEADS
    is_fwd_1 = lane_1 < SSD_HEADS

    def chunk_decay_row(e):
        tot = jnp.where(is_fwd_1, e[CHUNK - 1:CHUNK, :], e[0:1, :])
        dec = jnp.broadcast_to(jnp.exp(tot), (8, LANES))
        return tot, _expand3(dec, exp3_ref[...])[0:1, :]

    pair_is_sequence = seg_chunks == 2
    assert pair_is_sequence != has_init and seg_chunks in (2, N_CHUNKS)

    def fwd_convs(c):
        rows = slice(c * CHUNK, (c + 1) * CHUNK)
        w0 = staged_row(c * CHUNK)
        window = slice(w0, w0 + CHUNK + 2 * PAD)
        xc = _conv3_rows(xbc_ref[window, :], wcx_ref)
        xc = _silu(xc + bcx_ref[...])
        xs_ref[rows, :] = xc[:, 0:SSD_WIDTH]
        bc_ref[rows, :] = xc[:, SSD_WIDTH:].astype(BF16)
        ca = _conv3_rows(gchc_ref[window, :], wcs_ref)
        ymix_ref[rows, 0:CONV_WIDTH] = (gb_ref[rows, :] * ca).astype(BF16)
        return xc

    def fwd_decays(c):
        rows = slice(c * CHUNK, (c + 1) * CHUNK)
        dtt = dt_ref[rows, :]
        da = dtt * a_row
        sums = _dot3(tri3_ref[...], da)
        e = jnp.where(is_fwd_t, sums[0:CHUNK], sums[CHUNK:])
        e_ref[rows, :] = e
        tot, dec_row = chunk_decay_row(e)
        wst = (dtt * jnp.exp(tot - e)).astype(BF16)
        return _dot(wst, exp_ref[...]), dec_row

    def fwd_chunk_states(xc, wexp):
        xs = xc[:, 0:SSD_WIDTH]
        xst_f = (xs * wexp[:, 0:SSD_WIDTH]).astype(BF16)
        xst_b = (xs * wexp[:, SSD_WIDTH:]).astype(BF16)
        cst_f, cst_b = [], []
        for g in range(SSD_GROUPS):
            bt = jnp.transpose(xc[:, SSD_WIDTH + g * D_STATE:SSD_WIDTH + (g + 1) * D_STATE]).astype(BF16)
            cst_f.append(_dot(bt, xst_f[:, g * 256:(g + 1) * 256]))
            cst_b.append(_dot(bt, xst_b[:, g * 256:(g + 1) * 256]))
        return jnp.concatenate(cst_f, axis=1), jnp.concatenate(cst_b, axis=1)

    if has_init:
        sf_ref[...] = jnp.transpose(initf_ref[0])
        sb_ref[...] = jnp.transpose(initb_ref[0])

    chunks_per_block = ROW_BLOCK // CHUNK
    xcs = []
    for i in range(SLAB // ROW_BLOCK):
        proj_block(i)
        if i >= 1:
            xcs += [fwd_convs(c) for c in range((i - 1) * chunks_per_block, i * chunks_per_block)]
    xcs += [fwd_convs(c) for c in range(N_CHUNKS - chunks_per_block, N_CHUNKS)]
    decays = [fwd_decays(c) for c in range(N_CHUNKS)]
    states = [fwd_chunk_states(xc, wexp) for xc, (wexp, _) in zip(xcs, decays)]
    state = None if pair_is_sequence else sf_ref[...]
    for c in range(N_CHUNKS):
        cst_f, cst_b = states[c]
        csb_ref[c] = cst_b
        s_in = None if (pair_is_sequence and c % 2 == 0) else state
        if s_in is None:
            state = cst_f
        else:
            sfin_ref[c] = s_in
            state = s_in * decays[c][1][:, 0:SSD_WIDTH] + cst_f
        if pair_is_sequence and c % 2 == 1:
            sfo_ref[c // 2] = jnp.transpose(state)

    row_t = lax.broadcasted_iota(jnp.int32, (CHUNK, CHUNK), 0)
    col_t = lax.broadcasted_iota(jnp.int32, (CHUNK, CHUNK), 1)
    causal = row_t >= col_t
    anti = col_t >= row_t
    low_half = lane_t < SSD_HEAD_DIM

    def chunk_tables(c):
        rows = slice(c * CHUNK, (c + 1) * CHUNK)
        e = e_ref[rows, :]
        xs = xs_ref[rows, :]
        dtexp = _dot(dt_ref[rows, :].astype(BF16), exp_ref[...])
        ecs = _dot(jnp.exp(e).astype(BF16), exp_ref[...])
        return dict(rows=rows, e=e, et=jnp.transpose(e), xs=xs, bc=bc_ref[rows, :], ecs=ecs,
                    xdt_f=xs * dtexp[:, 0:SSD_WIDTH], xdt_b=xs * dtexp[:, SSD_WIDTH:])

    def chunk_output(t, s_f, s_b):
        e, et, xs, bc, ecs, xdt_f, xdt_b = t["e"], t["et"], t["xs"], t["bc"], t["ecs"], t["xdt_f"], t["xdt_b"]
        y_parts = []
        for g in range(SSD_GROUPS):
            b_g = bc[:, g * D_STATE:(g + 1) * D_STATE]
            c_g = bc[:, 256 + g * D_STATE:256 + (g + 1) * D_STATE]
            gmat = lax.dot_general(c_g, b_g, (((1,), (1,)), ((), ())), preferred_element_type=F32)
            y_off = jnp.zeros((CHUNK, 256), F32)
            if s_f is not None:
                y_off = y_off + _dot(c_g, s_f[:, g * 256:(g + 1) * 256].astype(BF16)) \
                    * ecs[:, g * 256:(g + 1) * 256]
            if s_b is not None:
                y_off = y_off + _dot(c_g, s_b[:, g * 256:(g + 1) * 256].astype(BF16)) \
                    * ecs[:, SSD_WIDTH + g * 256:SSD_WIDTH + (g + 1) * 256]
            for pair in range(2):
                p0 = g * 256 + pair * LANES
                xf_pair = xdt_f[:, p0:p0 + LANES]
                xb_pair = xdt_b[:, p0:p0 + LANES]
                scores, rhs = [], []
                for sub in range(2):
                    head = g * 4 + pair * 2 + sub
                    keep = low_half if sub == 0 else jnp.logical_not(low_half)
                    rhs.append(jnp.where(keep, xf_pair, 0.0).astype(BF16))
                    rhs.append(jnp.where(keep, xb_pair, 0.0).astype(BF16))
                    jf, jb = head, SSD_HEADS + head
                    d_f = e[:, jf:jf + 1] - et[jf:jf + 1, :]
                    d_b = e[:, jb:jb + 1] - et[jb:jb + 1, :]
                    scores.append((gmat * jnp.where(causal, jnp.exp(d_f), 0.0)).astype(BF16))
                    scores.append((gmat * jnp.where(anti, jnp.exp(d_b), 0.0)).astype(BF16))
                y_diag = _dot(jnp.concatenate(scores, axis=1), jnp.concatenate(rhs, axis=0))
                y_parts.append(y_off[:, pair * LANES:(pair + 1) * LANES] + y_diag)
        y = jnp.concatenate(y_parts, axis=1) + xs * dskip_ref[...]
        yz = y * _silu(z_ref[t["rows"], :])
        ms = jnp.mean(yz * yz, axis=-1, keepdims=True)
        yn = (yz * lax.rsqrt(ms + EPS)) * gssd_ref[...]
        ymix_ref[t["rows"], CONV_WIDTH:] = yn.astype(BF16)

    state = None if pair_is_sequence else sb_ref[...]
    for c in reversed(range(N_CHUNKS)):
        starts_sequence = pair_is_sequence and c % 2 == 0
        s_b = None if (pair_is_sequence and c % 2 == 1) else state
        t = chunk_tables(c)
        chunk_output(t, None if starts_sequence else sfin_ref[c], s_b)
        if s_b is None:
            state = csb_ref[c]
        else:
            state = s_b * chunk_decay_row(t["e"])[1][:, SSD_WIDTH:] + csb_ref[c]
        if starts_sequence:
            sbo_ref[c // 2] = jnp.transpose(state)
        if c % chunks_per_block == 0:
            rows = slice(c * CHUNK, c * CHUNK + ROW_BLOCK)
            o = _dot(ymix_ref[rows, :], wout_ref[...])
            out_ref[rows, :] = x_ref[rows, :] + gate_m * o


def _const_spec(shape):
    nd = len(shape)
    return pl.BlockSpec(shape, lambda i: (0,) * nd)


def _mixer_call(x2d, mod, mod_per_slab, seg_len, init_f, init_b, consts):
    n_slabs = x2d.shape[0] // SLAB
    seg_chunks = seg_len // CHUNK
    segs_per_slab = SLAB // seg_len
    has_init = init_f is not None
    state_rows = SSD_HEADS * SSD_HEAD_DIM
    staged_rows = SLAB + PAD * (1 + segs_per_slab)

    mod_map = (lambda i: (i, 0, 0)) if mod_per_slab else (lambda i: (0, 0, 0))
    in_specs = [
        pl.BlockSpec((SLAB, D_MODEL), lambda i: (i, 0)),
        pl.BlockSpec((1, N_MOD, D_MODEL), mod_map),
    ]
    args = [x2d, mod]
    if has_init:
        in_specs += [pl.BlockSpec((1, state_rows, D_STATE), lambda i: (i, 0, 0))] * 2
        args += [init_f, init_b]
    in_specs += [_const_spec(a.shape) for a in consts]
    args += list(consts)

    out_specs = [pl.BlockSpec((SLAB, D_MODEL), lambda i: (i, 0))]
    out_shape = [jax.ShapeDtypeStruct(x2d.shape, F32)]
    if not has_init:
        n_seq = n_slabs * segs_per_slab
        out_specs += [pl.BlockSpec((segs_per_slab, state_rows, D_STATE), lambda i: (i, 0, 0))] * 2
        out_shape += [jax.ShapeDtypeStruct((n_seq, state_rows, D_STATE), F32)] * 2

    scratch = [
        pltpu.VMEM((staged_rows, CONV_WIDTH), F32),
        pltpu.VMEM((staged_rows, 2 * SSD_WIDTH), F32),
        pltpu.VMEM((SLAB, CONV_WIDTH), F32),
        pltpu.VMEM((SLAB, SSD_WIDTH), F32),
        pltpu.VMEM((SLAB, LANES), F32),
        pltpu.VMEM((SLAB, LANES), F32),
        pltpu.VMEM((SLAB, SSD_WIDTH), F32),
        pltpu.VMEM((SLAB, 2 * SSD_GROUPS * D_STATE), BF16),
        pltpu.VMEM((SLAB, D_MODEL), BF16),
        pltpu.VMEM((N_CHUNKS, D_STATE, SSD_WIDTH), F32),
        pltpu.VMEM((N_CHUNKS, D_STATE, SSD_WIDTH), F32),
        pltpu.VMEM((D_STATE, SSD_WIDTH), F32),
        pltpu.VMEM((D_STATE, SSD_WIDTH), F32),
    ]
    return pl.pallas_call(
        functools.partial(_mixer_kernel, seg_chunks, has_init),
        grid=(n_slabs,),
        in_specs=in_specs,
        out_specs=out_specs,
        out_shape=out_shape,
        scratch_shapes=scratch,
        compiler_params=pltpu.CompilerParams(
            dimension_semantics=("arbitrary",), vmem_limit_bytes=VMEM_LIMIT),
        name="mixer_latent" if has_init else "mixer_prompt",
    )(*args)


def _ffn_conv(u, wc_ref, bc_ref, is_grid):
    n = u.shape[0]
    row = lax.broadcasted_iota(jnp.int32, (n, 1), 0)
    period = GRID_W if is_grid else 256
    pos = jnp.bitwise_and(row, period - 1)
    um = jnp.where(pos == 0, 0.0, pltpu.roll(u, 1, 0)).astype(BF16)
    up = jnp.where(pos == period - 1, 0.0, pltpu.roll(u, n - 1, 0)).astype(BF16)
    u = u.astype(BF16)
    wc = wc_ref[...].astype(BF16)

    def taps(di):
        return um * wc[3 * di:3 * di + 1, :] + u * wc[3 * di + 1:3 * di + 2, :] + up * wc[3 * di + 2:3 * di + 3, :]

    out = taps(1) + bc_ref[...].astype(BF16)
    if is_grid:
        zeros = jnp.zeros((GRID_W, u.shape[1]), BF16)
        out = out + jnp.concatenate([zeros, taps(0)[:n - GRID_W]], axis=0)
        out = out + jnp.concatenate([taps(2)[GRID_W:], zeros], axis=0)
    return out.astype(F32)


def _ffn_kernel(is_grid, x_ref, mod_ref, g2_ref, wup_ref, wc_ref, bc_ref, wdn_ref, gfin_ref, out_ref,
                h2_ref, acc_ref):
    j = pl.program_id(1)

    @pl.when(j == 0)
    def _():
        shift_f = mod_ref[0, 3:4, :]
        scale_f = mod_ref[0, 4:5, :]
        x = x_ref[...]
        ms = jnp.mean(x * x, axis=-1, keepdims=True)
        h2 = (x * lax.rsqrt(ms + EPS)) * g2_ref[...] * (1.0 + scale_f) + shift_f
        h2_ref[...] = h2.astype(BF16)
        acc_ref[...] = jnp.zeros_like(acc_ref)

    def up_proj(b):
        return _dot(h2_ref[...], wup_ref[b])

    def run_blocks(n_blocks):
        us = [up_proj(b) for b in range(n_blocks)]
        for b in range(n_blocks):
            u = _ffn_conv(us[b], wc_ref.at[b], bc_ref.at[b], is_grid)
            a = (_silu(u[:, 0:FF_BLOCK]) * u[:, FF_BLOCK:]).astype(BF16)
            acc_ref[...] += _dot(a, wdn_ref[b])

    n_last = N_FF_BLOCKS - (N_FF_STEPS - 1) * FF_BLOCKS_PER_STEP

    @pl.when(j < N_FF_STEPS - 1)
    def _():
        run_blocks(FF_BLOCKS_PER_STEP)

    @pl.when(j == N_FF_STEPS - 1)
    def _():
        run_blocks(n_last)

    @pl.when(j == N_FF_STEPS - 1)
    def _():
        gate_f = mod_ref[0, 5:6, :]
        x2 = x_ref[...] + gate_f * acc_ref[...]
        ms = jnp.mean(x2 * x2, axis=-1, keepdims=True)
        out_ref[...] = (x2 * lax.rsqrt(ms + EPS)) * gfin_ref[...]


def _ffn_call(x2d, mod, mod_per_slab, is_grid, consts):
    n_slabs = x2d.shape[0] // SLAB
    g2, wup3, wc3, bc3, wdn3, gfin = consts
    mod_map = (lambda i, j: (i, 0, 0)) if mod_per_slab else (lambda i, j: (0, 0, 0))
    return pl.pallas_call(
        functools.partial(_ffn_kernel, is_grid),
        grid=(n_slabs, N_FF_STEPS),
        in_specs=[
            pl.BlockSpec((SLAB, D_MODEL), lambda i, j: (i, 0)),
            pl.BlockSpec((1, N_MOD, D_MODEL), mod_map),
            pl.BlockSpec((1, D_MODEL), lambda i, j: (0, 0)),
            pl.BlockSpec((FF_BLOCKS_PER_STEP, D_MODEL, 2 * FF_BLOCK), lambda i, j: (j, 0, 0)),
            pl.BlockSpec((FF_BLOCKS_PER_STEP, 9, 2 * FF_BLOCK), lambda i, j: (j, 0, 0)),
            pl.BlockSpec((FF_BLOCKS_PER_STEP, 1, 2 * FF_BLOCK), lambda i, j: (j, 0, 0)),
            pl.BlockSpec((FF_BLOCKS_PER_STEP, FF_BLOCK, D_MODEL), lambda i, j: (j, 0, 0)),
            pl.BlockSpec((1, D_MODEL), lambda i, j: (0, 0)),
        ],
        out_specs=pl.BlockSpec((SLAB, D_MODEL), lambda i, j: (i, 0)),
        out_shape=jax.ShapeDtypeStruct(x2d.shape, F32),
        scratch_shapes=[
            pltpu.VMEM((SLAB, D_MODEL), BF16),
            pltpu.VMEM((SLAB, D_MODEL), F32),
        ],
        compiler_params=pltpu.CompilerParams(
            dimension_semantics=("arbitrary", "arbitrary"), vmem_limit_bytes=VMEM_LIMIT),
        name="ffn_latent" if is_grid else "ffn_prompt",
    )(x2d, mod, g2, wup3, wc3, bc3, wdn3, gfin)


def _pad_lanes(v):
    return jnp.pad(v.reshape(1, -1), ((0, 0), (0, LANES - v.size)))


def kernel(x_prompt, x_sample, state_ssd_fwd, state_ssd_bwd, c, c_ctx, g_norm1, g_norm2, w_ada, b_ada, w_in, w_conv_short, w_conv_ssd, b_conv_ssd, dt_bias, a_log, d_skip, g_ssd_norm, w_out, w_up, w_ffn_conv, b_ffn_conv, w_down, g_final):
    depth = w_in.shape[0]
    assert depth == 1, "kernel is written for the single-layer problem"
    bp, seq = x_prompt.shape[0], x_prompt.shape[1]
    bd, dec_seq = x_sample.shape[0], x_sample.shape[1]
    assert seq == 256 and dec_seq == SLAB and (bp * seq) % SLAB == 0

    n_c = bd + 1
    rows = -(-n_c // 8) * 8
    cvec = jnp.concatenate([c, c_ctx[None], jnp.zeros((rows - n_c, D_MODEL), F32)], axis=0)
    mod = _mod_call(cvec, w_ada[0], b_ada[0].reshape(1, -1)).reshape(rows, N_MOD, D_MODEL)
    mod_lat = mod[:bd]
    mod_ctx = mod[bd:bd + 1]

    w_in_main = _cast_bf16(w_in[0], DT_COL)
    w_dt = jnp.pad(w_in[0][:, DT_COL:], ((0, 0), (0, LANES - 2 * SSD_HEADS))).astype(BF16)
    idx = jnp.arange(CHUNK)
    ltri = (idx[:, None] >= idx[None, :]).astype(BF16)
    utri = (idx[:, None] <= idx[None, :]).astype(BF16)
    tri3 = jnp.concatenate([jnp.concatenate([ltri] * 3, axis=1), jnp.concatenate([utri] * 3, axis=1)], axis=0)
    expand = (jnp.arange(LANES)[:, None] == (jnp.arange(2 * SSD_WIDTH)[None, :] // SSD_HEAD_DIM)).astype(BF16)
    mixer_consts = (
        g_norm1[0].reshape(1, -1), w_in_main, w_dt, w_conv_short[0], w_conv_ssd[0], b_conv_ssd[0].reshape(1, -1),
        _pad_lanes(dt_bias[0]), _pad_lanes(a_log[0]),
        jnp.repeat(d_skip[0], SSD_HEAD_DIM).reshape(1, -1), g_ssd_norm[0].reshape(1, -1),
        _cast_bf16(w_out[0], D_MODEL), tri3, expand, jnp.concatenate([expand] * 3, axis=0),
    )
    wup3 = _up_blocks(w_up[0])
    pad_blocks = ((0, N_FF_PADDED - N_FF_BLOCKS), (0, 0), (0, 0))
    wc3 = jnp.pad(w_ffn_conv[0].reshape(9, 2, N_FF_BLOCKS, FF_BLOCK).transpose(2, 0, 1, 3)
                  .reshape(N_FF_BLOCKS, 9, 2 * FF_BLOCK), pad_blocks)
    bc3 = jnp.pad(b_ffn_conv[0].reshape(2, N_FF_BLOCKS, FF_BLOCK).transpose(1, 0, 2)
                  .reshape(N_FF_BLOCKS, 1, 2 * FF_BLOCK), pad_blocks)
    wdn3 = _cast_bf16(w_down[0], D_MODEL, N_FF_PADDED - N_FF_BLOCKS).reshape(N_FF_PADDED, FF_BLOCK, D_MODEL)
    ffn_consts = (g_norm2[0].reshape(1, -1), wup3, wc3, bc3, wdn3, g_final.reshape(1, -1))

    state_rows = SSD_HEADS * SSD_HEAD_DIM
    init_f = state_ssd_fwd[:, 0].reshape(bd, state_rows, D_STATE)
    init_b = state_ssd_bwd[:, 0].reshape(bd, state_rows, D_STATE)

    xp2d = x_prompt.reshape(bp * seq, D_MODEL)
    xs2d = x_sample.reshape(bd * dec_seq, D_MODEL)

    xp1, s_f, s_b = _mixer_call(xp2d, mod_ctx, False, seq, None, None, mixer_consts)
    (xs1,) = _mixer_call(xs2d, mod_lat, True, dec_seq, init_f, init_b, mixer_consts)

    y_prompt = _ffn_call(xp1, mod_ctx, False, False, ffn_consts).reshape(x_prompt.shape)
    y_sample = _ffn_call(xs1, mod_lat, True, True, ffn_consts).reshape(x_sample.shape)

    state_shape = (bp, depth, SSD_HEADS, SSD_HEAD_DIM, D_STATE)
    return (y_prompt, y_sample, s_f.reshape(state_shape), s_b.reshape(state_shape))
```

```python
import functools

import jax
import jax.numpy as jnp
from jax import lax
from jax.experimental import pallas as pl
from jax.experimental.pallas import tpu as pltpu

F32 = jnp.float32
BF16 = jnp.bfloat16

D_MODEL = 1024
GRID_W = 64
CONV_WIDTH = 512
SSD_WIDTH = 512
SSD_HEAD_DIM = 64
SSD_HEADS = 8
SSD_GROUPS = 2
D_STATE = 128
GROUP_WIDTH = SSD_WIDTH // SSD_GROUPS
BC_WIDTH = SSD_GROUPS * D_STATE
CHUNK = 128
D_FF = 2816
N_MOD = 6
PROMPT_SEQ = 256
EPS = 1e-6

SLAB = 1024
N_CHUNKS = SLAB // CHUNK
ROW_BLOCK = 256
PAD = 8
FF_BLOCK = 256
N_FF_BLOCKS = D_FF // FF_BLOCK
FF_BLOCKS_PER_STEP = 4
N_FF_STEPS = -(-N_FF_BLOCKS // FF_BLOCKS_PER_STEP)
N_FF_PADDED = N_FF_STEPS * FF_BLOCKS_PER_STEP
LANES = 128
DT_COL = 3072
VMEM_LIMIT = 58 * 1024 * 1024


def _silu(v):
    return v / (1.0 + jnp.exp(-v))


def _softplus(v):
    return jnp.maximum(v, 0.0) + jnp.log1p(jnp.exp(-jnp.abs(v)))


def _split3(v):
    hi = v.astype(BF16)
    r1 = v - hi.astype(F32)
    mid = r1.astype(BF16)
    lo = (r1 - mid.astype(F32)).astype(BF16)
    return hi, mid, lo


def _dot(a, b):
    return jnp.dot(a, b, preferred_element_type=F32)


def _dot3(m3, v):
    return _dot(m3, jnp.concatenate(_split3(v), axis=0))


def _expand3(v, expand3):
    return _dot(jnp.concatenate(_split3(v), axis=1), expand3)


def _conv3_rows(win, w_ref):
    n = win.shape[0]
    prev = pltpu.roll(win, 1, 0)[PAD:PAD + CHUNK]
    cur = win[PAD:PAD + CHUNK]
    nxt = pltpu.roll(win, n - 1, 0)[PAD:PAD + CHUNK]
    return prev * w_ref[0:1, :] + cur * w_ref[1:2, :] + nxt * w_ref[2:3, :]


def _mod_kernel(c_ref, w_ref, b_ref, o_ref):
    s = _silu(c_ref[...]).astype(BF16)
    o_ref[...] = _dot(s, w_ref[...].astype(BF16)) + b_ref[...]


def _mod_call(cvec, w_ada, b_ada):
    rows = cvec.shape[0]
    return pl.pallas_call(
        _mod_kernel,
        grid=(N_MOD,),
        in_specs=[
            pl.BlockSpec((rows, D_MODEL), lambda j: (0, 0)),
            pl.BlockSpec((D_MODEL, D_MODEL), lambda j: (0, j)),
            pl.BlockSpec((1, D_MODEL), lambda j: (0, j)),
        ],
        out_specs=pl.BlockSpec((rows, D_MODEL), lambda j: (0, j)),
        out_shape=jax.ShapeDtypeStruct((rows, N_MOD * D_MODEL), F32),
        compiler_params=pltpu.CompilerParams(dimension_semantics=("arbitrary",)),
        name="mod_vectors",
    )(cvec, w_ada, b_ada)


def _cast_kernel(n_src_blocks, src_ref, dst_ref):
    i = pl.program_id(0)

    @pl.when(i < n_src_blocks)
    def _():
        dst_ref[...] = src_ref[...].astype(BF16)

    @pl.when(i >= n_src_blocks)
    def _():
        dst_ref[...] = jnp.zeros_like(dst_ref)


def _cast_bf16(w, cols, pad_row_blocks=0):
    n_src = w.shape[0] // ROW_BLOCK
    n_dst = n_src + pad_row_blocks
    return pl.pallas_call(
        functools.partial(_cast_kernel, n_src),
        grid=(n_dst,),
        in_specs=[pl.BlockSpec((ROW_BLOCK, cols), lambda i: (jnp.minimum(i, n_src - 1), 0))],
        out_specs=pl.BlockSpec((ROW_BLOCK, cols), lambda i: (i, 0)),
        out_shape=jax.ShapeDtypeStruct((n_dst * ROW_BLOCK, cols), BF16),
        compiler_params=pltpu.CompilerParams(dimension_semantics=("arbitrary",)),
        name="cast_bf16",
    )(w)


def _up_blocks_kernel(gate_ref, value_ref, dst_ref):
    j = pl.program_id(0)

    @pl.when(j < N_FF_BLOCKS)
    def _():
        dst_ref[0, :, 0:FF_BLOCK] = gate_ref[...].astype(BF16)
        dst_ref[0, :, FF_BLOCK:] = value_ref[...].astype(BF16)

    @pl.when(j >= N_FF_BLOCKS)
    def _():
        dst_ref[...] = jnp.zeros_like(dst_ref)


def _up_blocks(w_up):
    last = N_FF_BLOCKS - 1
    return pl.pallas_call(
        _up_blocks_kernel,
        grid=(N_FF_PADDED,),
        in_specs=[
            pl.BlockSpec((D_MODEL, FF_BLOCK), lambda j: (0, jnp.minimum(j, last))),
            pl.BlockSpec((D_MODEL, FF_BLOCK), lambda j: (0, N_FF_BLOCKS + jnp.minimum(j, last))),
        ],
        out_specs=pl.BlockSpec((1, D_MODEL, 2 * FF_BLOCK), lambda j: (j, 0, 0)),
        out_shape=jax.ShapeDtypeStruct((N_FF_PADDED, D_MODEL, 2 * FF_BLOCK), BF16),
        compiler_params=pltpu.CompilerParams(dimension_semantics=("arbitrary",)),
        name="up_blocks",
    )(w_up, w_up)


def _mixer_kernel(seg_chunks, has_init, *refs):
    refs = list(refs)
    x_ref, mod_ref = refs[:2]
    refs = refs[2:]
    if has_init:
        initf_ref, initb_ref = refs[:2]
        refs = refs[2:]
    (g1_ref, win_ref, wdt_ref, wcs_ref, wcx_ref, bcx_ref, dtb_ref, alog_ref, dskip_ref, gssd_ref, wout_ref,
     tri3_ref, exp_ref, exp3_ref) = refs[:14]
    refs = refs[14:]
    out_ref = refs[0]
    refs = refs[1:]
    if not has_init:
        sfo_ref, sbo_ref = refs[:2]
        refs = refs[2:]
    (gchc_ref, xbc_ref, gb_ref, z_ref, dt_ref, e_ref, xs_ref, bc_ref, ymix_ref, sfin_ref, csb_ref,
     sf_ref, sb_ref) = refs

    shift_m = mod_ref[0, 0:1, :]
    scale_m = mod_ref[0, 1:2, :]
    gate_m = mod_ref[0, 2:3, :]

    seg_len = seg_chunks * CHUNK
    zpad_a = jnp.zeros((PAD, CONV_WIDTH), F32)
    zpad_x = jnp.zeros((PAD, 2 * SSD_WIDTH), F32)
    for s in range(SLAB // seg_len + 1):
        gap = slice(s * (seg_len + PAD), s * (seg_len + PAD) + PAD)
        gchc_ref[gap, :] = zpad_a
        xbc_ref[gap, :] = zpad_x

    def staged_row(r):
        return r + (r // seg_len) * PAD

    def proj_block(i):
        rows = slice(i * ROW_BLOCK, (i + 1) * ROW_BLOCK)
        rp = staged_row(i * ROW_BLOCK) + PAD
        staged = slice(rp, rp + ROW_BLOCK)
        xb = x_ref[rows, :]
        ms = jnp.mean(xb * xb, axis=-1, keepdims=True)
        h = (xb * lax.rsqrt(ms + EPS)) * g1_ref[...] * (1.0 + scale_m) + shift_m
        hb = h.astype(BF16)

        def proj(lo, hi):
            return _dot(hb, win_ref[:, lo:hi])

        cw = CONV_WIDTH
        gchc_ref[staged, :] = proj(0, cw) * proj(2 * cw, 3 * cw)
        gb_ref[rows, :] = proj(cw, 2 * cw)
        z_ref[rows, :] = proj(3 * cw, 3 * cw + SSD_WIDTH)
        xbc_ref[staged, :] = proj(3 * cw + SSD_WIDTH, DT_COL)
        dt_ref[rows, :] = _softplus(_dot(hb, wdt_ref[...]) + dtb_ref[...])

    a_row = -jnp.exp(alog_ref[...])
    lane_t = lax.broadcasted_iota(jnp.int32, (CHUNK, LANES), 1)
    lane_1 = lax.broadcasted_iota(jnp.int32, (1, LANES), 1)
    is_fwd_t = lane_t < SSD_HEADS
    is_fwd_1 = lane_1 < SSD_HEADS

    def chunk_decay_row(e):
        tot = jnp.where(is_fwd_1, e[CHUNK - 1:CHUNK, :], e[0:1, :])
        dec = jnp.broadcast_to(jnp.exp(tot), (8, LANES))
        return tot, _expand3(dec, exp3_ref[...])[0:1, :]

    pair_is_sequence = seg_chunks == 2
    assert pair_is_sequence != has_init and seg_chunks in (2, N_CHUNKS)

    def fwd_convs(c):
        rows = slice(c * CHUNK, (c + 1) * CHUNK)
        w0 = staged_row(c * CHUNK)
        window = slice(w0, w0 + CHUNK + 2 * PAD)
        xc = _conv3_rows(xbc_ref[window, :], wcx_ref)
        xc = _silu(xc + bcx_ref[...])
        xs_ref[rows, :] = xc[:, 0:SSD_WIDTH]
        bc_ref[rows, :] = xc[:, SSD_WIDTH:].astype(BF16)
        ca = _conv3_rows(gchc_ref[window, :], wcs_ref)
        ymix_ref[rows, 0:CONV_WIDTH] = (gb_ref[rows, :] * ca).astype(BF16)
        return xc

    def fwd_decays(c):
        rows = slice(c * CHUNK, (c + 1) * CHUNK)
        dtt = dt_ref[rows, :]
        da = dtt * a_row
        sums = _dot3(tri3_ref[...], da)
        e = jnp.where(is_fwd_t, sums[0:CHUNK], sums[CHUNK:])
        e_ref[rows, :] = e
        tot, dec_row = chunk_decay_row(e)
        wst = (dtt * jnp.exp(tot - e)).astype(BF16)
        return _dot(wst, exp_ref[...]), dec_row

    def fwd_chunk_states(xc, wexp):
        xs = xc[:, 0:SSD_WIDTH]
        xst_f = (xs * wexp[:, 0:SSD_WIDTH]).astype(BF16)
        xst_b = (xs * wexp[:, SSD_WIDTH:]).astype(BF16)
        cst_f, cst_b = [], []
        for g in range(SSD_GROUPS):
            bt = jnp.transpose(xc[:, SSD_WIDTH + g * D_STATE:SSD_WIDTH + (g + 1) * D_STATE]).astype(BF16)
            cst_f.append(_dot(bt, xst_f[:, g * GROUP_WIDTH:(g + 1) * GROUP_WIDTH]))
            cst_b.append(_dot(bt, xst_b[:, g * GROUP_WIDTH:(g + 1) * GROUP_WIDTH]))
        return jnp.concatenate(cst_f, axis=1), jnp.concatenate(cst_b, axis=1)

    if has_init:
        sf_ref[...] = jnp.transpose(initf_ref[0])
        sb_ref[...] = jnp.transpose(initb_ref[0])

    chunks_per_block = ROW_BLOCK // CHUNK
    xcs = []
    for i in range(SLAB // ROW_BLOCK):
        proj_block(i)
        if i >= 1:
            xcs += [fwd_convs(c) for c in range((i - 1) * chunks_per_block, i * chunks_per_block)]
    xcs += [fwd_convs(c) for c in range(N_CHUNKS - chunks_per_block, N_CHUNKS)]
    decays = [fwd_decays(c) for c in range(N_CHUNKS)]
    states = [fwd_chunk_states(xc, wexp) for xc, (wexp, _) in zip(xcs, decays)]
    state = None if pair_is_sequence else sf_ref[...]
    for c in range(N_CHUNKS):
        cst_f, cst_b = states[c]
        csb_ref[c] = cst_b
        s_in = None if (pair_is_sequence and c % 2 == 0) else state
        if s_in is None:
            state = cst_f
        else:
            sfin_ref[c] = s_in
            state = s_in * decays[c][1][:, 0:SSD_WIDTH] + cst_f
        if pair_is_sequence and c % 2 == 1:
            sfo_ref[c // 2] = jnp.transpose(state)

    row_t = lax.broadcasted_iota(jnp.int32, (CHUNK, CHUNK), 0)
    col_t = lax.broadcasted_iota(jnp.int32, (CHUNK, CHUNK), 1)
    causal = row_t >= col_t
    anti = col_t >= row_t
    low_half = lane_t < SSD_HEAD_DIM

    def chunk_tables(c):
        rows = slice(c * CHUNK, (c + 1) * CHUNK)
        e = e_ref[rows, :]
        xs = xs_ref[rows, :]
        dtexp = _dot(dt_ref[rows, :].astype(BF16), exp_ref[...])
        ecs = _dot(jnp.exp(e).astype(BF16), exp_ref[...])
        return dict(rows=rows, e=e, et=jnp.transpose(e), xs=xs, bc=bc_ref[rows, :], ecs=ecs,
                    xdt_f=xs * dtexp[:, 0:SSD_WIDTH], xdt_b=xs * dtexp[:, SSD_WIDTH:])

    def chunk_output(t, s_f, s_b):
        e, et, xs, bc, ecs, xdt_f, xdt_b = t["e"], t["et"], t["xs"], t["bc"], t["ecs"], t["xdt_f"], t["xdt_b"]
        y_parts = []
        for g in range(SSD_GROUPS):
            b_g = bc[:, g * D_STATE:(g + 1) * D_STATE]
            c_g = bc[:, BC_WIDTH + g * D_STATE:BC_WIDTH + (g + 1) * D_STATE]
            gmat = lax.dot_general(c_g, b_g, (((1,), (1,)), ((), ())), preferred_element_type=F32)
            cols = slice(g * GROUP_WIDTH, (g + 1) * GROUP_WIDTH)
            y_off = jnp.zeros((CHUNK, GROUP_WIDTH), F32)
            if s_f is not None:
                y_off = y_off + _dot(c_g, s_f[:, cols].astype(BF16)) * ecs[:, cols]
            if s_b is not None:
                y_off = y_off + _dot(c_g, s_b[:, cols].astype(BF16)) \
                    * ecs[:, SSD_WIDTH + g * GROUP_WIDTH:SSD_WIDTH + (g + 1) * GROUP_WIDTH]
            for pair in range(2):
                p0 = g * GROUP_WIDTH + pair * LANES
                xf_pair = xdt_f[:, p0:p0 + LANES]
                xb_pair = xdt_b[:, p0:p0 + LANES]
                scores, rhs = [], []
                for sub in range(2):
                    head = g * 4 + pair * 2 + sub
                    keep = low_half if sub == 0 else jnp.logical_not(low_half)
                    rhs.append(jnp.where(keep, xf_pair, 0.0).astype(BF16))
                    rhs.append(jnp.where(keep, xb_pair, 0.0).astype(BF16))
                    jf, jb = head, SSD_HEADS + head
                    d_f = e[:, jf:jf + 1] - et[jf:jf + 1, :]
                    d_b = e[:, jb:jb + 1] - et[jb:jb + 1, :]
                    scores.append((gmat * jnp.where(causal, jnp.exp(d_f), 0.0)).astype(BF16))
                    scores.append((gmat * jnp.where(anti, jnp.exp(d_b), 0.0)).astype(BF16))
                y_diag = _dot(jnp.concatenate(scores, axis=1), jnp.concatenate(rhs, axis=0))
                y_parts.append(y_off[:, pair * LANES:(pair + 1) * LANES] + y_diag)
        y = jnp.concatenate(y_parts, axis=1) + xs * dskip_ref[...]
        yz = y * _silu(z_ref[t["rows"], :])
        ms = jnp.mean(yz * yz, axis=-1, keepdims=True)
        yn = (yz * lax.rsqrt(ms + EPS)) * gssd_ref[...]
        ymix_ref[t["rows"], CONV_WIDTH:] = yn.astype(BF16)

    state = None if pair_is_sequence else sb_ref[...]
    for c in reversed(range(N_CHUNKS)):
        starts_sequence = pair_is_sequence and c % 2 == 0
        s_b = None if (pair_is_sequence and c % 2 == 1) else state
        t = chunk_tables(c)
        chunk_output(t, None if starts_sequence else sfin_ref[c], s_b)
        if s_b is None:
            state = csb_ref[c]
        else:
            state = s_b * chunk_decay_row(t["e"])[1][:, SSD_WIDTH:] + csb_ref[c]
        if starts_sequence:
            sbo_ref[c // 2] = jnp.transpose(state)
        if c % chunks_per_block == 0:
            rows = slice(c * CHUNK, c * CHUNK + ROW_BLOCK)
            o = _dot(ymix_ref[rows, :], wout_ref[...])
            out_ref[rows, :] = x_ref[rows, :] + gate_m * o


def _const_spec(shape):
    nd = len(shape)
    return pl.BlockSpec(shape, lambda i: (0,) * nd)


def _mixer_call(x2d, mod, mod_per_slab, seg_len, init_f, init_b, consts):
    n_slabs = x2d.shape[0] // SLAB
    seg_chunks = seg_len // CHUNK
    segs_per_slab = SLAB // seg_len
    has_init = init_f is not None
    state_rows = SSD_HEADS * SSD_HEAD_DIM
    staged_rows = SLAB + PAD * (1 + segs_per_slab)

    mod_map = (lambda i: (i, 0, 0)) if mod_per_slab else (lambda i: (0, 0, 0))
    in_specs = [
        pl.BlockSpec((SLAB, D_MODEL), lambda i: (i, 0)),
        pl.BlockSpec((1, N_MOD, D_MODEL), mod_map),
    ]
    args = [x2d, mod]
    if has_init:
        in_specs += [pl.BlockSpec((1, state_rows, D_STATE), lambda i: (i, 0, 0))] * 2
        args += [init_f, init_b]
    in_specs += [_const_spec(a.shape) for a in consts]
    args += list(consts)

    out_specs = [pl.BlockSpec((SLAB, D_MODEL), lambda i: (i, 0))]
    out_shape = [jax.ShapeDtypeStruct(x2d.shape, F32)]
    if not has_init:
        n_seq = n_slabs * segs_per_slab
        out_specs += [pl.BlockSpec((segs_per_slab, state_rows, D_STATE), lambda i: (i, 0, 0))] * 2
        out_shape += [jax.ShapeDtypeStruct((n_seq, state_rows, D_STATE), F32)] * 2

    scratch = [
        pltpu.VMEM((staged_rows, CONV_WIDTH), F32),
        pltpu.VMEM((staged_rows, 2 * SSD_WIDTH), F32),
        pltpu.VMEM((SLAB, CONV_WIDTH), F32),
        pltpu.VMEM((SLAB, SSD_WIDTH), F32),
        pltpu.VMEM((SLAB, LANES), F32),
        pltpu.VMEM((SLAB, LANES), F32),
        pltpu.VMEM((SLAB, SSD_WIDTH), F32),
        pltpu.VMEM((SLAB, 2 * SSD_GROUPS * D_STATE), BF16),
        pltpu.VMEM((SLAB, D_MODEL), BF16),
        pltpu.VMEM((N_CHUNKS, D_STATE, SSD_WIDTH), F32),
        pltpu.VMEM((N_CHUNKS, D_STATE, SSD_WIDTH), F32),
        pltpu.VMEM((D_STATE, SSD_WIDTH), F32),
        pltpu.VMEM((D_STATE, SSD_WIDTH), F32),
    ]
    return pl.pallas_call(
        functools.partial(_mixer_kernel, seg_chunks, has_init),
        grid=(n_slabs,),
        in_specs=in_specs,
        out_specs=out_specs,
        out_shape=out_shape,
        scratch_shapes=scratch,
        compiler_params=pltpu.CompilerParams(
            dimension_semantics=("arbitrary",), vmem_limit_bytes=VMEM_LIMIT),
        name="mixer_latent" if has_init else "mixer_prompt",
    )(*args)


def _ffn_conv(u, wc, bias, is_grid):
    n = u.shape[0]
    row = lax.broadcasted_iota(jnp.int32, (n, 1), 0)
    period = GRID_W if is_grid else PROMPT_SEQ
    pos = jnp.bitwise_and(row, period - 1)
    um = jnp.where(pos == 0, 0.0, pltpu.roll(u, 1, 0)).astype(BF16)
    up = jnp.where(pos == period - 1, 0.0, pltpu.roll(u, n - 1, 0)).astype(BF16)
    u = u.astype(BF16)
    wc = wc.astype(BF16)

    def taps(di):
        return um * wc[3 * di:3 * di + 1, :] + u * wc[3 * di + 1:3 * di + 2, :] + up * wc[3 * di + 2:3 * di + 3, :]

    out = taps(1) + bias.astype(BF16)
    if is_grid:
        zeros = jnp.zeros((GRID_W, u.shape[1]), BF16)
        out = out + jnp.concatenate([zeros, taps(0)[:n - GRID_W]], axis=0)
        out = out + jnp.concatenate([taps(2)[GRID_W:], zeros], axis=0)
    return out


def _ffn_kernel(is_grid, x_ref, mod_ref, g2_ref, wup_ref, wc_ref, bc_ref, wdn_ref, gfin_ref, out_ref,
                h2_ref, acc_ref):
    j = pl.program_id(1)

    @pl.when(j == 0)
    def _():
        shift_f = mod_ref[0, 3:4, :]
        scale_f = mod_ref[0, 4:5, :]
        x = x_ref[...]
        ms = jnp.mean(x * x, axis=-1, keepdims=True)
        h2 = (x * lax.rsqrt(ms + EPS)) * g2_ref[...] * (1.0 + scale_f) + shift_f
        h2_ref[...] = h2.astype(BF16)
        acc_ref[...] = jnp.zeros_like(acc_ref)

    def up_proj(b):
        return _dot(h2_ref[...], wup_ref[b])

    def run_blocks(n_blocks):
        us = [up_proj(b) for b in range(n_blocks)]
        for b in range(n_blocks):
            gated = []
            for t in range(FF_BLOCK // LANES):
                gate_cols = slice(t * LANES, (t + 1) * LANES)
                value_cols = slice(FF_BLOCK + t * LANES, FF_BLOCK + (t + 1) * LANES)
                ug = _ffn_conv(us[b][:, gate_cols], wc_ref[b, :, gate_cols], bc_ref[b, :, gate_cols], is_grid)
                uv = _ffn_conv(us[b][:, value_cols], wc_ref[b, :, value_cols], bc_ref[b, :, value_cols], is_grid)
                gated.append((_silu(ug) * uv).astype(BF16))
            acc_ref[...] += _dot(jnp.concatenate(gated, axis=1), wdn_ref[b])

    n_last = N_FF_BLOCKS - (N_FF_STEPS - 1) * FF_BLOCKS_PER_STEP

    @pl.when(j < N_FF_STEPS - 1)
    def _():
        run_blocks(FF_BLOCKS_PER_STEP)

    @pl.when(j == N_FF_STEPS - 1)
    def _():
        run_blocks(n_last)

    @pl.when(j == N_FF_STEPS - 1)
    def _():
        gate_f = mod_ref[0, 5:6, :]
        x2 = x_ref[...] + gate_f * acc_ref[...]
        ms = jnp.mean(x2 * x2, axis=-1, keepdims=True)
        out_ref[...] = (x2 * lax.rsqrt(ms + EPS)) * gfin_ref[...]


def _ffn_call(x2d, mod, mod_per_slab, is_grid, consts):
    n_slabs = x2d.shape[0] // SLAB
    g2, wup3, wc3, bc3, wdn3, gfin = consts
    mod_map = (lambda i, j: (i, 0, 0)) if mod_per_slab else (lambda i, j: (0, 0, 0))
    return pl.pallas_call(
        functools.partial(_ffn_kernel, is_grid),
        grid=(n_slabs, N_FF_STEPS),
        in_specs=[
            pl.BlockSpec((SLAB, D_MODEL), lambda i, j: (i, 0)),
            pl.BlockSpec((1, N_MOD, D_MODEL), mod_map),
            pl.BlockSpec((1, D_MODEL), lambda i, j: (0, 0)),
            pl.BlockSpec((FF_BLOCKS_PER_STEP, D_MODEL, 2 * FF_BLOCK), lambda i, j: (j, 0, 0)),
            pl.BlockSpec((FF_BLOCKS_PER_STEP, 9, 2 * FF_BLOCK), lambda i, j: (j, 0, 0)),
            pl.BlockSpec((FF_BLOCKS_PER_STEP, 1, 2 * FF_BLOCK), lambda i, j: (j, 0, 0)),
            pl.BlockSpec((FF_BLOCKS_PER_STEP, FF_BLOCK, D_MODEL), lambda i, j: (j, 0, 0)),
            pl.BlockSpec((1, D_MODEL), lambda i, j: (0, 0)),
        ],
        out_specs=pl.BlockSpec((SLAB, D_MODEL), lambda i, j: (i, 0)),
        out_shape=jax.ShapeDtypeStruct(x2d.shape, F32),
        scratch_shapes=[
            pltpu.VMEM((SLAB, D_MODEL), BF16),
            pltpu.VMEM((SLAB, D_MODEL), F32),
        ],
        compiler_params=pltpu.CompilerParams(
            dimension_semantics=("arbitrary", "arbitrary"), vmem_limit_bytes=VMEM_LIMIT),
        name="ffn_latent" if is_grid else "ffn_prompt",
    )(x2d, mod, g2, wup3, wc3, bc3, wdn3, gfin)


def _pad_lanes(v):
    return jnp.pad(v.reshape(1, -1), ((0, 0), (0, LANES - v.size)))


def kernel(x_prompt, x_sample, state_ssd_fwd, state_ssd_bwd, c, c_ctx, g_norm1, g_norm2, w_ada, b_ada, w_in, w_conv_short, w_conv_ssd, b_conv_ssd, dt_bias, a_log, d_skip, g_ssd_norm, w_out, w_up, w_ffn_conv, b_ffn_conv, w_down, g_final):
    depth = w_in.shape[0]
    assert depth == 1, "kernel is written for the single-layer problem"
    bp, seq = x_prompt.shape[0], x_prompt.shape[1]
    bd, dec_seq = x_sample.shape[0], x_sample.shape[1]
    assert seq == PROMPT_SEQ and dec_seq == SLAB and (bp * seq) % SLAB == 0

    n_c = bd + 1
    rows = -(-n_c // 8) * 8
    cvec = jnp.concatenate([c, c_ctx[None], jnp.zeros((rows - n_c, D_MODEL), F32)], axis=0)
    mod = _mod_call(cvec, w_ada[0], b_ada[0].reshape(1, -1)).reshape(rows, N_MOD, D_MODEL)
    mod_lat = mod[:bd]
    mod_ctx = mod[bd:bd + 1]

    w_in_main = _cast_bf16(w_in[0], DT_COL)
    w_dt = jnp.pad(w_in[0][:, DT_COL:], ((0, 0), (0, LANES - 2 * SSD_HEADS))).astype(BF16)
    idx = jnp.arange(CHUNK)
    ltri = (idx[:, None] >= idx[None, :]).astype(BF16)
    utri = (idx[:, None] <= idx[None, :]).astype(BF16)
    tri3 = jnp.concatenate([jnp.concatenate([ltri] * 3, axis=1), jnp.concatenate([utri] * 3, axis=1)], axis=0)
    expand = (jnp.arange(LANES)[:, None] == (jnp.arange(2 * SSD_WIDTH)[None, :] // SSD_HEAD_DIM)).astype(BF16)
    mixer_consts = (
        g_norm1[0].reshape(1, -1), w_in_main, w_dt, w_conv_short[0], w_conv_ssd[0], b_conv_ssd[0].reshape(1, -1),
        _pad_lanes(dt_bias[0]), _pad_lanes(a_log[0]),
        jnp.repeat(d_skip[0], SSD_HEAD_DIM).reshape(1, -1), g_ssd_norm[0].reshape(1, -1),
        _cast_bf16(w_out[0], D_MODEL), tri3, expand, jnp.concatenate([expand] * 3, axis=0),
    )
    wup3 = _up_blocks(w_up[0])
    pad_blocks = ((0, N_FF_PADDED - N_FF_BLOCKS), (0, 0), (0, 0))
    wc3 = jnp.pad(w_ffn_conv[0].reshape(9, 2, N_FF_BLOCKS, FF_BLOCK).transpose(2, 0, 1, 3)
                  .reshape(N_FF_BLOCKS, 9, 2 * FF_BLOCK), pad_blocks)
    bc3 = jnp.pad(b_ffn_conv[0].reshape(2, N_FF_BLOCKS, FF_BLOCK).transpose(1, 0, 2)
                  .reshape(N_FF_BLOCKS, 1, 2 * FF_BLOCK), pad_blocks)
    wdn3 = _cast_bf16(w_down[0], D_MODEL, N_FF_PADDED - N_FF_BLOCKS).reshape(N_FF_PADDED, FF_BLOCK, D_MODEL)
    ffn_consts = (g_norm2[0].reshape(1, -1), wup3, wc3, bc3, wdn3, g_final.reshape(1, -1))

    state_rows = SSD_HEADS * SSD_HEAD_DIM
    init_f = state_ssd_fwd[:, 0].reshape(bd, state_rows, D_STATE)
    init_b = state_ssd_bwd[:, 0].reshape(bd, state_rows, D_STATE)

    xp2d = x_prompt.reshape(bp * seq, D_MODEL)
    xs2d = x_sample.reshape(bd * dec_seq, D_MODEL)

    xp1, s_f, s_b = _mixer_call(xp2d, mod_ctx, False, seq, None, None, mixer_consts)
    (xs1,) = _mixer_call(xs2d, mod_lat, True, dec_seq, init_f, init_b, mixer_consts)

    y_prompt = _ffn_call(xp1, mod_ctx, False, False, ffn_consts).reshape(x_prompt.shape)
    y_sample = _ffn_call(xs1, mod_lat, True, True, ffn_consts).reshape(x_sample.shape)

    state_shape = (bp, depth, SSD_HEADS, SSD_HEAD_DIM, D_STATE)
    return (y_prompt, y_sample, s_f.reshape(state_shape), s_b.reshape(state_shape))
```

```python
import functools

import jax
import jax.numpy as jnp
from jax import lax
from jax.experimental import pallas as pl
from jax.experimental.pallas import tpu as pltpu

F32 = jnp.float32
BF16 = jnp.bfloat16

D_MODEL = 1024
GRID_W = 64
CONV_WIDTH = 512
SSD_WIDTH = 512
SSD_HEAD_DIM = 64
SSD_HEADS = 8
SSD_GROUPS = 2
D_STATE = 128
GROUP_WIDTH = SSD_WIDTH // SSD_GROUPS
BC_WIDTH = SSD_GROUPS * D_STATE
CHUNK = 128
D_FF = 2816
N_MOD = 6
PROMPT_SEQ = 256
EPS = 1e-6

SLAB = 1024
N_CHUNKS = SLAB // CHUNK
ROW_BLOCK = 256
PAD = 8
FF_BLOCK = 256
N_FF_BLOCKS = D_FF // FF_BLOCK
FF_BLOCKS_PER_STEP = 4
N_FF_STEPS = -(-N_FF_BLOCKS // FF_BLOCKS_PER_STEP)
N_FF_PADDED = N_FF_STEPS * FF_BLOCKS_PER_STEP
LANES = 128
DT_COL = 3072
VMEM_LIMIT = 58 * 1024 * 1024


def _silu(v):
    return v / (1.0 + jnp.exp(-v))


def _softplus(v):
    return jnp.maximum(v, 0.0) + jnp.log1p(jnp.exp(-jnp.abs(v)))


def _split3(v):
    hi = v.astype(BF16)
    r1 = v - hi.astype(F32)
    mid = r1.astype(BF16)
    lo = (r1 - mid.astype(F32)).astype(BF16)
    return hi, mid, lo


def _dot(a, b):
    return jnp.dot(a, b, preferred_element_type=F32)


def _dot3(m3, v):
    return _dot(m3, jnp.concatenate(_split3(v), axis=0))


def _expand3(v, expand3):
    return _dot(jnp.concatenate(_split3(v), axis=1), expand3)


def _conv3_rows(win, w_ref):
    n = win.shape[0]
    prev = pltpu.roll(win, 1, 0)[PAD:PAD + CHUNK]
    cur = win[PAD:PAD + CHUNK]
    nxt = pltpu.roll(win, n - 1, 0)[PAD:PAD + CHUNK]
    return prev * w_ref[0:1, :] + cur * w_ref[1:2, :] + nxt * w_ref[2:3, :]


def _mod_kernel(c_ref, w_ref, b_ref, o_ref):
    s = _silu(c_ref[...]).astype(BF16)
    o_ref[...] = _dot(s, w_ref[...].astype(BF16)) + b_ref[...]


def _mod_call(cvec, w_ada, b_ada):
    rows = cvec.shape[0]
    return pl.pallas_call(
        _mod_kernel,
        grid=(N_MOD,),
        in_specs=[
            pl.BlockSpec((rows, D_MODEL), lambda j: (0, 0)),
            pl.BlockSpec((D_MODEL, D_MODEL), lambda j: (0, j)),
            pl.BlockSpec((1, D_MODEL), lambda j: (0, j)),
        ],
        out_specs=pl.BlockSpec((rows, D_MODEL), lambda j: (0, j)),
        out_shape=jax.ShapeDtypeStruct((rows, N_MOD * D_MODEL), F32),
        compiler_params=pltpu.CompilerParams(dimension_semantics=("arbitrary",)),
        name="mod_vectors",
    )(cvec, w_ada, b_ada)


def _cast_kernel(n_src_blocks, src_ref, dst_ref):
    i = pl.program_id(0)

    @pl.when(i < n_src_blocks)
    def _():
        dst_ref[...] = src_ref[...].astype(BF16)

    @pl.when(i >= n_src_blocks)
    def _():
        dst_ref[...] = jnp.zeros_like(dst_ref)


def _cast_bf16(w, cols, pad_row_blocks=0):
    n_src = w.shape[0] // ROW_BLOCK
    n_dst = n_src + pad_row_blocks
    return pl.pallas_call(
        functools.partial(_cast_kernel, n_src),
        grid=(n_dst,),
        in_specs=[pl.BlockSpec((ROW_BLOCK, cols), lambda i: (jnp.minimum(i, n_src - 1), 0))],
        out_specs=pl.BlockSpec((ROW_BLOCK, cols), lambda i: (i, 0)),
        out_shape=jax.ShapeDtypeStruct((n_dst * ROW_BLOCK, cols), BF16),
        compiler_params=pltpu.CompilerParams(dimension_semantics=("arbitrary",)),
        name="cast_bf16",
    )(w)


def _ffn_blocks_kernel(gate_ref, value_ref, down_ref, up_dst_ref, down_dst_ref):
    j = pl.program_id(0)

    @pl.when(j < N_FF_BLOCKS)
    def _():
        up_dst_ref[0, :, 0:FF_BLOCK] = gate_ref[...].astype(BF16)
        up_dst_ref[0, :, FF_BLOCK:] = value_ref[...].astype(BF16)
        down_dst_ref[0] = down_ref[...].astype(BF16)

    @pl.when(j >= N_FF_BLOCKS)
    def _():
        up_dst_ref[...] = jnp.zeros_like(up_dst_ref)
        down_dst_ref[...] = jnp.zeros_like(down_dst_ref)


def _ffn_blocks(w_up, w_down):
    last = N_FF_BLOCKS - 1
    return pl.pallas_call(
        _ffn_blocks_kernel,
        grid=(N_FF_PADDED,),
        in_specs=[
            pl.BlockSpec((D_MODEL, FF_BLOCK), lambda j: (0, jnp.minimum(j, last))),
            pl.BlockSpec((D_MODEL, FF_BLOCK), lambda j: (0, N_FF_BLOCKS + jnp.minimum(j, last))),
            pl.BlockSpec((FF_BLOCK, D_MODEL), lambda j: (jnp.minimum(j, last), 0)),
        ],
        out_specs=[
            pl.BlockSpec((1, D_MODEL, 2 * FF_BLOCK), lambda j: (j, 0, 0)),
            pl.BlockSpec((1, FF_BLOCK, D_MODEL), lambda j: (j, 0, 0)),
        ],
        out_shape=[
            jax.ShapeDtypeStruct((N_FF_PADDED, D_MODEL, 2 * FF_BLOCK), BF16),
            jax.ShapeDtypeStruct((N_FF_PADDED, FF_BLOCK, D_MODEL), BF16),
        ],
        compiler_params=pltpu.CompilerParams(dimension_semantics=("arbitrary",)),
        name="ffn_blocks",
    )(w_up, w_up, w_down)


def _mixer_kernel(seg_chunks, has_init, *refs):
    refs = list(refs)
    x_ref, mod_ref = refs[:2]
    refs = refs[2:]
    if has_init:
        initf_ref, initb_ref = refs[:2]
        refs = refs[2:]
    (g1_ref, win_ref, wdt_ref, wcs_ref, wcx_ref, bcx_ref, dtb_ref, alog_ref, dskip_ref, gssd_ref, wout_ref,
     tri3_ref, exp_ref, exp3_ref) = refs[:14]
    refs = refs[14:]
    out_ref = refs[0]
    refs = refs[1:]
    if not has_init:
        sfo_ref, sbo_ref = refs[:2]
        refs = refs[2:]
    (gchc_ref, xbc_ref, gb_ref, z_ref, dt_ref, e_ref, xs_ref, bc_ref, ymix_ref, sfin_ref, csb_ref,
     sf_ref, sb_ref) = refs

    shift_m = mod_ref[0, 0:1, :]
    scale_m = mod_ref[0, 1:2, :]
    gate_m = mod_ref[0, 2:3, :]

    seg_len = seg_chunks * CHUNK
    zpad_a = jnp.zeros((PAD, CONV_WIDTH), F32)
    zpad_x = jnp.zeros((PAD, 2 * SSD_WIDTH), F32)
    for s in range(SLAB // seg_len + 1):
        gap = slice(s * (seg_len + PAD), s * (seg_len + PAD) + PAD)
        gchc_ref[gap, :] = zpad_a
        xbc_ref[gap, :] = zpad_x

    def staged_row(r):
        return r + (r // seg_len) * PAD

    def proj_block(i):
        rows = slice(i * ROW_BLOCK, (i + 1) * ROW_BLOCK)
        rp = staged_row(i * ROW_BLOCK) + PAD
        staged = slice(rp, rp + ROW_BLOCK)
        xb = x_ref[rows, :]
        ms = jnp.mean(xb * xb, axis=-1, keepdims=True)
        h = (xb * lax.rsqrt(ms + EPS)) * g1_ref[...] * (1.0 + scale_m) + shift_m
        hb = h.astype(BF16)

        def proj(lo, hi):
            return _dot(hb, win_ref[:, lo:hi])

        cw = CONV_WIDTH
        gchc_ref[staged, :] = proj(0, cw) * proj(2 * cw, 3 * cw)
        gb_ref[rows, :] = proj(cw, 2 * cw)
        z_ref[rows, :] = proj(3 * cw, 3 * cw + SSD_WIDTH)
        xbc_ref[staged, :] = proj(3 * cw + SSD_WIDTH, DT_COL)
        dt_ref[rows, :] = _softplus(_dot(hb, wdt_ref[...]) + dtb_ref[...])

    a_row = -jnp.exp(alog_ref[...])
    lane_t = lax.broadcasted_iota(jnp.int32, (CHUNK, LANES), 1)
    lane_1 = lax.broadcasted_iota(jnp.int32, (1, LANES), 1)
    is_fwd_t = lane_t < SSD_HEADS
    is_fwd_1 = lane_1 < SSD_HEADS

    def chunk_decay_row(e):
        tot = jnp.where(is_fwd_1, e[CHUNK - 1:CHUNK, :], e[0:1, :])
        dec = jnp.broadcast_to(jnp.exp(tot), (8, LANES))
        return tot, _expand3(dec, exp3_ref[...])[0:1, :]

    pair_is_sequence = seg_chunks == 2
    assert pair_is_sequence != has_init and seg_chunks in (2, N_CHUNKS)

    def fwd_convs(c):
        rows = slice(c * CHUNK, (c + 1) * CHUNK)
        w0 = staged_row(c * CHUNK)
        window = slice(w0, w0 + CHUNK + 2 * PAD)
        xc = _conv3_rows(xbc_ref[window, :], wcx_ref)
        xc = _silu(xc + bcx_ref[...])
        xs_ref[rows, :] = xc[:, 0:SSD_WIDTH]
        bc_ref[rows, :] = xc[:, SSD_WIDTH:].astype(BF16)
        ca = _conv3_rows(gchc_ref[window, :], wcs_ref)
        ymix_ref[rows, 0:CONV_WIDTH] = (gb_ref[rows, :] * ca).astype(BF16)
        return xc

    def fwd_decays(c):
        rows = slice(c * CHUNK, (c + 1) * CHUNK)
        dtt = dt_ref[rows, :]
        da = dtt * a_row
        sums = _dot3(tri3_ref[...], da)
        e = jnp.where(is_fwd_t, sums[0:CHUNK], sums[CHUNK:])
        e_ref[rows, :] = e
        tot, dec_row = chunk_decay_row(e)
        wst = (dtt * jnp.exp(tot - e)).astype(BF16)
        return _dot(wst, exp_ref[...]), dec_row

    def fwd_chunk_states(xc, wexp):
        xs = xc[:, 0:SSD_WIDTH]
        xst_f = (xs * wexp[:, 0:SSD_WIDTH]).astype(BF16)
        xst_b = (xs * wexp[:, SSD_WIDTH:]).astype(BF16)
        cst_f, cst_b = [], []
        for g in range(SSD_GROUPS):
            bt = jnp.transpose(xc[:, SSD_WIDTH + g * D_STATE:SSD_WIDTH + (g + 1) * D_STATE]).astype(BF16)
            cst_f.append(_dot(bt, xst_f[:, g * GROUP_WIDTH:(g + 1) * GROUP_WIDTH]))
            cst_b.append(_dot(bt, xst_b[:, g * GROUP_WIDTH:(g + 1) * GROUP_WIDTH]))
        return jnp.concatenate(cst_f, axis=1), jnp.concatenate(cst_b, axis=1)

    if has_init:
        sf_ref[...] = jnp.transpose(initf_ref[0])
        sb_ref[...] = jnp.transpose(initb_ref[0])

    chunks_per_block = ROW_BLOCK // CHUNK
    xcs = []
    for i in range(SLAB // ROW_BLOCK):
        proj_block(i)
        if i >= 1:
            xcs += [fwd_convs(c) for c in range((i - 1) * chunks_per_block, i * chunks_per_block)]
    xcs += [fwd_convs(c) for c in range(N_CHUNKS - chunks_per_block, N_CHUNKS)]
    decays = [fwd_decays(c) for c in range(N_CHUNKS)]
    states = [fwd_chunk_states(xc, wexp) for xc, (wexp, _) in zip(xcs, decays)]
    state = None if pair_is_sequence else sf_ref[...]
    for c in range(N_CHUNKS):
        cst_f, cst_b = states[c]
        csb_ref[c] = cst_b
        s_in = None if (pair_is_sequence and c % 2 == 0) else state
        if s_in is None:
            state = cst_f
        else:
            sfin_ref[c] = s_in
            state = s_in * decays[c][1][:, 0:SSD_WIDTH] + cst_f
        if pair_is_sequence and c % 2 == 1:
            sfo_ref[c // 2] = jnp.transpose(state)

    row_t = lax.broadcasted_iota(jnp.int32, (CHUNK, CHUNK), 0)
    col_t = lax.broadcasted_iota(jnp.int32, (CHUNK, CHUNK), 1)
    causal = row_t >= col_t
    anti = col_t >= row_t
    low_half = lane_t < SSD_HEAD_DIM

    def chunk_tables(c):
        rows = slice(c * CHUNK, (c + 1) * CHUNK)
        e = e_ref[rows, :]
        xs = xs_ref[rows, :]
        dtexp = _dot(dt_ref[rows, :].astype(BF16), exp_ref[...])
        ecs = _dot(jnp.exp(e).astype(BF16), exp_ref[...])
        return dict(rows=rows, e=e, et=jnp.transpose(e), xs=xs, bc=bc_ref[rows, :], ecs=ecs,
                    xdt_f=xs * dtexp[:, 0:SSD_WIDTH], xdt_b=xs * dtexp[:, SSD_WIDTH:])

    def chunk_output(t, s_f, s_b):
        e, et, xs, bc, ecs, xdt_f, xdt_b = t["e"], t["et"], t["xs"], t["bc"], t["ecs"], t["xdt_f"], t["xdt_b"]
        y_parts = []
        for g in range(SSD_GROUPS):
            b_g = bc[:, g * D_STATE:(g + 1) * D_STATE]
            c_g = bc[:, BC_WIDTH + g * D_STATE:BC_WIDTH + (g + 1) * D_STATE]
            gmat = lax.dot_general(c_g, b_g, (((1,), (1,)), ((), ())), preferred_element_type=F32)
            cols = slice(g * GROUP_WIDTH, (g + 1) * GROUP_WIDTH)
            y_off = jnp.zeros((CHUNK, GROUP_WIDTH), F32)
            if s_f is not None:
                y_off = y_off + _dot(c_g, s_f[:, cols].astype(BF16)) * ecs[:, cols]
            if s_b is not None:
                y_off = y_off + _dot(c_g, s_b[:, cols].astype(BF16)) \
                    * ecs[:, SSD_WIDTH + g * GROUP_WIDTH:SSD_WIDTH + (g + 1) * GROUP_WIDTH]
            for pair in range(2):
                p0 = g * GROUP_WIDTH + pair * LANES
                xf_pair = xdt_f[:, p0:p0 + LANES]
                xb_pair = xdt_b[:, p0:p0 + LANES]
                scores, rhs = [], []
                for sub in range(2):
                    head = g * 4 + pair * 2 + sub
                    keep = low_half if sub == 0 else jnp.logical_not(low_half)
                    rhs.append(jnp.where(keep, xf_pair, 0.0).astype(BF16))
                    rhs.append(jnp.where(keep, xb_pair, 0.0).astype(BF16))
                    jf, jb = head, SSD_HEADS + head
                    d_f = e[:, jf:jf + 1] - et[jf:jf + 1, :]
                    d_b = e[:, jb:jb + 1] - et[jb:jb + 1, :]
                    scores.append((gmat * jnp.where(causal, jnp.exp(d_f), 0.0)).astype(BF16))
                    scores.append((gmat * jnp.where(anti, jnp.exp(d_b), 0.0)).astype(BF16))
                y_diag = _dot(jnp.concatenate(scores, axis=1), jnp.concatenate(rhs, axis=0))
                y_parts.append(y_off[:, pair * LANES:(pair + 1) * LANES] + y_diag)
        y = jnp.concatenate(y_parts, axis=1) + xs * dskip_ref[...]
        yz = y * _silu(z_ref[t["rows"], :])
        ms = jnp.mean(yz * yz, axis=-1, keepdims=True)
        yn = (yz * lax.rsqrt(ms + EPS)) * gssd_ref[...]
        ymix_ref[t["rows"], CONV_WIDTH:] = yn.astype(BF16)

    state = None if pair_is_sequence else sb_ref[...]
    for c in reversed(range(N_CHUNKS)):
        starts_sequence = pair_is_sequence and c % 2 == 0
        s_b = None if (pair_is_sequence and c % 2 == 1) else state
        t = chunk_tables(c)
        chunk_output(t, None if starts_sequence else sfin_ref[c], s_b)
        if s_b is None:
            state = csb_ref[c]
        else:
            state = s_b * chunk_decay_row(t["e"])[1][:, SSD_WIDTH:] + csb_ref[c]
        if starts_sequence:
            sbo_ref[c // 2] = jnp.transpose(state)
        if c % chunks_per_block == 0:
            rows = slice(c * CHUNK, c * CHUNK + ROW_BLOCK)
            o = _dot(ymix_ref[rows, :], wout_ref[...])
            out_ref[rows, :] = x_ref[rows, :] + gate_m * o


def _const_spec(shape):
    nd = len(shape)
    return pl.BlockSpec(shape, lambda i: (0,) * nd)


def _mixer_call(x2d, mod, mod_per_slab, seg_len, init_f, init_b, consts):
    n_slabs = x2d.shape[0] // SLAB
    seg_chunks = seg_len // CHUNK
    segs_per_slab = SLAB // seg_len
    has_init = init_f is not None
    state_rows = SSD_HEADS * SSD_HEAD_DIM
    staged_rows = SLAB + PAD * (1 + segs_per_slab)

    mod_map = (lambda i: (i, 0, 0)) if mod_per_slab else (lambda i: (0, 0, 0))
    in_specs = [
        pl.BlockSpec((SLAB, D_MODEL), lambda i: (i, 0)),
        pl.BlockSpec((1, N_MOD, D_MODEL), mod_map),
    ]
    args = [x2d, mod]
    if has_init:
        in_specs += [pl.BlockSpec((1, state_rows, D_STATE), lambda i: (i, 0, 0))] * 2
        args += [init_f, init_b]
    in_specs += [_const_spec(a.shape) for a in consts]
    args += list(consts)

    out_specs = [pl.BlockSpec((SLAB, D_MODEL), lambda i: (i, 0))]
    out_shape = [jax.ShapeDtypeStruct(x2d.shape, F32)]
    if not has_init:
        n_seq = n_slabs * segs_per_slab
        out_specs += [pl.BlockSpec((segs_per_slab, state_rows, D_STATE), lambda i: (i, 0, 0))] * 2
        out_shape += [jax.ShapeDtypeStruct((n_seq, state_rows, D_STATE), F32)] * 2

    scratch = [
        pltpu.VMEM((staged_rows, CONV_WIDTH), F32),
        pltpu.VMEM((staged_rows, 2 * SSD_WIDTH), F32),
        pltpu.VMEM((SLAB, CONV_WIDTH), F32),
        pltpu.VMEM((SLAB, SSD_WIDTH), F32),
        pltpu.VMEM((SLAB, LANES), F32),
        pltpu.VMEM((SLAB, LANES), F32),
        pltpu.VMEM((SLAB, SSD_WIDTH), F32),
        pltpu.VMEM((SLAB, 2 * SSD_GROUPS * D_STATE), BF16),
        pltpu.VMEM((SLAB, D_MODEL), BF16),
        pltpu.VMEM((N_CHUNKS, D_STATE, SSD_WIDTH), F32),
        pltpu.VMEM((N_CHUNKS, D_STATE, SSD_WIDTH), F32),
        pltpu.VMEM((D_STATE, SSD_WIDTH), F32),
        pltpu.VMEM((D_STATE, SSD_WIDTH), F32),
    ]
    return pl.pallas_call(
        functools.partial(_mixer_kernel, seg_chunks, has_init),
        grid=(n_slabs,),
        in_specs=in_specs,
        out_specs=out_specs,
        out_shape=out_shape,
        scratch_shapes=scratch,
        compiler_params=pltpu.CompilerParams(
            dimension_semantics=("arbitrary",), vmem_limit_bytes=VMEM_LIMIT),
        name="mixer_latent" if has_init else "mixer_prompt",
    )(*args)


def _ffn_conv(u, wc, bias, is_grid):
    n = u.shape[0]
    row = lax.broadcasted_iota(jnp.int32, (n, 1), 0)
    period = GRID_W if is_grid else PROMPT_SEQ
    pos = jnp.bitwise_and(row, period - 1)
    um = jnp.where(pos == 0, 0.0, pltpu.roll(u, 1, 0)).astype(BF16)
    up = jnp.where(pos == period - 1, 0.0, pltpu.roll(u, n - 1, 0)).astype(BF16)
    u = u.astype(BF16)
    wc = wc.astype(BF16)

    def taps(di):
        return um * wc[3 * di:3 * di + 1, :] + u * wc[3 * di + 1:3 * di + 2, :] + up * wc[3 * di + 2:3 * di + 3, :]

    out = taps(1) + bias.astype(BF16)
    if is_grid:
        zeros = jnp.zeros((GRID_W, u.shape[1]), BF16)
        out = out + jnp.concatenate([zeros, taps(0)[:n - GRID_W]], axis=0)
        out = out + jnp.concatenate([taps(2)[GRID_W:], zeros], axis=0)
    return out


def _ffn_kernel(is_grid, x_ref, mod_ref, g2_ref, wup_ref, wc_ref, bc_ref, wdn_ref, gfin_ref, out_ref,
                h2_ref, acc_ref):
    j = pl.program_id(1)

    @pl.when(j == 0)
    def _():
        shift_f = mod_ref[0, 3:4, :]
        scale_f = mod_ref[0, 4:5, :]
        x = x_ref[...]
        ms = jnp.mean(x * x, axis=-1, keepdims=True)
        h2 = (x * lax.rsqrt(ms + EPS)) * g2_ref[...] * (1.0 + scale_f) + shift_f
        h2_ref[...] = h2.astype(BF16)
        acc_ref[...] = jnp.zeros_like(acc_ref)

    def up_proj(b):
        return _dot(h2_ref[...], wup_ref[b])

    def run_blocks(n_blocks):
        us = [up_proj(b) for b in range(n_blocks)]
        for b in range(n_blocks):
            gated = []
            for t in range(FF_BLOCK // LANES):
                gate_cols = slice(t * LANES, (t + 1) * LANES)
                value_cols = slice(FF_BLOCK + t * LANES, FF_BLOCK + (t + 1) * LANES)
                ug = _ffn_conv(us[b][:, gate_cols], wc_ref[b, :, gate_cols], bc_ref[b, :, gate_cols], is_grid)
                uv = _ffn_conv(us[b][:, value_cols], wc_ref[b, :, value_cols], bc_ref[b, :, value_cols], is_grid)
                gated.append((_silu(ug) * uv).astype(BF16))
            acc_ref[...] += _dot(jnp.concatenate(gated, axis=1), wdn_ref[b])

    n_last = N_FF_BLOCKS - (N_FF_STEPS - 1) * FF_BLOCKS_PER_STEP

    @pl.when(j < N_FF_STEPS - 1)
    def _():
        run_blocks(FF_BLOCKS_PER_STEP)

    @pl.when(j == N_FF_STEPS - 1)
    def _():
        run_blocks(n_last)

    @pl.when(j == N_FF_STEPS - 1)
    def _():
        gate_f = mod_ref[0, 5:6, :]
        x2 = x_ref[...] + gate_f * acc_ref[...]
        ms = jnp.mean(x2 * x2, axis=-1, keepdims=True)
        out_ref[...] = (x2 * lax.rsqrt(ms + EPS)) * gfin_ref[...]


def _ffn_call(x2d, mod, mod_per_slab, is_grid, consts):
    n_slabs = x2d.shape[0] // SLAB
    g2, wup3, wc3, bc3, wdn3, gfin = consts
    mod_map = (lambda i, j: (i, 0, 0)) if mod_per_slab else (lambda i, j: (0, 0, 0))
    return pl.pallas_call(
        functools.partial(_ffn_kernel, is_grid),
        grid=(n_slabs, N_FF_STEPS),
        in_specs=[
            pl.BlockSpec((SLAB, D_MODEL), lambda i, j: (i, 0)),
            pl.BlockSpec((1, N_MOD, D_MODEL), mod_map),
            pl.BlockSpec((1, D_MODEL), lambda i, j: (0, 0)),
            pl.BlockSpec((FF_BLOCKS_PER_STEP, D_MODEL, 2 * FF_BLOCK), lambda i, j: (j, 0, 0)),
            pl.BlockSpec((FF_BLOCKS_PER_STEP, 9, 2 * FF_BLOCK), lambda i, j: (j, 0, 0)),
            pl.BlockSpec((FF_BLOCKS_PER_STEP, 1, 2 * FF_BLOCK), lambda i, j: (j, 0, 0)),
            pl.BlockSpec((FF_BLOCKS_PER_STEP, FF_BLOCK, D_MODEL), lambda i, j: (j, 0, 0)),
            pl.BlockSpec((1, D_MODEL), lambda i, j: (0, 0)),
        ],
        out_specs=pl.BlockSpec((SLAB, D_MODEL), lambda i, j: (i, 0)),
        out_shape=jax.ShapeDtypeStruct(x2d.shape, F32),
        scratch_shapes=[
            pltpu.VMEM((SLAB, D_MODEL), BF16),
            pltpu.VMEM((SLAB, D_MODEL), F32),
        ],
        compiler_params=pltpu.CompilerParams(
            dimension_semantics=("arbitrary", "arbitrary"), vmem_limit_bytes=VMEM_LIMIT),
        name="ffn_latent" if is_grid else "ffn_prompt",
    )(x2d, mod, g2, wup3, wc3, bc3, wdn3, gfin)


def _pad_lanes(v):
    return jnp.pad(v.reshape(1, -1), ((0, 0), (0, LANES - v.size)))


def kernel(x_prompt, x_sample, state_ssd_fwd, state_ssd_bwd, c, c_ctx, g_norm1, g_norm2, w_ada, b_ada, w_in, w_conv_short, w_conv_ssd, b_conv_ssd, dt_bias, a_log, d_skip, g_ssd_norm, w_out, w_up, w_ffn_conv, b_ffn_conv, w_down, g_final):
    depth = w_in.shape[0]
    assert depth == 1, "kernel is written for the single-layer problem"
    bp, seq = x_prompt.shape[0], x_prompt.shape[1]
    bd, dec_seq = x_sample.shape[0], x_sample.shape[1]
    assert seq == PROMPT_SEQ and dec_seq == SLAB and (bp * seq) % SLAB == 0

    n_c = bd + 1
    rows = -(-n_c // 8) * 8
    cvec = jnp.concatenate([c, c_ctx[None], jnp.zeros((rows - n_c, D_MODEL), F32)], axis=0)
    mod = _mod_call(cvec, w_ada[0], b_ada[0].reshape(1, -1)).reshape(rows, N_MOD, D_MODEL)
    mod_lat = mod[:bd]
    mod_ctx = mod[bd:bd + 1]

    w_in_main = _cast_bf16(w_in[0], DT_COL)
    w_dt = jnp.pad(w_in[0][:, DT_COL:], ((0, 0), (0, LANES - 2 * SSD_HEADS))).astype(BF16)
    idx = jnp.arange(CHUNK)
    ltri = (idx[:, None] >= idx[None, :]).astype(BF16)
    utri = (idx[:, None] <= idx[None, :]).astype(BF16)
    tri3 = jnp.concatenate([jnp.concatenate([ltri] * 3, axis=1), jnp.concatenate([utri] * 3, axis=1)], axis=0)
    expand = (jnp.arange(LANES)[:, None] == (jnp.arange(2 * SSD_WIDTH)[None, :] // SSD_HEAD_DIM)).astype(BF16)
    mixer_consts = (
        g_norm1[0].reshape(1, -1), w_in_main, w_dt, w_conv_short[0], w_conv_ssd[0], b_conv_ssd[0].reshape(1, -1),
        _pad_lanes(dt_bias[0]), _pad_lanes(a_log[0]),
        jnp.repeat(d_skip[0], SSD_HEAD_DIM).reshape(1, -1), g_ssd_norm[0].reshape(1, -1),
        _cast_bf16(w_out[0], D_MODEL), tri3, expand, jnp.concatenate([expand] * 3, axis=0),
    )
    wup3, wdn3 = _ffn_blocks(w_up[0], w_down[0])
    pad_blocks = ((0, N_FF_PADDED - N_FF_BLOCKS), (0, 0), (0, 0))
    wc3 = jnp.pad(w_ffn_conv[0].reshape(9, 2, N_FF_BLOCKS, FF_BLOCK).transpose(2, 0, 1, 3)
                  .reshape(N_FF_BLOCKS, 9, 2 * FF_BLOCK), pad_blocks)
    bc3 = jnp.pad(b_ffn_conv[0].reshape(2, N_FF_BLOCKS, FF_BLOCK).transpose(1, 0, 2)
                  .reshape(N_FF_BLOCKS, 1, 2 * FF_BLOCK), pad_blocks)
    ffn_consts = (g_norm2[0].reshape(1, -1), wup3, wc3, bc3, wdn3, g_final.reshape(1, -1))

    state_rows = SSD_HEADS * SSD_HEAD_DIM
    init_f = state_ssd_fwd[:, 0].reshape(bd, state_rows, D_STATE)
    init_b = state_ssd_bwd[:, 0].reshape(bd, state_rows, D_STATE)

    xp2d = x_prompt.reshape(bp * seq, D_MODEL)
    xs2d = x_sample.reshape(bd * dec_seq, D_MODEL)

    xp1, s_f, s_b = _mixer_call(xp2d, mod_ctx, False, seq, None, None, mixer_consts)
    (xs1,) = _mixer_call(xs2d, mod_lat, True, dec_seq, init_f, init_b, mixer_consts)

    y_prompt = _ffn_call(xp1, mod_ctx, False, False, ffn_consts).reshape(x_prompt.shape)
    y_sample = _ffn_call(xs1, mod_lat, True, True, ffn_consts).reshape(x_sample.shape)

    state_shape = (bp, depth, SSD_HEADS, SSD_HEAD_DIM, D_STATE)
    return (y_prompt, y_sample, s_f.reshape(state_shape), s_b.reshape(state_shape))
```
